```python
import math
import jax, jax.numpy as jnp
from jax import lax
import numpy as np

D_MODEL = 1024
BATCH = 8
SEQ = 2048
DEPTH = 2

HEAD_DIM = 64
A_HEADS = 8
A_KV_HEADS = 2
B_HEADS = 8
B_KV_HEADS = 2
N_BRANCHES = 2
D_FF = 4 * D_MODEL
GRID_W = 64
ROPE_THETA = 10000.0
Q_BLOCK = 128
WINDOW = 128
N_BUCKETS = 32
MAX_DISTANCE = 128
EPS = 1e-6
NEG_INF = -1e30

A_Q = A_HEADS * HEAD_DIM
A_KV = A_KV_HEADS * HEAD_DIM
B_Q = B_HEADS * HEAD_DIM
B_KV = B_KV_HEADS * HEAD_DIM
PROJ_COLS = A_Q + 2 * A_KV + B_Q + 2 * B_KV + N_BRANCHES * D_MODEL

kernel_name = "gated_hybrid_axial_rope_window_sink_encoder"


def rms_norm(x, g):
    xf = x.astype(jnp.float32)
    y = xf * lax.rsqrt(jnp.mean(xf * xf, axis=-1, keepdims=True) + EPS)
    return (y * g.astype(jnp.float32)).astype(x.dtype)


def axial_rope_tables(seq):
    rows = seq // GRID_W
    row = jnp.repeat(jnp.arange(rows, dtype=jnp.int32), GRID_W)
    col = jnp.tile(jnp.arange(GRID_W, dtype=jnp.int32), rows)
    n_freq = HEAD_DIM // 4
    inv_freq = ROPE_THETA ** (-jnp.arange(n_freq, dtype=jnp.float32) / n_freq)
    ang_row = row.astype(jnp.float32)[:, None] * inv_freq[None, :]
    ang_col = col.astype(jnp.float32)[:, None] * inv_freq[None, :]
    return (jnp.cos(ang_row), jnp.sin(ang_row), jnp.cos(ang_col), jnp.sin(ang_col))


def _rotate(x, cos, sin):
    n = x.shape[-1] // 2
    x1, x2 = x[..., :n], x[..., n:]
    c, s = cos[None, :, None, :], sin[None, :, None, :]
    return jnp.concatenate([x1 * c - x2 * s, x1 * s + x2 * c], axis=-1)


def apply_axial_rope(x, tables):
    cr, sr, cc, sc = tables
    xf = x.astype(jnp.float32)
    half = HEAD_DIM // 2
    out = jnp.concatenate([_rotate(xf[..., :half], cr, sr),
                           _rotate(xf[..., half:], cc, sc)], axis=-1)
    return out.astype(x.dtype)


def t5_bucket(rel):
    nb = N_BUCKETS // 2
    max_exact = nb // 2
    n = jnp.abs(rel)
    large = max_exact + (jnp.log(jnp.maximum(n, 1).astype(jnp.float32) / max_exact)
                         / math.log(MAX_DISTANCE / max_exact) * (nb - max_exact)).astype(jnp.int32)
    large = jnp.minimum(large, nb - 1)
    return jnp.where(rel > 0, nb, 0) + jnp.where(n < max_exact, n, large)


def global_attention(q, k, v):
    b, s, h, hd = q.shape
    kvh = k.shape[2]
    g = h // kvh
    nblk = s // Q_BLOCK
    scale = hd ** -0.5
    qb = q.reshape(b, nblk, Q_BLOCK, kvh, g, hd).transpose(1, 0, 2, 3, 4, 5)

    def one_block(q_blk):
        logits = jnp.einsum('bqkgd,bskd->bkgqs', q_blk, k).astype(jnp.float32) * scale
        p = jax.nn.softmax(logits, axis=-1)
        return jnp.einsum('bkgqs,bskd->bqkgd', p.astype(v.dtype), v)

    o = lax.map(one_block, qb)
    return o.transpose(1, 0, 2, 3, 4, 5).reshape(b, s, h * hd)


def window_attention(q, k, v, bias, valid, sink):
    b, s, h, hd = q.shape
    kvh = k.shape[2]
    g = h // kvh
    nblk = s // WINDOW
    scale = hd ** -0.5
    qb = q.reshape(b, nblk, WINDOW, kvh, g, hd)
    pad = ((0, 0), (WINDOW, WINDOW), (0, 0), (0, 0))
    kp = jnp.pad(k, pad).reshape(b, nblk + 2, WINDOW, kvh, hd)
    vp = jnp.pad(v, pad).reshape(b, nblk + 2, WINDOW, kvh, hd)
    kb = jnp.concatenate([kp[:, :-2], kp[:, 1:-1], kp[:, 2:]], axis=2)
    vb = jnp.concatenate([vp[:, :-2], vp[:, 1:-1], vp[:, 2:]], axis=2)
    logits = jnp.einsum('bnqkgd,bnskd->bnkgqs', qb, kb).astype(jnp.float32) * scale
    logits = logits + bias.reshape(kvh, g, WINDOW, 3 * WINDOW)[None, None]
    logits = jnp.where(valid[None, :, None, None], logits, NEG_INF)
    sink_l = sink.astype(jnp.float32).reshape(kvh, g)[None, None, :, :, None, None]
    m = jnp.maximum(jnp.max(logits, axis=-1, keepdims=True), sink_l)
    p = jnp.exp(logits - m)
    denom = jnp.sum(p, axis=-1, keepdims=True) + jnp.exp(sink_l - m)
    p = (p / denom).astype(v.dtype)
    o = jnp.einsum('bnkgqs,bnskd->bnqkgd', p, vb)
    return o.reshape(b, s, h * hd)


def setup_inputs(seed: int = 0) -> dict:
    key = jax.random.key(seed)
    ks = jax.random.split(key, 16)
    f32 = jnp.float32
    nrm = lambda k, shape, sc: jax.random.normal(k, shape, f32) * sc
    return {
        "x": nrm(ks[0], (BATCH, SEQ, D_MODEL), 1.0),
        "w_in": nrm(ks[1], (DEPTH, D_MODEL, PROJ_COLS), D_MODEL ** -0.5),
        "b_gate": nrm(ks[2], (DEPTH, N_BRANCHES * D_MODEL), 0.1),
        "qn_a": 1.0 + nrm(ks[3], (DEPTH, HEAD_DIM), 0.02),
        "kn_a": 1.0 + nrm(ks[4], (DEPTH, HEAD_DIM), 0.02),
        "qn_b": 1.0 + nrm(ks[5], (DEPTH, HEAD_DIM), 0.02),
        "kn_b": 1.0 + nrm(ks[6], (DEPTH, HEAD_DIM), 0.02),
        "w_o_a": nrm(ks[7], (DEPTH, A_Q, D_MODEL), A_Q ** -0.5),
        "w_o_b": nrm(ks[8], (DEPTH, B_Q, D_MODEL), B_Q ** -0.5),
        "w_out": nrm(ks[9], (DEPTH, D_MODEL, D_MODEL), D_MODEL ** -0.5),
        "sink_b": nrm(ks[10], (DEPTH, B_HEADS), 0.5),
        "rel_bias": nrm(ks[11], (N_BUCKETS, B_HEADS), 0.5),
        "norm_mix": 1.0 + nrm(ks[12], (DEPTH, D_MODEL), 0.02),
        "norm_mlp": 1.0 + nrm(ks[13], (DEPTH, D_MODEL), 0.02),
        "w_mlp1": nrm(ks[14], (DEPTH, D_MODEL, D_FF), D_MODEL ** -0.5),
        "w_mlp2": nrm(ks[15], (DEPTH, D_FF, D_MODEL), D_FF ** -0.5),
    }


def reference(x, w_in, b_gate, qn_a, kn_a, qn_b, kn_b, w_o_a, w_o_b, w_out,
              sink_b, rel_bias, norm_mix, norm_mlp, w_mlp1, w_mlp2):
    b, s, d = x.shape
    nblk = s // WINDOW
    rope = axial_rope_tables(s)

    r = jnp.arange(WINDOW, dtype=jnp.int32)[:, None]
    j = jnp.arange(3 * WINDOW, dtype=jnp.int32)[None, :]
    rel = j - WINDOW - r
    kpos = (jnp.arange(nblk, dtype=jnp.int32)[:, None, None] * WINDOW + j[None] - WINDOW)
    valid = (jnp.abs(rel)[None] <= WINDOW) & (kpos >= 0) & (kpos < s)
    bias_b = jnp.transpose(rel_bias.astype(jnp.float32)[t5_bucket(rel)], (2, 0, 1))

    splits = [A_Q, A_Q + A_KV, A_Q + 2 * A_KV,
              A_Q + 2 * A_KV + B_Q, A_Q + 2 * A_KV + B_Q + B_KV,
              A_Q + 2 * A_KV + B_Q + 2 * B_KV]

    for l in range(DEPTH):
        h = rms_norm(x, norm_mix[l])
        proj = jnp.einsum('bsd,dc->bsc', h, w_in[l])
        qa, ka, va, qb, kb, vb, gates = jnp.split(proj, splits, axis=-1)

        qa = apply_axial_rope(rms_norm(qa.reshape(b, s, A_HEADS, HEAD_DIM), qn_a[l]), rope)
        ka = apply_axial_rope(rms_norm(ka.reshape(b, s, A_KV_HEADS, HEAD_DIM), kn_a[l]), rope)
        va = va.reshape(b, s, A_KV_HEADS, HEAD_DIM)
        oa = global_attention(qa, ka, va)

        qb = rms_norm(qb.reshape(b, s, B_HEADS, HEAD_DIM), qn_b[l])
        kb = rms_norm(kb.reshape(b, s, B_KV_HEADS, HEAD_DIM), kn_b[l])
        vb = vb.reshape(b, s, B_KV_HEADS, HEAD_DIM)
        ob = window_attention(qb, kb, vb, bias_b, valid, sink_b[l])

        g = jax.nn.sigmoid((gates + b_gate[l]).astype(jnp.float32)).astype(x.dtype)
        g = g.reshape(b, s, N_BRANCHES, d)
        ya = jnp.einsum('bsc,cd->bsd', oa, w_o_a[l])
        yb = jnp.einsum('bsc,cd->bsd', ob, w_o_b[l])
        mixed = g[:, :, 0] * ya + g[:, :, 1] * yb
        x = x + jnp.einsum('bsd,de->bse', mixed, w_out[l])

        h = rms_norm(x, norm_mlp[l])
        u = jnp.square(jax.nn.relu(jnp.einsum('bsd,df->bsf', h, w_mlp1[l])))
        x = x + jnp.einsum('bsf,fd->bsd', u, w_mlp2[l])
    return x
```

```python
import math
from functools import partial

import jax
import jax.numpy as jnp
from jax import lax
from jax.experimental import pallas as pl
from jax.experimental.pallas import tpu as pltpu

F32 = jnp.float32
BF16 = jnp.bfloat16

HEAD_DIM = 64
N_HEADS = 8
N_KV = 2
GROUP = N_HEADS // N_KV
GRID_W = 64
ROPE_THETA = 10000.0
WINDOW = 128
N_BUCKETS = 32
MAX_DISTANCE = 128
EPS = 1e-6
NEG_INF = -1e30

LANES = 128
VMEM_LIMIT = 56 * 1024 * 1024

QA_CHUNK, KA_CHUNK, VA_CHUNK = 0, 8, 10
QB_CHUNK, KB_CHUNK, VB_CHUNK = 12, 20, 22
SLAB_CHUNKS = 24


def _rms(x, gain):
    ms = jnp.mean(x * x, axis=-1, keepdims=True)
    return x * lax.rsqrt(ms + EPS) * gain


def _qkv_body(x_ref, gn_ref, w_ref, gain_ref, cos_ref, sin_ref, o_ref):
    tm = x_ref.shape[0]
    h = _rms(x_ref[...], gn_ref[...]).astype(BF16)
    p = jnp.dot(h, w_ref[...], preferred_element_type=F32)
    lane = lax.broadcasted_iota(jnp.int32, (tm, LANES), 1)
    lo = lane < HEAD_DIM
    first16 = (lane & 16) == 0
    cos = cos_ref[...]
    sin = sin_ref[...]

    def chunk(c):
        return p[:, c * LANES:(c + 1) * LANES]

    def head_norm(c, gain):
        sq = c * c
        s_lo = jnp.sum(jnp.where(lo, sq, 0.0), axis=-1, keepdims=True)
        s_hi = jnp.sum(jnp.where(lo, 0.0, sq), axis=-1, keepdims=True)
        ms = jnp.where(lo, s_lo, s_hi) * (1.0 / HEAD_DIM)
        return c * lax.rsqrt(ms + EPS) * gain

    def rope(c):
        ahead = pltpu.roll(c, LANES - 16, 1)
        behind = pltpu.roll(c, 16, 1)
        return c * cos + jnp.where(first16, ahead, behind) * sin

    def put(idx, val):
        o_ref[:, idx * LANES:(idx + 1) * LANES] = val.astype(BF16)

    def put_q(base, c, val):
        put(base + 2 * c, jnp.where(lo, val, 0.0))
        put(base + 2 * c + 1, jnp.where(lo, 0.0, val))

    def put_dup(base, val):
        sw = pltpu.roll(val, HEAD_DIM, 1)
        put(base, jnp.where(lo, val, sw))
        put(base + 1, jnp.where(lo, sw, val))

    scale = HEAD_DIM ** -0.5
    qn_a, kn_a, qn_b, kn_b = (gain_ref[i:i + 1, :] for i in range(4))
    for c in range(4):
        put_q(QA_CHUNK, c, rope(head_norm(chunk(c), qn_a)) * scale)
    put_dup(KA_CHUNK, rope(head_norm(chunk(4), kn_a)))
    put_dup(VA_CHUNK, chunk(5))
    for c in range(4):
        put_q(QB_CHUNK, c, head_norm(chunk(6 + c), qn_b) * scale)
    put_dup(KB_CHUNK, head_norm(chunk(10), kn_b))
    put_dup(VB_CHUNK, chunk(11))


def _qkv_call(x2, gn, w_qkv, gains, cos_t, sin_t, seq, tm):
    n, d = x2.shape
    nseq = seq // tm
    return pl.pallas_call(
        _qkv_body,
        grid=(n // tm,),
        in_specs=[
            pl.BlockSpec((tm, d), lambda i: (i, 0)),
            pl.BlockSpec((1, d), lambda i: (0, 0)),
            pl.BlockSpec(w_qkv.shape, lambda i: (0, 0)),
            pl.BlockSpec(gains.shape, lambda i: (0, 0)),
            pl.BlockSpec((tm, LANES), lambda i: (i % nseq, 0)),
            pl.BlockSpec((tm, LANES), lambda i: (i % nseq, 0)),
        ],
        out_specs=pl.BlockSpec((tm, SLAB_CHUNKS * LANES), lambda i: (i, 0)),
        out_shape=jax.ShapeDtypeStruct((n, SLAB_CHUNKS * LANES), BF16),
        compiler_params=pltpu.CompilerParams(
            dimension_semantics=("arbitrary",), vmem_limit_bytes=VMEM_LIMIT),
        name="qkv_proj",
    )(x2, gn, w_qkv, gains, cos_t, sin_t)


def _stack_heads(q_ref):
    return jnp.concatenate(
        [q_ref[:, k * LANES:(k + 1) * LANES] for k in range(GROUP)], axis=0)


def _store_heads(o_ref, o, tq):
    lane = lax.broadcasted_iota(jnp.int32, (tq, LANES), 1)
    lo = lane < HEAD_DIM
    for c in range(GROUP // 2):
        even = o[(2 * c) * tq:(2 * c + 1) * tq]
        odd = o[(2 * c + 1) * tq:(2 * c + 2) * tq]
        o_ref[:, c * LANES:(c + 1) * LANES] = jnp.where(lo, even, odd).astype(o_ref.dtype)


_NT = (((1,), (1,)), ((), ()))


def _global_attn_body(q_ref, k_ref, v_ref, o_ref):
    tq = q_ref.shape[0]
    q4 = _stack_heads(q_ref)
    s = lax.dot_general(q4, k_ref[...], _NT, preferred_element_type=F32)
    m = jnp.max(s, axis=-1, keepdims=True)
    p = jnp.exp(s - m)
    l = jnp.sum(p, axis=-1, keepdims=True)
    o = jnp.dot(p.astype(BF16), v_ref[...], preferred_element_type=F32)
    _store_heads(o_ref, o / l, tq)


def _global_attn_call(slab, batch, seq, tq):
    n = slab.shape[0]
    nq = seq // tq
    qw = GROUP * LANES
    return pl.pallas_call(
        _global_attn_body,
        grid=(batch, N_KV, nq),
        in_specs=[
            pl.BlockSpec((tq, qw), lambda b, g, i: (b * nq + i, QA_CHUNK // GROUP + g)),
            pl.BlockSpec((seq, LANES), lambda b, g, i: (b, KA_CHUNK + g)),
            pl.BlockSpec((seq, LANES), lambda b, g, i: (b, VA_CHUNK + g)),
        ],
        out_specs=pl.BlockSpec((tq, qw // 2), lambda b, g, i: (b * nq + i, g)),
        out_shape=jax.ShapeDtypeStruct((n, N_HEADS * HEAD_DIM), BF16),
        compiler_params=pltpu.CompilerParams(
            dimension_semantics=("arbitrary",) * 3, vmem_limit_bytes=VMEM_LIMIT),
        name="global_attn",
    )(slab, slab, slab)


def _window_attn_body(q_ref, k_ref, v_ref, bias_ref, sink_ref, o_ref):
    w = WINDOW
    seq = k_ref.shape[0]
    nblk = seq // w
    i = pl.program_id(2)
    q4 = _stack_heads(q_ref)

    def band(ref):
        tiles = []
        for t in (-1, 0, 1):
            start = pl.multiple_of(jnp.clip(i + t, 0, nblk - 1) * w, w)
            tiles.append(ref[pl.ds(start, w), :])
        return jnp.concatenate(tiles, axis=0)

    s = lax.dot_general(q4, band(k_ref), _NT, preferred_element_type=F32)
    s = s + bias_ref[...]
    r = lax.broadcasted_iota(jnp.int32, s.shape, 0) & (w - 1)
    j = lax.broadcasted_iota(jnp.int32, s.shape, 1)
    rel = j - w - r
    kpos = i * w + j - w
    valid = (jnp.abs(rel) <= w) & (kpos >= 0) & (kpos < seq)
    s = jnp.where(valid, s, NEG_INF)
    sink = sink_ref[...]
    m = jnp.maximum(jnp.max(s, axis=-1, keepdims=True), sink)
    p = jnp.exp(s - m)
    denom = jnp.sum(p, axis=-1, keepdims=True) + jnp.exp(sink - m)
    o = jnp.dot(p.astype(BF16), band(v_ref), preferred_element_type=F32)
    _store_heads(o_ref, o / denom, w)


def _window_attn_call(slab, bias, sink, batch, seq):
    n = slab.shape[0]
    w = WINDOW
    nq = seq // w
    qw = GROUP * LANES
    return pl.pallas_call(
        _window_attn_body,
        grid=(batch, N_KV, nq),
        in_specs=[
            pl.BlockSpec((w, qw), lambda b, g, i: (b * nq + i, QB_CHUNK // GROUP + g)),
            pl.BlockSpec((seq, LANES), lambda b, g, i: (b, KB_CHUNK + g)),
            pl.BlockSpec((seq, LANES), lambda b, g, i: (b, VB_CHUNK + g)),
            pl.BlockSpec((None, GROUP * w, 3 * w), lambda b, g, i: (g, 0, 0)),
            pl.BlockSpec((None, GROUP * w, 1), lambda b, g, i: (g, 0, 0)),
        ],
        out_specs=pl.BlockSpec((w, qw // 2), lambda b, g, i: (b * nq + i, g)),
        out_shape=jax.ShapeDtypeStruct((n, N_HEADS * HEAD_DIM), BF16),
        compiler_params=pltpu.CompilerParams(
            dimension_semantics=("arbitrary",) * 3, vmem_limit_bytes=VMEM_LIMIT),
        name="window_attn",
    )(slab, slab, slab, bias, sink)


def _mix_mlp_body(x_ref, oa_ref, ob_ref, gn_ref, wg_ref, bg_ref, woa_ref, wob_ref, wout_ref,
                  gm_ref, w1_ref, w2_ref, o_ref):
    d = x_ref.shape[1]
    x = x_ref[...]
    h = _rms(x, gn_ref[...]).astype(BF16)
    gates = jnp.dot(h, wg_ref[...], preferred_element_type=F32) + bg_ref[...]
    g = jax.nn.sigmoid(gates)
    ya = jnp.dot(oa_ref[...], woa_ref[...], preferred_element_type=F32)
    yb = jnp.dot(ob_ref[...], wob_ref[...], preferred_element_type=F32)
    mixed = g[:, :d] * ya + g[:, d:] * yb
    x = x + jnp.dot(mixed.astype(BF16), wout_ref[...], preferred_element_type=F32)
    h = _rms(x, gm_ref[...]).astype(BF16)
    u = jnp.square(jnp.maximum(jnp.dot(h, w1_ref[...], preferred_element_type=F32), 0.0))
    o_ref[...] = x + jnp.dot(u.astype(BF16), w2_ref[...], preferred_element_type=F32)


def _resident(arr):
    return pl.BlockSpec(arr.shape, lambda i: (0,) * arr.ndim, pipeline_mode=pl.Buffered(1))


def _mix_mlp_call(x2, oa, ob, gn, wg, bg, woa, wob, wout, gm, w1, w2, tm):
    n, d = x2.shape
    row = lambda width: pl.BlockSpec((tm, width), lambda i: (i, 0))
    consts = (gn, wg, bg, woa, wob, wout, gm, w1, w2)
    return pl.pallas_call(
        _mix_mlp_body,
        grid=(n // tm,),
        in_specs=[row(d), row(oa.shape[1]), row(ob.shape[1])] + [_resident(a) for a in consts],
        out_specs=row(d),
        out_shape=jax.ShapeDtypeStruct((n, d), F32),
        compiler_params=pltpu.CompilerParams(
            dimension_semantics=("arbitrary",), vmem_limit_bytes=VMEM_LIMIT),
        name="mix_mlp",
    )(x2, oa, ob, *consts)


def _rope_tables(seq):
    rows = seq // GRID_W
    row = jnp.repeat(jnp.arange(rows, dtype=jnp.int32), GRID_W)
    col = jnp.tile(jnp.arange(GRID_W, dtype=jnp.int32), rows)
    n_freq = HEAD_DIM // 4
    inv_freq = ROPE_THETA ** (-jnp.arange(n_freq, dtype=F32) / n_freq)
    ang_row = row.astype(F32)[:, None] * inv_freq[None, :]
    ang_col = col.astype(F32)[:, None] * inv_freq[None, :]
    cr, sr, cc, sc = jnp.cos(ang_row), jnp.sin(ang_row), jnp.cos(ang_col), jnp.sin(ang_col)
    cos64 = jnp.concatenate([cr, cr, cc, cc], axis=-1)
    sin64 = jnp.concatenate([-sr, sr, -sc, sc], axis=-1)
    return jnp.tile(cos64, (1, 2)), jnp.tile(sin64, (1, 2))


def _t5_bucket(rel):
    nb = N_BUCKETS // 2
    max_exact = nb // 2
    n = jnp.abs(rel)
    large = max_exact + (jnp.log(jnp.maximum(n, 1).astype(F32) / max_exact)
                         / math.log(MAX_DISTANCE / max_exact) * (nb - max_exact)).astype(jnp.int32)
    large = jnp.minimum(large, nb - 1)
    return jnp.where(rel > 0, nb, 0) + jnp.where(n < max_exact, n, large)


def _window_bias(rel_bias):
    w = WINDOW
    r = jnp.arange(w, dtype=jnp.int32)[:, None]
    j = jnp.arange(3 * w, dtype=jnp.int32)[None, :]
    bias = jnp.transpose(rel_bias.astype(F32)[_t5_bucket(j - w - r)], (2, 0, 1))
    return bias.reshape(N_KV, GROUP * w, 3 * w)


def kernel(x, w_in, b_gate, qn_a, kn_a, qn_b, kn_b, w_o_a, w_o_b, w_out,
           sink_b, rel_bias, norm_mix, norm_mlp, w_mlp1, w_mlp2):
    batch, seq, d = x.shape
    depth = w_in.shape[0]
    n_qkv = 2 * (N_HEADS + 2 * N_KV) * HEAD_DIM
    cos_t, sin_t = _rope_tables(seq)
    bias = _window_bias(rel_bias)
    x2 = x.reshape(batch * seq, d)
    pair = lambda v: jnp.tile(v, 2)
    for l in range(depth):
        gains = jnp.zeros((8, LANES), F32).at[:4].set(
            jnp.stack([pair(qn_a[l]), pair(kn_a[l]), pair(qn_b[l]), pair(kn_b[l])]))
        slab = _qkv_call(x2, norm_mix[l][None, :], w_in[l, :, :n_qkv].astype(BF16),
                         gains, cos_t, sin_t, seq, tm=256)
        oa = _global_attn_call(slab, batch, seq, tq=256)
        sink = jnp.repeat(sink_b[l].astype(F32), WINDOW).reshape(N_KV, GROUP * WINDOW, 1)
        ob = _window_attn_call(slab, bias, sink, batch, seq)
        x2 = _mix_mlp_call(
            x2, oa, ob, norm_mix[l][None, :], w_in[l, :, n_qkv:].astype(BF16),
            b_gate[l][None, :], w_o_a[l].astype(BF16), w_o_b[l].astype(BF16),
            w_out[l].astype(BF16), norm_mlp[l][None, :], w_mlp1[l].astype(BF16),
            w_mlp2[l].astype(BF16), tm=256)
    return x2.reshape(batch, seq, d)
```

```python
import math
from functools import partial

import jax
import jax.numpy as jnp
from jax import lax
from jax.experimental import pallas as pl
from jax.experimental.pallas import tpu as pltpu

F32 = jnp.float32
BF16 = jnp.bfloat16

HEAD_DIM = 64
N_HEADS = 8
N_KV = 2
GROUP = N_HEADS // N_KV
GRID_W = 64
ROPE_THETA = 10000.0
WINDOW = 128
N_BUCKETS = 32
MAX_DISTANCE = 128
EPS = 1e-6
NEG_INF = -1e30

LANES = 128
VMEM_LIMIT = 56 * 1024 * 1024

QA_CHUNK, KA_CHUNK, VA_CHUNK = 0, 8, 10
QB_CHUNK, KB_CHUNK, VB_CHUNK = 12, 20, 22
SLAB_CHUNKS = 24


def _rms(x, gain):
    ms = jnp.mean(x * x, axis=-1, keepdims=True)
    return x * lax.rsqrt(ms + EPS) * gain


def _qkv_body(x_ref, gn_ref, w_ref, gain_ref, cos_ref, sin_ref, o_ref):
    tm = x_ref.shape[0]
    h = _rms(x_ref[...], gn_ref[...]).astype(BF16)
    p = jnp.dot(h, w_ref[...], preferred_element_type=F32)
    lane = lax.broadcasted_iota(jnp.int32, (tm, LANES), 1)
    lo = lane < HEAD_DIM
    first16 = (lane & 16) == 0
    cos = cos_ref[...]
    sin = sin_ref[...]

    def chunk(c):
        return p[:, c * LANES:(c + 1) * LANES]

    def head_norm(c, gain):
        sq = c * c
        s_lo = jnp.sum(jnp.where(lo, sq, 0.0), axis=-1, keepdims=True)
        s_hi = jnp.sum(jnp.where(lo, 0.0, sq), axis=-1, keepdims=True)
        ms = jnp.where(lo, s_lo, s_hi) * (1.0 / HEAD_DIM)
        return c * lax.rsqrt(ms + EPS) * gain

    def rope(c):
        ahead = pltpu.roll(c, LANES - 16, 1)
        behind = pltpu.roll(c, 16, 1)
        return c * cos + jnp.where(first16, ahead, behind) * sin

    def put(idx, val):
        o_ref[:, idx * LANES:(idx + 1) * LANES] = val.astype(BF16)

    def put_q(base, c, val):
        put(base + 2 * c, jnp.where(lo, val, 0.0))
        put(base + 2 * c + 1, jnp.where(lo, 0.0, val))

    def put_dup(base, val):
        sw = pltpu.roll(val, HEAD_DIM, 1)
        put(base, jnp.where(lo, val, sw))
        put(base + 1, jnp.where(lo, sw, val))

    scale = HEAD_DIM ** -0.5
    qn_a, kn_a, qn_b, kn_b = (gain_ref[i:i + 1, :] for i in range(4))
    for c in range(4):
        put_q(QA_CHUNK, c, rope(head_norm(chunk(c), qn_a)) * scale)
    put_dup(KA_CHUNK, rope(head_norm(chunk(4), kn_a)))
    put_dup(VA_CHUNK, chunk(5))
    for c in range(4):
        put_q(QB_CHUNK, c, head_norm(chunk(6 + c), qn_b) * scale)
    put_dup(KB_CHUNK, head_norm(chunk(10), kn_b))
    put_dup(VB_CHUNK, chunk(11))


def _qkv_call(x2, gn, w_qkv, gains, cos_t, sin_t, seq, tm):
    n, d = x2.shape
    nseq = seq // tm
    return pl.pallas_call(
        _qkv_body,
        grid=(n // tm,),
        in_specs=[
            pl.BlockSpec((tm, d), lambda i: (i, 0)),
            pl.BlockSpec((1, d), lambda i: (0, 0)),
            pl.BlockSpec(w_qkv.shape, lambda i: (0, 0)),
            pl.BlockSpec(gains.shape, lambda i: (0, 0)),
            pl.BlockSpec((tm, LANES), lambda i: (i % nseq, 0)),
            pl.BlockSpec((tm, LANES), lambda i: (i % nseq, 0)),
        ],
        out_specs=pl.BlockSpec((tm, SLAB_CHUNKS * LANES), lambda i: (i, 0)),
        out_shape=jax.ShapeDtypeStruct((n, SLAB_CHUNKS * LANES), BF16),
        compiler_params=pltpu.CompilerParams(
            dimension_semantics=("arbitrary",), vmem_limit_bytes=VMEM_LIMIT),
        name="qkv_proj",
    )(x2, gn, w_qkv, gains, cos_t, sin_t)


ROW_BLOCK = 128


def _stack_heads(q_ref, row0):
    return jnp.concatenate(
        [q_ref[pl.ds(row0, ROW_BLOCK), k * LANES:(k + 1) * LANES] for k in range(GROUP)], axis=0)


def _store_heads(o_ref, row0, o):
    rb = ROW_BLOCK
    lane = lax.broadcasted_iota(jnp.int32, (rb, LANES), 1)
    lo = lane < HEAD_DIM
    for c in range(GROUP // 2):
        even = o[(2 * c) * rb:(2 * c + 1) * rb]
        odd = o[(2 * c + 1) * rb:(2 * c + 2) * rb]
        o_ref[pl.ds(row0, rb), c * LANES:(c + 1) * LANES] = (
            jnp.where(lo, even, odd).astype(o_ref.dtype))


_NT = (((1,), (1,)), ((), ()))


def _global_attn_body(q_ref, k_ref, v_ref, o_ref, s_buf, p_buf, m_buf, l_buf):
    rb = ROW_BLOCK
    n_blocks = q_ref.shape[0] // rb

    def row(j):
        return pl.multiple_of(j * rb, rb)

    def scores(j, slot):
        s = lax.dot_general(_stack_heads(q_ref, row(j)), k_ref[...], _NT,
                            preferred_element_type=F32)
        s_buf[slot] = s
        m_buf[slot] = jnp.max(s, axis=-1, keepdims=True)

    def probs(slot):
        p = jnp.exp(s_buf[slot] - m_buf[slot])
        l_buf[slot] = jnp.sum(p, axis=-1, keepdims=True)
        p_buf[slot] = p.astype(BF16)

    def output(j, slot):
        o = jnp.dot(p_buf[slot], v_ref[...], preferred_element_type=F32)
        _store_heads(o_ref, row(j), o / l_buf[slot])

    scores(0, 0)
    scores(1, 1)
    probs(0)

    def steady(jj, carry):
        j = 2 + 2 * jj
        scores(j, 0)
        probs(1)
        output(j - 2, 0)
        scores(j + 1, 1)
        probs(0)
        output(j - 1, 1)
        return carry

    lax.fori_loop(0, (n_blocks - 2) // 2, steady, 0)
    probs(1)
    output(n_blocks - 2, 0)
    output(n_blocks - 1, 1)


def _global_attn_call(slab, batch, seq):
    n = slab.shape[0]
    qw = GROUP * LANES
    rows = GROUP * ROW_BLOCK
    assert (seq // ROW_BLOCK) % 2 == 0
    return pl.pallas_call(
        _global_attn_body,
        grid=(batch, N_KV),
        in_specs=[
            pl.BlockSpec((seq, qw), lambda b, g: (b, QA_CHUNK // GROUP + g)),
            pl.BlockSpec((seq, LANES), lambda b, g: (b, KA_CHUNK + g)),
            pl.BlockSpec((seq, LANES), lambda b, g: (b, VA_CHUNK + g)),
        ],
        out_specs=pl.BlockSpec((seq, qw // 2), lambda b, g: (b, g)),
        out_shape=jax.ShapeDtypeStruct((n, N_HEADS * HEAD_DIM), BF16),
        scratch_shapes=[
            pltpu.VMEM((2, rows, seq), F32),
            pltpu.VMEM((2, rows, seq), BF16),
            pltpu.VMEM((2, rows, 1), F32),
            pltpu.VMEM((2, rows, 1), F32),
        ],
        compiler_params=pltpu.CompilerParams(
            dimension_semantics=("arbitrary",) * 2, vmem_limit_bytes=VMEM_LIMIT),
        name="global_attn",
    )(slab, slab, slab)


WINDOW_BLOCKS = 4


def _window_attn_body(q_ref, k_ref, v_ref, bias_ref, sink_ref, o_ref):
    w = WINDOW
    nb = WINDOW_BLOCKS
    seq = k_ref.shape[0]
    nblk = seq // w
    i = pl.program_id(2)
    n_steps = pl.num_programs(2)
    bias = bias_ref[...]
    sink = sink_ref[...]

    def tiles(ref):
        out = []
        for t in range(-1, nb + 1):
            start = pl.multiple_of(jnp.clip(nb * i + t, 0, nblk - 1) * w, w)
            out.append(ref[pl.ds(start, w), :])
        return out

    kt = tiles(k_ref)
    vt = tiles(v_ref)
    for jb in range(nb):
        q4 = _stack_heads(q_ref, jb * w)
        kw = jnp.concatenate(kt[jb:jb + 3], axis=0)
        vw = jnp.concatenate(vt[jb:jb + 3], axis=0)
        s = lax.dot_general(q4, kw, _NT, preferred_element_type=F32) + bias
        cols = [s[:, t * w:(t + 1) * w] for t in range(3)]
        if jb == 0:
            cols[0] = jnp.where(i > 0, cols[0], NEG_INF)
        if jb == nb - 1:
            cols[2] = jnp.where(i < n_steps - 1, cols[2], NEG_INF)
        m = jnp.maximum(jnp.maximum(cols[0], cols[1]), cols[2])
        m = jnp.maximum(jnp.max(m, axis=-1, keepdims=True), sink)
        ps = [jnp.exp(c - m) for c in cols]
        denom = jnp.sum(ps[0] + ps[1] + ps[2], axis=-1, keepdims=True) + jnp.exp(sink - m)
        p = jnp.concatenate([c.astype(BF16) for c in ps], axis=1)
        o = jnp.dot(p, vw, preferred_element_type=F32)
        _store_heads(o_ref, jb * w, o / denom)


def _window_attn_call(slab, bias, sink, batch, seq):
    n = slab.shape[0]
    w = WINDOW
    tq = WINDOW_BLOCKS * w
    nq = seq // tq
    qw = GROUP * LANES
    return pl.pallas_call(
        _window_attn_body,
        grid=(batch, N_KV, nq),
        in_specs=[
            pl.BlockSpec((tq, qw), lambda b, g, i: (b * nq + i, QB_CHUNK // GROUP + g)),
            pl.BlockSpec((seq, LANES), lambda b, g, i: (b, KB_CHUNK + g)),
            pl.BlockSpec((seq, LANES), lambda b, g, i: (b, VB_CHUNK + g)),
            pl.BlockSpec((None, GROUP * w, 3 * w), lambda b, g, i: (g, 0, 0)),
            pl.BlockSpec((None, GROUP * w, 1), lambda b, g, i: (g, 0, 0)),
        ],
        out_specs=pl.BlockSpec((tq, qw // 2), lambda b, g, i: (b * nq + i, g)),
        out_shape=jax.ShapeDtypeStruct((n, N_HEADS * HEAD_DIM), BF16),
        compiler_params=pltpu.CompilerParams(
            dimension_semantics=("arbitrary",) * 3, vmem_limit_bytes=VMEM_LIMIT),
        name="window_attn",
    )(slab, slab, slab, bias, sink)


def _mix_mlp_body(x_ref, oa_ref, ob_ref, gn_ref, wg_ref, bg_ref, woa_ref, wob_ref, wout_ref,
                  gm_ref, w1_ref, w2_ref, o_ref):
    d = x_ref.shape[1]
    x = x_ref[...]
    h = _rms(x, gn_ref[...]).astype(BF16)
    gates = jnp.dot(h, wg_ref[...], preferred_element_type=F32) + bg_ref[...]
    g = jax.nn.sigmoid(gates)
    ya = jnp.dot(oa_ref[...], woa_ref[...], preferred_element_type=F32)
    yb = jnp.dot(ob_ref[...], wob_ref[...], preferred_element_type=F32)
    mixed = g[:, :d] * ya + g[:, d:] * yb
    x = x + jnp.dot(mixed.astype(BF16), wout_ref[...], preferred_element_type=F32)
    h = _rms(x, gm_ref[...]).astype(BF16)
    u = jnp.square(jnp.maximum(jnp.dot(h, w1_ref[...], preferred_element_type=F32), 0.0))
    o_ref[...] = x + jnp.dot(u.astype(BF16), w2_ref[...], preferred_element_type=F32)


def _resident(arr):
    return pl.BlockSpec(arr.shape, lambda i: (0,) * arr.ndim, pipeline_mode=pl.Buffered(1))


def _mix_mlp_call(x2, oa, ob, gn, wg, bg, woa, wob, wout, gm, w1, w2, tm):
    n, d = x2.shape
    row = lambda width: pl.BlockSpec((tm, width), lambda i: (i, 0))
    consts = (gn, wg, bg, woa, wob, wout, gm, w1, w2)
    return pl.pallas_call(
        _mix_mlp_body,
        grid=(n // tm,),
        in_specs=[row(d), row(oa.shape[1]), row(ob.shape[1])] + [_resident(a) for a in consts],
        out_specs=row(d),
        out_shape=jax.ShapeDtypeStruct((n, d), F32),
        compiler_params=pltpu.CompilerParams(
            dimension_semantics=("arbitrary",), vmem_limit_bytes=VMEM_LIMIT),
        name="mix_mlp",
    )(x2, oa, ob, *consts)


def _rope_tables(seq):
    rows = seq // GRID_W
    row = jnp.repeat(jnp.arange(rows, dtype=jnp.int32), GRID_W)
    col = jnp.tile(jnp.arange(GRID_W, dtype=jnp.int32), rows)
    n_freq = HEAD_DIM // 4
    inv_freq = ROPE_THETA ** (-jnp.arange(n_freq, dtype=F32) / n_freq)
    ang_row = row.astype(F32)[:, None] * inv_freq[None, :]
    ang_col = col.astype(F32)[:, None] * inv_freq[None, :]
    cr, sr, cc, sc = jnp.cos(ang_row), jnp.sin(ang_row), jnp.cos(ang_col), jnp.sin(ang_col)
    cos64 = jnp.concatenate([cr, cr, cc, cc], axis=-1)
    sin64 = jnp.concatenate([-sr, sr, -sc, sc], axis=-1)
    return jnp.tile(cos64, (1, 2)), jnp.tile(sin64, (1, 2))


def _t5_bucket(rel):
    nb = N_BUCKETS // 2
    max_exact = nb // 2
    n = jnp.abs(rel)
    large = max_exact + (jnp.log(jnp.maximum(n, 1).astype(F32) / max_exact)
                         / math.log(MAX_DISTANCE / max_exact) * (nb - max_exact)).astype(jnp.int32)
    large = jnp.minimum(large, nb - 1)
    return jnp.where(rel > 0, nb, 0) + jnp.where(n < max_exact, n, large)


def _window_bias(rel_bias):
    w = WINDOW
    r = jnp.arange(w, dtype=jnp.int32)[:, None]
    j = jnp.arange(3 * w, dtype=jnp.int32)[None, :]
    rel = j - w - r
    bucket = _t5_bucket(rel)
    hit = bucket[None, :, :, None] == jnp.arange(N_BUCKETS, dtype=jnp.int32)
    bias = jnp.sum(jnp.where(hit, rel_bias.astype(F32).T[:, None, None, :], 0.0), axis=-1)
    bias = jnp.where(jnp.abs(rel)[None] <= w, bias, NEG_INF)
    return bias.reshape(N_KV, GROUP * w, 3 * w)


def kernel(x, w_in, b_gate, qn_a, kn_a, qn_b, kn_b, w_o_a, w_o_b, w_out,
           sink_b, rel_bias, norm_mix, norm_mlp, w_mlp1, w_mlp2):
    batch, seq, d = x.shape
    depth = w_in.shape[0]
    n_qkv = 2 * (N_HEADS + 2 * N_KV) * HEAD_DIM
    cos_t, sin_t = _rope_tables(seq)
    bias = _window_bias(rel_bias)
    x2 = x.reshape(batch * seq, d)
    pair = lambda v: jnp.tile(v, 2)
    for l in range(depth):
        gains = jnp.zeros((8, LANES), F32).at[:4].set(
            jnp.stack([pair(qn_a[l]), pair(kn_a[l]), pair(qn_b[l]), pair(kn_b[l])]))
        slab = _qkv_call(x2, norm_mix[l][None, :], w_in[l, :, :n_qkv].astype(BF16),
                         gains, cos_t, sin_t, seq, tm=256)
        oa = _global_attn_call(slab, batch, seq)
        sink = jnp.repeat(sink_b[l].astype(F32), WINDOW).reshape(N_KV, GROUP * WINDOW, 1)
        ob = _window_attn_call(slab, bias, sink, batch, seq)
        x2 = _mix_mlp_call(
            x2, oa, ob, norm_mix[l][None, :], w_in[l, :, n_qkv:].astype(BF16),
            b_gate[l][None, :], w_o_a[l].astype(BF16), w_o_b[l].astype(BF16),
            w_out[l].astype(BF16), norm_mlp[l][None, :], w_mlp1[l].astype(BF16),
            w_mlp2[l].astype(BF16), tm=256)
    return x2.reshape(batch, seq, d)
```

```python
import math
from functools import partial

import jax
import jax.numpy as jnp
from jax import lax
from jax.experimental import pallas as pl
from jax.experimental.pallas import tpu as pltpu

F32 = jnp.float32
BF16 = jnp.bfloat16

HEAD_DIM = 64
N_HEADS = 8
N_KV = 2
GROUP = N_HEADS // N_KV
GRID_W = 64
ROPE_THETA = 10000.0
WINDOW = 128
N_BUCKETS = 32
MAX_DISTANCE = 128
EPS = 1e-6
NEG_INF = -1e30

LANES = 128
VMEM_LIMIT = 56 * 1024 * 1024

QA_CHUNK, KA_CHUNK, VA_CHUNK = 0, 8, 10
QB_CHUNK, KB_CHUNK = 12, 20
SLAB_CHUNKS = 22


def _rms(x, gain):
    ms = jnp.mean(x * x, axis=-1, keepdims=True)
    return x * lax.rsqrt(ms + EPS) * gain


def _qkv_body(x_ref, gn_ref, w_ref, gain_ref, cos_ref, sin_ref, o_ref, vt_ref):
    tm = x_ref.shape[0]
    h = _rms(x_ref[...], gn_ref[...]).astype(BF16)
    p = jnp.dot(h, w_ref[...], preferred_element_type=F32)
    lane = lax.broadcasted_iota(jnp.int32, (tm, LANES), 1)
    lo = lane < HEAD_DIM
    first16 = (lane & 16) == 0
    cos = cos_ref[...]
    sin = sin_ref[...]

    def chunk(c):
        return p[:, c * LANES:(c + 1) * LANES]

    def head_norm(c, gain):
        sq = c * c
        s_lo = jnp.sum(jnp.where(lo, sq, 0.0), axis=-1, keepdims=True)
        s_hi = jnp.sum(jnp.where(lo, 0.0, sq), axis=-1, keepdims=True)
        ms = jnp.where(lo, s_lo, s_hi) * (1.0 / HEAD_DIM)
        return c * lax.rsqrt(ms + EPS) * gain

    def rope(c):
        ahead = pltpu.roll(c, LANES - 16, 1)
        behind = pltpu.roll(c, 16, 1)
        return c * cos + jnp.where(first16, ahead, behind) * sin

    def put(idx, val):
        o_ref[:, idx * LANES:(idx + 1) * LANES] = val.astype(BF16)

    def put_q(base, c, val):
        put(base + 2 * c, jnp.where(lo, val, 0.0))
        put(base + 2 * c + 1, jnp.where(lo, 0.0, val))

    def put_dup(base, val):
        sw = pltpu.roll(val, HEAD_DIM, 1)
        put(base, jnp.where(lo, val, sw))
        put(base + 1, jnp.where(lo, sw, val))

    scale = HEAD_DIM ** -0.5
    qn_a, kn_a, qn_b, kn_b = (gain_ref[i:i + 1, :] for i in range(4))
    for c in range(4):
        put_q(QA_CHUNK, c, rope(head_norm(chunk(c), qn_a)) * scale)
    put_dup(KA_CHUNK, rope(head_norm(chunk(4), kn_a)))
    put_dup(VA_CHUNK, chunk(5))
    for c in range(4):
        put_q(QB_CHUNK, c, head_norm(chunk(6 + c), qn_b) * scale)
    put_dup(KB_CHUNK, head_norm(chunk(10), kn_b))
    vt_ref[...] = chunk(11).T.astype(BF16)


def _qkv_call(x2, gn, w_qkv, gains, cos_t, sin_t, seq, tm):
    n, d = x2.shape
    nseq = seq // tm
    return pl.pallas_call(
        _qkv_body,
        grid=(n // tm,),
        in_specs=[
            pl.BlockSpec((tm, d), lambda i: (i, 0)),
            pl.BlockSpec((1, d), lambda i: (0, 0)),
            pl.BlockSpec(w_qkv.shape, lambda i: (0, 0)),
            pl.BlockSpec(gains.shape, lambda i: (0, 0)),
            pl.BlockSpec((tm, LANES), lambda i: (i % nseq, 0)),
            pl.BlockSpec((tm, LANES), lambda i: (i % nseq, 0)),
        ],
        out_specs=[pl.BlockSpec((tm, SLAB_CHUNKS * LANES), lambda i: (i, 0)),
                   pl.BlockSpec((N_KV * HEAD_DIM, tm), lambda i: (0, i))],
        out_shape=[jax.ShapeDtypeStruct((n, SLAB_CHUNKS * LANES), BF16),
                   jax.ShapeDtypeStruct((N_KV * HEAD_DIM, n), BF16)],
        compiler_params=pltpu.CompilerParams(
            dimension_semantics=("arbitrary",), vmem_limit_bytes=VMEM_LIMIT),
        name="qkv_proj",
    )(x2, gn, w_qkv, gains, cos_t, sin_t)


ROW_BLOCK = 128


def _stack_heads(q_ref, row0):
    return jnp.concatenate(
        [q_ref[pl.ds(row0, ROW_BLOCK), k * LANES:(k + 1) * LANES] for k in range(GROUP)], axis=0)


def _store_heads(o_ref, row0, o):
    rb = ROW_BLOCK
    lane = lax.broadcasted_iota(jnp.int32, (rb, LANES), 1)
    lo = lane < HEAD_DIM
    for c in range(GROUP // 2):
        even = o[(2 * c) * rb:(2 * c + 1) * rb]
        odd = o[(2 * c + 1) * rb:(2 * c + 2) * rb]
        o_ref[pl.ds(row0, rb), c * LANES:(c + 1) * LANES] = (
            jnp.where(lo, even, odd).astype(o_ref.dtype))


_NT = (((1,), (1,)), ((), ()))


def _global_attn_body(q_ref, k_ref, v_ref, o_ref, s_buf, p_buf, m_buf, l_buf):
    rb = ROW_BLOCK
    n_blocks = q_ref.shape[0] // rb

    def row(j):
        return pl.multiple_of(j * rb, rb)

    def scores(j, slot):
        s = lax.dot_general(_stack_heads(q_ref, row(j)), k_ref[...], _NT,
                            preferred_element_type=F32)
        s_buf[slot] = s
        m_buf[slot] = jnp.max(s, axis=-1, keepdims=True)

    def probs(slot):
        p = jnp.exp(s_buf[slot] - m_buf[slot])
        l_buf[slot] = jnp.sum(p, axis=-1, keepdims=True)
        p_buf[slot] = p.astype(BF16)

    def output(j, slot):
        o = jnp.dot(p_buf[slot], v_ref[...], preferred_element_type=F32)
        _store_heads(o_ref, row(j), o / l_buf[slot])

    scores(0, 0)
    scores(1, 1)
    probs(0)

    def steady(jj, carry):
        j = 2 + 2 * jj
        scores(j, 0)
        probs(1)
        output(j - 2, 0)
        scores(j + 1, 1)
        probs(0)
        output(j - 1, 1)
        return carry

    lax.fori_loop(0, (n_blocks - 2) // 2, steady, 0)
    probs(1)
    output(n_blocks - 2, 0)
    output(n_blocks - 1, 1)


def _global_attn_call(slab, batch, seq):
    n = slab.shape[0]
    qw = GROUP * LANES
    rows = GROUP * ROW_BLOCK
    assert (seq // ROW_BLOCK) % 2 == 0
    return pl.pallas_call(
        _global_attn_body,
        grid=(batch, N_KV),
        in_specs=[
            pl.BlockSpec((seq, qw), lambda b, g: (b, QA_CHUNK // GROUP + g)),
            pl.BlockSpec((seq, LANES), lambda b, g: (b, KA_CHUNK + g)),
            pl.BlockSpec((seq, LANES), lambda b, g: (b, VA_CHUNK + g)),
        ],
        out_specs=pl.BlockSpec((seq, qw // 2), lambda b, g: (b, g)),
        out_shape=jax.ShapeDtypeStruct((n, N_HEADS * HEAD_DIM), BF16),
        scratch_shapes=[
            pltpu.VMEM((2, rows, seq), F32),
            pltpu.VMEM((2, rows, seq), BF16),
            pltpu.VMEM((2, rows, 1), F32),
            pltpu.VMEM((2, rows, 1), F32),
        ],
        compiler_params=pltpu.CompilerParams(
            dimension_semantics=("arbitrary",) * 2, vmem_limit_bytes=VMEM_LIMIT),
        name="global_attn",
    )(slab, slab, slab)


WINDOW_BLOCKS = 4


def _window_attn_body(q_ref, k_ref, vl_ref, vm_ref, vr_ref, bias_ref, sink_ref, o_ref):
    w = WINDOW
    nb = WINDOW_BLOCKS
    nblk = k_ref.shape[0] // w
    i = pl.program_id(2)
    n_steps = pl.num_programs(2)
    bias = bias_ref[...]
    sink = sink_ref[...]

    kt = []
    for t in range(-1, nb + 1):
        start = pl.multiple_of(jnp.clip(nb * i + t, 0, nblk - 1) * w, w)
        kt.append(k_ref[pl.ds(start, w), :])
    vt = [vl_ref[...]] + [vm_ref[:, t * w:(t + 1) * w] for t in range(nb)] + [vr_ref[...]]
    for jb in range(nb):
        q4 = _stack_heads(q_ref, jb * w)
        kw = jnp.concatenate(kt[jb:jb + 3], axis=0)
        s = lax.dot_general(kw, q4, _NT, preferred_element_type=F32) + bias
        rows = [s[t * w:(t + 1) * w] for t in range(3)]
        if jb == 0:
            rows[0] = jnp.where(i > 0, rows[0], NEG_INF)
        if jb == nb - 1:
            rows[2] = jnp.where(i < n_steps - 1, rows[2], NEG_INF)
        m = jnp.maximum(jnp.maximum(rows[0], rows[1]), rows[2])
        m = jnp.maximum(jnp.max(m, axis=0, keepdims=True), sink)
        ps = [jnp.exp(r - m) for r in rows]
        denom = jnp.sum(ps[0] + ps[1] + ps[2], axis=0, keepdims=True) + jnp.exp(sink - m)
        p = jnp.concatenate([r.astype(BF16) for r in ps], axis=0)
        vw = jnp.concatenate(vt[jb:jb + 3], axis=1)
        o = jnp.dot(vw, p, preferred_element_type=F32) / denom
        o = jnp.concatenate([o[:, k * w:(k + 1) * w] for k in range(GROUP)], axis=0)
        o_ref[jb * w:(jb + 1) * w, :] = o.T.astype(o_ref.dtype)


def _window_attn_call(slab, vt, bias, sink, batch, seq):
    n = slab.shape[0]
    w = WINDOW
    nb = WINDOW_BLOCKS
    tq = nb * w
    nq = seq // tq
    nblk = seq // w
    qw = GROUP * LANES
    return pl.pallas_call(
        _window_attn_body,
        grid=(batch, N_KV, nq),
        in_specs=[
            pl.BlockSpec((tq, qw), lambda b, g, i: (b * nq + i, QB_CHUNK // GROUP + g)),
            pl.BlockSpec((seq, LANES), lambda b, g, i: (b, KB_CHUNK + g)),
            pl.BlockSpec((HEAD_DIM, w),
                         lambda b, g, i: (g, b * nblk + jnp.maximum(nb * i - 1, 0))),
            pl.BlockSpec((HEAD_DIM, tq), lambda b, g, i: (g, b * nq + i)),
            pl.BlockSpec((HEAD_DIM, w),
                         lambda b, g, i: (g, b * nblk + jnp.minimum(nb * i + nb, nblk - 1))),
            pl.BlockSpec((None, 3 * w, GROUP * w), lambda b, g, i: (g, 0, 0)),
            pl.BlockSpec((None, 1, GROUP * w), lambda b, g, i: (g, 0, 0)),
        ],
        out_specs=pl.BlockSpec((tq, qw // 2), lambda b, g, i: (b * nq + i, g)),
        out_shape=jax.ShapeDtypeStruct((n, N_HEADS * HEAD_DIM), BF16),
        compiler_params=pltpu.CompilerParams(
            dimension_semantics=("arbitrary",) * 3, vmem_limit_bytes=VMEM_LIMIT),
        name="window_attn",
    )(slab, slab, vt, vt, vt, bias, sink)


def _mix_mlp_body(x_ref, oa_ref, ob_ref, gn_ref, wg_ref, bg_ref, woa_ref, wob_ref, wout_ref,
                  gm_ref, w1_ref, w2_ref, o_ref):
    d = x_ref.shape[1]
    x = x_ref[...]
    h = _rms(x, gn_ref[...]).astype(BF16)
    gates = jnp.dot(h, wg_ref[...], preferred_element_type=F32) + bg_ref[...]
    g = jax.nn.sigmoid(gates)
    ya = jnp.dot(oa_ref[...], woa_ref[...], preferred_element_type=F32)
    yb = jnp.dot(ob_ref[...], wob_ref[...], preferred_element_type=F32)
    mixed = g[:, :d] * ya + g[:, d:] * yb
    x = x + jnp.dot(mixed.astype(BF16), wout_ref[...], preferred_element_type=F32)
    h = _rms(x, gm_ref[...]).astype(BF16)
    u = jnp.square(jnp.maximum(jnp.dot(h, w1_ref[...], preferred_element_type=F32), 0.0))
    o_ref[...] = x + jnp.dot(u.astype(BF16), w2_ref[...], preferred_element_type=F32)


def _resident(arr):
    return pl.BlockSpec(arr.shape, lambda i: (0,) * arr.ndim, pipeline_mode=pl.Buffered(1))


def _mix_mlp_call(x2, oa, ob, gn, wg, bg, woa, wob, wout, gm, w1, w2, tm):
    n, d = x2.shape
    row = lambda width: pl.BlockSpec((tm, width), lambda i: (i, 0))
    consts = (gn, wg, bg, woa, wob, wout, gm, w1, w2)
    return pl.pallas_call(
        _mix_mlp_body,
        grid=(n // tm,),
        in_specs=[row(d), row(oa.shape[1]), row(ob.shape[1])] + [_resident(a) for a in consts],
        out_specs=row(d),
        out_shape=jax.ShapeDtypeStruct((n, d), F32),
        compiler_params=pltpu.CompilerParams(
            dimension_semantics=("arbitrary",), vmem_limit_bytes=VMEM_LIMIT),
        name="mix_mlp",
    )(x2, oa, ob, *consts)


def _rope_tables(seq):
    rows = seq // GRID_W
    row = jnp.repeat(jnp.arange(rows, dtype=jnp.int32), GRID_W)
    col = jnp.tile(jnp.arange(GRID_W, dtype=jnp.int32), rows)
    n_freq = HEAD_DIM // 4
    inv_freq = ROPE_THETA ** (-jnp.arange(n_freq, dtype=F32) / n_freq)
    ang_row = row.astype(F32)[:, None] * inv_freq[None, :]
    ang_col = col.astype(F32)[:, None] * inv_freq[None, :]
    cr, sr, cc, sc = jnp.cos(ang_row), jnp.sin(ang_row), jnp.cos(ang_col), jnp.sin(ang_col)
    cos64 = jnp.concatenate([cr, cr, cc, cc], axis=-1)
    sin64 = jnp.concatenate([-sr, sr, -sc, sc], axis=-1)
    return jnp.tile(cos64, (1, 2)), jnp.tile(sin64, (1, 2))


def _t5_bucket(rel):
    nb = N_BUCKETS // 2
    max_exact = nb // 2
    n = jnp.abs(rel)
    large = max_exact + (jnp.log(jnp.maximum(n, 1).astype(F32) / max_exact)
                         / math.log(MAX_DISTANCE / max_exact) * (nb - max_exact)).astype(jnp.int32)
    large = jnp.minimum(large, nb - 1)
    return jnp.where(rel > 0, nb, 0) + jnp.where(n < max_exact, n, large)


def _window_bias(rel_bias):
    w = WINDOW
    r = jnp.arange(w, dtype=jnp.int32)[:, None]
    j = jnp.arange(3 * w, dtype=jnp.int32)[None, :]
    rel = j - w - r
    bucket = _t5_bucket(rel)
    hit = bucket[None, :, :, None] == jnp.arange(N_BUCKETS, dtype=jnp.int32)
    bias = jnp.sum(jnp.where(hit, rel_bias.astype(F32).T[:, None, None, :], 0.0), axis=-1)
    bias = jnp.where(jnp.abs(rel)[None] <= w, bias, NEG_INF)
    return jnp.transpose(bias.reshape(N_KV, GROUP * w, 3 * w), (0, 2, 1))


def kernel(x, w_in, b_gate, qn_a, kn_a, qn_b, kn_b, w_o_a, w_o_b, w_out,
           sink_b, rel_bias, norm_mix, norm_mlp, w_mlp1, w_mlp2):
    batch, seq, d = x.shape
    depth = w_in.shape[0]
    n_qkv = 2 * (N_HEADS + 2 * N_KV) * HEAD_DIM
    cos_t, sin_t = _rope_tables(seq)
    bias = _window_bias(rel_bias)
    x2 = x.reshape(batch * seq, d)
    pair = lambda v: jnp.tile(v, 2)
    for l in range(depth):
        gains = jnp.zeros((8, LANES), F32).at[:4].set(
            jnp.stack([pair(qn_a[l]), pair(kn_a[l]), pair(qn_b[l]), pair(kn_b[l])]))
        slab, vt_b = _qkv_call(x2, norm_mix[l][None, :], w_in[l, :, :n_qkv].astype(BF16),
                               gains, cos_t, sin_t, seq, tm=256)
        oa = _global_attn_call(slab, batch, seq)
        sink = jnp.repeat(sink_b[l].astype(F32), WINDOW).reshape(N_KV, 1, GROUP * WINDOW)
        ob = _window_attn_call(slab, vt_b, bias, sink, batch, seq)
        x2 = _mix_mlp_call(
            x2, oa, ob, norm_mix[l][None, :], w_in[l, :, n_qkv:].astype(BF16),
            b_gate[l][None, :], w_o_a[l].astype(BF16), w_o_b[l].astype(BF16),
            w_out[l].astype(BF16), norm_mlp[l][None, :], w_mlp1[l].astype(BF16),
            w_mlp2[l].astype(BF16), tm=256)
    return x2.reshape(batch, seq, d)
```

```python
import math
from functools import partial

import jax
import jax.numpy as jnp
from jax import lax
from jax.experimental import pallas as pl
from jax.experimental.pallas import tpu as pltpu

F32 = jnp.float32
BF16 = jnp.bfloat16

HEAD_DIM = 64
N_HEADS = 8
N_KV = 2
GROUP = N_HEADS // N_KV
GRID_W = 64
ROPE_THETA = 10000.0
WINDOW = 128
N_BUCKETS = 32
MAX_DISTANCE = 128
EPS = 1e-6
NEG_INF = -1e30

LANES = 128
VMEM_LIMIT = 56 * 1024 * 1024

QA_CHUNK, QB_CHUNK, KA_CHUNK, KB_CHUNK = 0, 8, 16, 18
SLAB_CHUNKS = 20
VT_ROWS = 80


def _rms(x, gain):
    ms = jnp.mean(x * x, axis=-1, keepdims=True)
    return x * lax.rsqrt(ms + EPS) * gain


def _qkv_body(x_ref, gn_ref, w_ref, gain_ref, cos_ref, sin_ref, o_ref, vta_ref, vtb_ref):
    tm = x_ref.shape[0]
    h = _rms(x_ref[...], gn_ref[...]).astype(BF16)
    p = jnp.dot(h, w_ref[...], preferred_element_type=F32)
    lane = lax.broadcasted_iota(jnp.int32, (tm, LANES), 1)
    lo = lane < HEAD_DIM
    first16 = (lane & 16) == 0
    cos = cos_ref[...]
    sin = sin_ref[...]

    def chunk(c):
        return p[:, c * LANES:(c + 1) * LANES]

    def head_norm(c, gain):
        sq = c * c
        s_lo = jnp.sum(jnp.where(lo, sq, 0.0), axis=-1, keepdims=True)
        s_hi = jnp.sum(jnp.where(lo, 0.0, sq), axis=-1, keepdims=True)
        ms = jnp.where(lo, s_lo, s_hi) * (1.0 / HEAD_DIM)
        return c * lax.rsqrt(ms + EPS) * gain

    def rope(c):
        ahead = pltpu.roll(c, LANES - 16, 1)
        behind = pltpu.roll(c, 16, 1)
        return c * cos + jnp.where(first16, ahead, behind) * sin

    def put(idx, val):
        o_ref[:, idx * LANES:(idx + 1) * LANES] = val.astype(BF16)

    def put_q(base, c, val):
        put(base + 2 * c, jnp.where(lo, val, 0.0))
        put(base + 2 * c + 1, jnp.where(lo, 0.0, val))

    def put_dup(base, val):
        sw = pltpu.roll(val, HEAD_DIM, 1)
        put(base, jnp.where(lo, val, sw))
        put(base + 1, jnp.where(lo, sw, val))

    scale = HEAD_DIM ** -0.5
    qn_a, kn_a, qn_b, kn_b = (gain_ref[i:i + 1, :] for i in range(4))
    for c in range(4):
        put_q(QA_CHUNK, c, rope(head_norm(chunk(c), qn_a)) * scale)
    put_dup(KA_CHUNK, rope(head_norm(chunk(4), kn_a)))
    for c in range(4):
        put_q(QB_CHUNK, c, head_norm(chunk(6 + c), qn_b) * scale)
    put_dup(KB_CHUNK, head_norm(chunk(10), kn_b))
    vta = chunk(5).T
    pad_rows = VT_ROWS - HEAD_DIM
    ones_row = (lax.broadcasted_iota(jnp.int32, (pad_rows, tm), 0) == 0).astype(F32)
    vta_ref[...] = jnp.concatenate(
        [vta[:HEAD_DIM], ones_row, vta[HEAD_DIM:], ones_row], axis=0).astype(BF16)
    vtb_ref[...] = chunk(11).T.astype(BF16)


def _qkv_call(x2, gn, w_qkv, gains, cos_t, sin_t, seq, tm):
    n, d = x2.shape
    nseq = seq // tm
    return pl.pallas_call(
        _qkv_body,
        grid=(n // tm,),
        in_specs=[
            pl.BlockSpec((tm, d), lambda i: (i, 0)),
            pl.BlockSpec((1, d), lambda i: (0, 0)),
            pl.BlockSpec(w_qkv.shape, lambda i: (0, 0)),
            pl.BlockSpec(gains.shape, lambda i: (0, 0)),
            pl.BlockSpec((tm, LANES), lambda i: (i % nseq, 0)),
            pl.BlockSpec((tm, LANES), lambda i: (i % nseq, 0)),
        ],
        out_specs=[pl.BlockSpec((tm, SLAB_CHUNKS * LANES), lambda i: (i, 0)),
                   pl.BlockSpec((N_KV * VT_ROWS, tm), lambda i: (0, i)),
                   pl.BlockSpec((N_KV * HEAD_DIM, tm), lambda i: (0, i))],
        out_shape=[jax.ShapeDtypeStruct((n, SLAB_CHUNKS * LANES), BF16),
                   jax.ShapeDtypeStruct((N_KV * VT_ROWS, n), BF16),
                   jax.ShapeDtypeStruct((N_KV * HEAD_DIM, n), BF16)],
        compiler_params=pltpu.CompilerParams(
            dimension_semantics=("arbitrary",), vmem_limit_bytes=VMEM_LIMIT),
        name="qkv_proj",
    )(x2, gn, w_qkv, gains, cos_t, sin_t)


ROW_BLOCK = 128


def _stack_heads(q_ref, row0):
    return jnp.concatenate(
        [q_ref[pl.ds(row0, ROW_BLOCK), k * LANES:(k + 1) * LANES] for k in range(GROUP)], axis=0)


def _store_heads_t(o_ref, row0, o):
    rb = ROW_BLOCK
    o = jnp.concatenate([o[:, k * rb:(k + 1) * rb] for k in range(GROUP)], axis=0)
    o_ref[pl.ds(row0, rb), :] = o.T.astype(o_ref.dtype)


_NT = (((1,), (1,)), ((), ()))


def _global_attn_body(q_ref, k_ref, vt_ref, o_ref, s_buf, p_buf, m_buf):
    rb = ROW_BLOCK
    n_blocks = q_ref.shape[0] // rb

    def row(j):
        return pl.multiple_of(j * rb, rb)

    def scores(j, slot):
        s = lax.dot_general(k_ref[...], _stack_heads(q_ref, row(j)), _NT,
                            preferred_element_type=F32)
        s_buf[slot] = s
        m_buf[slot] = jnp.max(s, axis=0, keepdims=True)

    def probs(slot):
        p_buf[slot] = jnp.exp(s_buf[slot] - m_buf[slot]).astype(BF16)

    def output(j, slot):
        o = jnp.dot(vt_ref[...], p_buf[slot], preferred_element_type=F32)
        _store_heads_t(o_ref, row(j), o[:HEAD_DIM] / o[HEAD_DIM:HEAD_DIM + 1])

    scores(0, 0)
    scores(1, 1)
    probs(0)

    def steady(jj, carry):
        j = 2 + 2 * jj
        scores(j, 0)
        probs(1)
        output(j - 2, 0)
        scores(j + 1, 1)
        probs(0)
        output(j - 1, 1)
        return carry

    lax.fori_loop(0, (n_blocks - 2) // 2, steady, 0)
    probs(1)
    output(n_blocks - 2, 0)
    output(n_blocks - 1, 1)


def _global_attn_call(slab, vt, batch, seq):
    n = slab.shape[0]
    qw = GROUP * LANES
    cols = GROUP * ROW_BLOCK
    assert (seq // ROW_BLOCK) % 2 == 0
    return pl.pallas_call(
        _global_attn_body,
        grid=(batch, N_KV),
        in_specs=[
            pl.BlockSpec((seq, qw), lambda b, g: (b, QA_CHUNK // GROUP + g)),
            pl.BlockSpec((seq, LANES), lambda b, g: (b, KA_CHUNK + g)),
            pl.BlockSpec((VT_ROWS, seq), lambda b, g: (g, b)),
        ],
        out_specs=pl.BlockSpec((seq, qw // 2), lambda b, g: (b, g)),
        out_shape=jax.ShapeDtypeStruct((n, N_HEADS * HEAD_DIM), BF16),
        scratch_shapes=[
            pltpu.VMEM((2, seq, cols), F32),
            pltpu.VMEM((2, seq, cols), BF16),
            pltpu.VMEM((2, 1, cols), F32),
        ],
        compiler_params=pltpu.CompilerParams(
            dimension_semantics=("arbitrary",) * 2, vmem_limit_bytes=VMEM_LIMIT),
        name="global_attn",
    )(slab, slab, vt)


WINDOW_BLOCKS = 4


def _window_attn_body(q_ref, k_ref, vl_ref, vm_ref, vr_ref, bias_ref, sink_ref, o_ref):
    w = WINDOW
    nb = WINDOW_BLOCKS
    nblk = k_ref.shape[0] // w
    i = pl.program_id(2)
    n_steps = pl.num_programs(2)
    bias = bias_ref[...]
    sink = sink_ref[...]

    kt = []
    for t in range(-1, nb + 1):
        start = pl.multiple_of(jnp.clip(nb * i + t, 0, nblk - 1) * w, w)
        kt.append(k_ref[pl.ds(start, w), :])
    vt = [vl_ref[...]] + [vm_ref[:, t * w:(t + 1) * w] for t in range(nb)] + [vr_ref[...]]
    for jb in range(nb):
        q4 = _stack_heads(q_ref, jb * w)
        kw = jnp.concatenate(kt[jb:jb + 3], axis=0)
        s = lax.dot_general(kw, q4, _NT, preferred_element_type=F32) + bias
        rows = [s[t * w:(t + 1) * w] for t in range(3)]
        if jb == 0:
            rows[0] = jnp.where(i > 0, rows[0], NEG_INF)
        if jb == nb - 1:
            rows[2] = jnp.where(i < n_steps - 1, rows[2], NEG_INF)
        m = jnp.maximum(jnp.maximum(rows[0], rows[1]), rows[2])
        m = jnp.maximum(jnp.max(m, axis=0, keepdims=True), sink)
        ps = [jnp.exp(r - m) for r in rows]
        denom = jnp.sum(ps[0] + ps[1] + ps[2], axis=0, keepdims=True) + jnp.exp(sink - m)
        p = jnp.concatenate([r.astype(BF16) for r in ps], axis=0)
        vw = jnp.concatenate(vt[jb:jb + 3], axis=1)
        o = jnp.dot(vw, p, preferred_element_type=F32) / denom
        _store_heads_t(o_ref, jb * w, o)


def _window_attn_call(slab, vt, bias, sink, batch, seq):
    n = slab.shape[0]
    w = WINDOW
    nb = WINDOW_BLOCKS
    tq = nb * w
    nq = seq // tq
    nblk = seq // w
    qw = GROUP * LANES
    return pl.pallas_call(
        _window_attn_body,
        grid=(batch, N_KV, nq),
        in_specs=[
            pl.BlockSpec((tq, qw), lambda b, g, i: (b * nq + i, QB_CHUNK // GROUP + g)),
            pl.BlockSpec((seq, LANES), lambda b, g, i: (b, KB_CHUNK + g)),
            pl.BlockSpec((HEAD_DIM, w),
                         lambda b, g, i: (g, b * nblk + jnp.maximum(nb * i - 1, 0))),
            pl.BlockSpec((HEAD_DIM, tq), lambda b, g, i: (g, b * nq + i)),
            pl.BlockSpec((HEAD_DIM, w),
                         lambda b, g, i: (g, b * nblk + jnp.minimum(nb * i + nb, nblk - 1))),
            pl.BlockSpec((None, 3 * w, GROUP * w), lambda b, g, i: (g, 0, 0)),
            pl.BlockSpec((None, 1, GROUP * w), lambda b, g, i: (g, 0, 0)),
        ],
        out_specs=pl.BlockSpec((tq, qw // 2), lambda b, g, i: (b * nq + i, g)),
        out_shape=jax.ShapeDtypeStruct((n, N_HEADS * HEAD_DIM), BF16),
        compiler_params=pltpu.CompilerParams(
            dimension_semantics=("arbitrary",) * 3, vmem_limit_bytes=VMEM_LIMIT),
        name="window_attn",
    )(slab, slab, vt, vt, vt, bias, sink)


def _mix_mlp_body(x_ref, oa_ref, ob_ref, gn_ref, wg_ref, bg_ref, woa_ref, wob_ref, wout_ref,
                  gm_ref, w1_ref, w2_ref, o_ref):
    d = x_ref.shape[1]
    x = x_ref[...]
    h = _rms(x, gn_ref[...]).astype(BF16)
    gates = jnp.dot(h, wg_ref[...], preferred_element_type=F32) + bg_ref[...]
    g = jax.nn.sigmoid(gates)
    ya = jnp.dot(oa_ref[...], woa_ref[...], preferred_element_type=F32)
    yb = jnp.dot(ob_ref[...], wob_ref[...], preferred_element_type=F32)
    mixed = g[:, :d] * ya + g[:, d:] * yb
    x = x + jnp.dot(mixed.astype(BF16), wout_ref[...], preferred_element_type=F32)
    h = _rms(x, gm_ref[...]).astype(BF16)
    u = jnp.square(jnp.maximum(jnp.dot(h, w1_ref[...], preferred_element_type=F32), 0.0))
    o_ref[...] = x + jnp.dot(u.astype(BF16), w2_ref[...], preferred_element_type=F32)


def _resident(arr):
    return pl.BlockSpec(arr.shape, lambda i: (0,) * arr.ndim, pipeline_mode=pl.Buffered(1))


def _mix_mlp_call(x2, oa, ob, gn, wg, bg, woa, wob, wout, gm, w1, w2, tm):
    n, d = x2.shape
    row = lambda width: pl.BlockSpec((tm, width), lambda i: (i, 0))
    consts = (gn, wg, bg, woa, wob, wout, gm, w1, w2)
    return pl.pallas_call(
        _mix_mlp_body,
        grid=(n // tm,),
        in_specs=[row(d), row(oa.shape[1]), row(ob.shape[1])] + [_resident(a) for a in consts],
        out_specs=row(d),
        out_shape=jax.ShapeDtypeStruct((n, d), F32),
        compiler_params=pltpu.CompilerParams(
            dimension_semantics=("arbitrary",), vmem_limit_bytes=VMEM_LIMIT),
        name="mix_mlp",
    )(x2, oa, ob, *consts)


def _rope_tables(seq):
    rows = seq // GRID_W
    row = jnp.repeat(jnp.arange(rows, dtype=jnp.int32), GRID_W)
    col = jnp.tile(jnp.arange(GRID_W, dtype=jnp.int32), rows)
    n_freq = HEAD_DIM // 4
    inv_freq = ROPE_THETA ** (-jnp.arange(n_freq, dtype=F32) / n_freq)
    ang_row = row.astype(F32)[:, None] * inv_freq[None, :]
    ang_col = col.astype(F32)[:, None] * inv_freq[None, :]
    cr, sr, cc, sc = jnp.cos(ang_row), jnp.sin(ang_row), jnp.cos(ang_col), jnp.sin(ang_col)
    cos64 = jnp.concatenate([cr, cr, cc, cc], axis=-1)
    sin64 = jnp.concatenate([-sr, sr, -sc, sc], axis=-1)
    return jnp.tile(cos64, (1, 2)), jnp.tile(sin64, (1, 2))


def _t5_bucket(rel):
    nb = N_BUCKETS // 2
    max_exact = nb // 2
    n = jnp.abs(rel)
    large = max_exact + (jnp.log(jnp.maximum(n, 1).astype(F32) / max_exact)
                         / math.log(MAX_DISTANCE / max_exact) * (nb - max_exact)).astype(jnp.int32)
    large = jnp.minimum(large, nb - 1)
    return jnp.where(rel > 0, nb, 0) + jnp.where(n < max_exact, n, large)


def _window_bias(rel_bias):
    w = WINDOW
    r = jnp.arange(w, dtype=jnp.int32)[:, None]
    j = jnp.arange(3 * w, dtype=jnp.int32)[None, :]
    rel = j - w - r
    bucket = _t5_bucket(rel)
    hit = bucket[None, :, :, None] == jnp.arange(N_BUCKETS, dtype=jnp.int32)
    bias = jnp.sum(jnp.where(hit, rel_bias.astype(F32).T[:, None, None, :], 0.0), axis=-1)
    bias = jnp.where(jnp.abs(rel)[None] <= w, bias, NEG_INF)
    return jnp.transpose(bias.reshape(N_KV, GROUP * w, 3 * w), (0, 2, 1))


def kernel(x, w_in, b_gate, qn_a, kn_a, qn_b, kn_b, w_o_a, w_o_b, w_out,
           sink_b, rel_bias, norm_mix, norm_mlp, w_mlp1, w_mlp2):
    batch, seq, d = x.shape
    depth = w_in.shape[0]
    n_qkv = 2 * (N_HEADS + 2 * N_KV) * HEAD_DIM
    cos_t, sin_t = _rope_tables(seq)
    bias = _window_bias(rel_bias)
    x2 = x.reshape(batch * seq, d)
    pair = lambda v: jnp.tile(v, 2)
    for l in range(depth):
        gains = jnp.zeros((8, LANES), F32).at[:4].set(
            jnp.stack([pair(qn_a[l]), pair(kn_a[l]), pair(qn_b[l]), pair(kn_b[l])]))
        slab, vt_a, vt_b = _qkv_call(x2, norm_mix[l][None, :], w_in[l, :, :n_qkv].astype(BF16),
                                     gains, cos_t, sin_t, seq, tm=256)
        oa = _global_attn_call(slab, vt_a, batch, seq)
        sink = jnp.repeat(sink_b[l].astype(F32), WINDOW).reshape(N_KV, 1, GROUP * WINDOW)
        ob = _window_attn_call(slab, vt_b, bias, sink, batch, seq)
        x2 = _mix_mlp_call(
            x2, oa, ob, norm_mix[l][None, :], w_in[l, :, n_qkv:].astype(BF16),
            b_gate[l][None, :], w_o_a[l].astype(BF16), w_o_b[l].astype(BF16),
            w_out[l].astype(BF16), norm_mlp[l][None, :], w_mlp1[l].astype(BF16),
            w_mlp2[l].astype(BF16), tm=256)
    return x2.reshape(batch, seq, d)
```

```python
import math
from functools import partial

import jax
import jax.numpy as jnp
from jax import lax
from jax.experimental import pallas as pl
from jax.experimental.pallas import tpu as pltpu

F32 = jnp.float32
BF16 = jnp.bfloat16

HEAD_DIM = 64
N_HEADS = 8
N_KV = 2
GROUP = N_HEADS // N_KV
GRID_W = 64
ROPE_THETA = 10000.0
WINDOW = 128
N_BUCKETS = 32
MAX_DISTANCE = 128
EPS = 1e-6
NEG_INF = -1e30

LANES = 128
VMEM_LIMIT = 56 * 1024 * 1024

QA_CHUNK, QB_CHUNK, KA_CHUNK, KB_CHUNK = 0, 8, 16, 18
SLAB_CHUNKS = 20
VT_ROWS = 80


def _rms(x, gain):
    ms = jnp.mean(x * x, axis=-1, keepdims=True)
    return x * lax.rsqrt(ms + EPS) * gain


def _qkv_body(x_ref, gn_ref, w_ref, gain_ref, cos_ref, sin_ref, ones_ref,
              o_ref, vta_ref, vtb_ref):
    tm = x_ref.shape[0]
    h = _rms(x_ref[...], gn_ref[...]).astype(BF16)
    p = jnp.dot(h, w_ref[...], preferred_element_type=F32)
    lane = lax.broadcasted_iota(jnp.int32, (tm, LANES), 1)
    lo = lane < HEAD_DIM
    first16 = (lane & 16) == 0
    cos = cos_ref[...]
    sin = sin_ref[...]

    def chunk(c):
        return p[:, c * LANES:(c + 1) * LANES]

    def head_norm_pair(ca, gain_a, cb, gain_b):
        c2 = jnp.concatenate([ca, cb], axis=1)
        sq = c2 * c2
        hi = sq.astype(BF16)
        lo_part = (sq - hi.astype(F32)).astype(BF16)
        ss = (jnp.dot(hi, ones_ref[...], preferred_element_type=F32)
              + jnp.dot(lo_part, ones_ref[...], preferred_element_type=F32))
        y = c2 * lax.rsqrt(ss * (1.0 / HEAD_DIM) + EPS)
        return y[:, :LANES] * gain_a, y[:, LANES:] * gain_b

    def rope(c):
        ahead = pltpu.roll(c, LANES - 16, 1)
        behind = pltpu.roll(c, 16, 1)
        return c * cos + jnp.where(first16, ahead, behind) * sin

    def put(idx, val):
        o_ref[:, idx * LANES:(idx + 1) * LANES] = val.astype(BF16)

    def put_q(base, c, val):
        put(base + 2 * c, jnp.where(lo, val, 0.0))
        put(base + 2 * c + 1, jnp.where(lo, 0.0, val))

    def put_dup(base, val):
        sw = pltpu.roll(val, HEAD_DIM, 1)
        put(base, jnp.where(lo, val, sw))
        put(base + 1, jnp.where(lo, sw, val))

    scale = HEAD_DIM ** -0.5
    qn_a, kn_a, qn_b, kn_b = (gain_ref[i:i + 1, :] for i in range(4))
    for c in (0, 2):
        qa = head_norm_pair(chunk(c), qn_a, chunk(c + 1), qn_a)
        qb = head_norm_pair(chunk(6 + c), qn_b, chunk(7 + c), qn_b)
        for d in range(2):
            put_q(QA_CHUNK, c + d, rope(qa[d]) * scale)
            put_q(QB_CHUNK, c + d, qb[d] * scale)
    ka, kb = head_norm_pair(chunk(4), kn_a, chunk(10), kn_b)
    put_dup(KA_CHUNK, rope(ka))
    put_dup(KB_CHUNK, kb)
    vta = chunk(5).T
    pad_rows = VT_ROWS - HEAD_DIM
    ones_row = (lax.broadcasted_iota(jnp.int32, (pad_rows, tm), 0) == 0).astype(F32)
    vta_ref[...] = jnp.concatenate(
        [vta[:HEAD_DIM], ones_row, vta[HEAD_DIM:], ones_row], axis=0).astype(BF16)
    vtb_ref[...] = chunk(11).T.astype(BF16)


def _qkv_call(x2, gn, w_qkv, gains, cos_t, sin_t, seq, tm):
    n, d = x2.shape
    nseq = seq // tm
    head_ones = jnp.kron(jnp.eye(2 * LANES // HEAD_DIM, dtype=F32),
                         jnp.ones((HEAD_DIM, HEAD_DIM), F32)).astype(BF16)
    return pl.pallas_call(
        _qkv_body,
        grid=(n // tm,),
        in_specs=[
            pl.BlockSpec((tm, d), lambda i: (i, 0)),
            pl.BlockSpec((1, d), lambda i: (0, 0)),
            pl.BlockSpec(w_qkv.shape, lambda i: (0, 0)),
            pl.BlockSpec(gains.shape, lambda i: (0, 0)),
            pl.BlockSpec((tm, LANES), lambda i: (i % nseq, 0)),
            pl.BlockSpec((tm, LANES), lambda i: (i % nseq, 0)),
            pl.BlockSpec(head_ones.shape, lambda i: (0, 0)),
        ],
        out_specs=[pl.BlockSpec((tm, SLAB_CHUNKS * LANES), lambda i: (i, 0)),
                   pl.BlockSpec((N_KV * VT_ROWS, tm), lambda i: (0, i)),
                   pl.BlockSpec((N_KV * HEAD_DIM, tm), lambda i: (0, i))],
        out_shape=[jax.ShapeDtypeStruct((n, SLAB_CHUNKS * LANES), BF16),
                   jax.ShapeDtypeStruct((N_KV * VT_ROWS, n), BF16),
                   jax.ShapeDtypeStruct((N_KV * HEAD_DIM, n), BF16)],
        compiler_params=pltpu.CompilerParams(
            dimension_semantics=("arbitrary",), vmem_limit_bytes=VMEM_LIMIT),
        name="qkv_proj",
    )(x2, gn, w_qkv, gains, cos_t, sin_t, head_ones)


ROW_BLOCK = 128


def _stack_heads(q_ref, row0):
    return jnp.concatenate(
        [q_ref[pl.ds(row0, ROW_BLOCK), k * LANES:(k + 1) * LANES] for k in range(GROUP)], axis=0)


def _store_heads_t(o_ref, row0, o):
    rb = ROW_BLOCK
    o = jnp.concatenate([o[:, k * rb:(k + 1) * rb] for k in range(GROUP)], axis=0)
    o_ref[pl.ds(row0, rb), :] = o.T.astype(o_ref.dtype)


_NT = (((1,), (1,)), ((), ()))


def _global_attn_body(q_ref, k_ref, vt_ref, o_ref, s_buf, p_buf, m_buf):
    rb = ROW_BLOCK
    n_blocks = q_ref.shape[0] // rb

    def row(j):
        return pl.multiple_of(j * rb, rb)

    def scores(j, slot):
        s = lax.dot_general(k_ref[...], _stack_heads(q_ref, row(j)), _NT,
                            preferred_element_type=F32)
        s_buf[slot] = s
        m_buf[slot] = jnp.max(s, axis=0, keepdims=True)

    def probs(slot):
        p_buf[slot] = jnp.exp(s_buf[slot] - m_buf[slot]).astype(BF16)

    def output(j, slot):
        o = jnp.dot(vt_ref[...], p_buf[slot], preferred_element_type=F32)
        _store_heads_t(o_ref, row(j), o[:HEAD_DIM] / o[HEAD_DIM:HEAD_DIM + 1])

    scores(0, 0)
    scores(1, 1)
    probs(0)

    def steady(jj, carry):
        j = 2 + 2 * jj
        scores(j, 0)
        probs(1)
        output(j - 2, 0)
        scores(j + 1, 1)
        probs(0)
        output(j - 1, 1)
        return carry

    lax.fori_loop(0, (n_blocks - 2) // 2, steady, 0)
    probs(1)
    output(n_blocks - 2, 0)
    output(n_blocks - 1, 1)


def _global_attn_call(slab, vt, batch, seq):
    n = slab.shape[0]
    qw = GROUP * LANES
    cols = GROUP * ROW_BLOCK
    assert (seq // ROW_BLOCK) % 2 == 0
    return pl.pallas_call(
        _global_attn_body,
        grid=(batch, N_KV),
        in_specs=[
            pl.BlockSpec((seq, qw), lambda b, g: (b, QA_CHUNK // GROUP + g)),
            pl.BlockSpec((seq, LANES), lambda b, g: (b, KA_CHUNK + g)),
            pl.BlockSpec((VT_ROWS, seq), lambda b, g: (g, b)),
        ],
        out_specs=pl.BlockSpec((seq, qw // 2), lambda b, g: (b, g)),
        out_shape=jax.ShapeDtypeStruct((n, N_HEADS * HEAD_DIM), BF16),
        scratch_shapes=[
            pltpu.VMEM((2, seq, cols), F32),
            pltpu.VMEM((2, seq, cols), BF16),
            pltpu.VMEM((2, 1, cols), F32),
        ],
        compiler_params=pltpu.CompilerParams(
            dimension_semantics=("arbitrary",) * 2, vmem_limit_bytes=VMEM_LIMIT),
        name="global_attn",
    )(slab, slab, vt)


WINDOW_BLOCKS = 8


def _window_attn_body(q_ref, k_ref, vl_ref, vm_ref, vr_ref, bias_ref, sink_ref, o_ref):
    w = WINDOW
    nb = WINDOW_BLOCKS
    nblk = k_ref.shape[0] // w
    i = pl.program_id(2)
    n_steps = pl.num_programs(2)
    bias = bias_ref[...]
    sink = sink_ref[...]

    kt = []
    for t in range(-1, nb + 1):
        start = pl.multiple_of(jnp.clip(nb * i + t, 0, nblk - 1) * w, w)
        kt.append(k_ref[pl.ds(start, w), :])
    vt = [vl_ref[...]] + [vm_ref[:, t * w:(t + 1) * w] for t in range(nb)] + [vr_ref[...]]
    for jb in range(nb):
        q4 = _stack_heads(q_ref, jb * w)
        kw = jnp.concatenate(kt[jb:jb + 3], axis=0)
        s = lax.dot_general(kw, q4, _NT, preferred_element_type=F32) + bias
        rows = [s[t * w:(t + 1) * w] for t in range(3)]
        if jb == 0:
            rows[0] = jnp.where(i > 0, rows[0], NEG_INF)
        if jb == nb - 1:
            rows[2] = jnp.where(i < n_steps - 1, rows[2], NEG_INF)
        m = jnp.maximum(jnp.maximum(rows[0], rows[1]), rows[2])
        m = jnp.maximum(jnp.max(m, axis=0, keepdims=True), sink)
        ps = [jnp.exp(r - m) for r in rows]
        denom = jnp.sum(ps[0] + ps[1] + ps[2], axis=0, keepdims=True) + jnp.exp(sink - m)
        p = jnp.concatenate([r.astype(BF16) for r in ps], axis=0)
        vw = jnp.concatenate(vt[jb:jb + 3], axis=1)
        o = jnp.dot(vw, p, preferred_element_type=F32) / denom
        _store_heads_t(o_ref, jb * w, o)


def _window_attn_call(slab, vt, bias, sink, batch, seq):
    n = slab.shape[0]
    w = WINDOW
    nb = WINDOW_BLOCKS
    tq = nb * w
    nq = seq // tq
    nblk = seq // w
    qw = GROUP * LANES
    return pl.pallas_call(
        _window_attn_body,
        grid=(batch, N_KV, nq),
        in_specs=[
            pl.BlockSpec((tq, qw), lambda b, g, i: (b * nq + i, QB_CHUNK // GROUP + g)),
            pl.BlockSpec((seq, LANES), lambda b, g, i: (b, KB_CHUNK + g)),
            pl.BlockSpec((HEAD_DIM, w),
                         lambda b, g, i: (g, b * nblk + jnp.maximum(nb * i - 1, 0))),
            pl.BlockSpec((HEAD_DIM, tq), lambda b, g, i: (g, b * nq + i)),
            pl.BlockSpec((HEAD_DIM, w),
                         lambda b, g, i: (g, b * nblk + jnp.minimum(nb * i + nb, nblk - 1))),
            pl.BlockSpec((None, 3 * w, GROUP * w), lambda b, g, i: (g, 0, 0)),
            pl.BlockSpec((None, 1, GROUP * w), lambda b, g, i: (g, 0, 0)),
        ],
        out_specs=pl.BlockSpec((tq, qw // 2), lambda b, g, i: (b * nq + i, g)),
        out_shape=jax.ShapeDtypeStruct((n, N_HEADS * HEAD_DIM), BF16),
        compiler_params=pltpu.CompilerParams(
            dimension_semantics=("arbitrary",) * 3, vmem_limit_bytes=VMEM_LIMIT),
        name="window_attn",
    )(slab, slab, vt, vt, vt, bias, sink)


def _mix_mlp_body(x_ref, oa_ref, ob_ref, gn_ref, wg_ref, bg_ref, woa_ref, wob_ref, wout_ref,
                  gm_ref, w1_ref, w2_ref, o_ref):
    d = x_ref.shape[1]
    x = x_ref[...]
    h = _rms(x, gn_ref[...]).astype(BF16)
    gates = jnp.dot(h, wg_ref[...], preferred_element_type=F32) + bg_ref[...]
    g = jax.nn.sigmoid(gates)
    ya = jnp.dot(oa_ref[...], woa_ref[...], preferred_element_type=F32)
    yb = jnp.dot(ob_ref[...], wob_ref[...], preferred_element_type=F32)
    mixed = g[:, :d] * ya + g[:, d:] * yb
    x = x + jnp.dot(mixed.astype(BF16), wout_ref[...], preferred_element_type=F32)
    h = _rms(x, gm_ref[...]).astype(BF16)
    u = jnp.square(jnp.maximum(jnp.dot(h, w1_ref[...], preferred_element_type=F32), 0.0))
    o_ref[...] = x + jnp.dot(u.astype(BF16), w2_ref[...], preferred_element_type=F32)


def _resident(arr):
    return pl.BlockSpec(arr.shape, lambda i: (0,) * arr.ndim, pipeline_mode=pl.Buffered(1))


def _mix_mlp_call(x2, oa, ob, gn, wg, bg, woa, wob, wout, gm, w1, w2, tm):
    n, d = x2.shape
    row = lambda width: pl.BlockSpec((tm, width), lambda i: (i, 0))
    consts = (gn, wg, bg, woa, wob, wout, gm, w1, w2)
    return pl.pallas_call(
        _mix_mlp_body,
        grid=(n // tm,),
        in_specs=[row(d), row(oa.shape[1]), row(ob.shape[1])] + [_resident(a) for a in consts],
        out_specs=row(d),
        out_shape=jax.ShapeDtypeStruct((n, d), F32),
        compiler_params=pltpu.CompilerParams(
            dimension_semantics=("arbitrary",), vmem_limit_bytes=VMEM_LIMIT),
        name="mix_mlp",
    )(x2, oa, ob, *consts)


def _rope_tables(seq):
    rows = seq // GRID_W
    row = jnp.repeat(jnp.arange(rows, dtype=jnp.int32), GRID_W)
    col = jnp.tile(jnp.arange(GRID_W, dtype=jnp.int32), rows)
    n_freq = HEAD_DIM // 4
    inv_freq = ROPE_THETA ** (-jnp.arange(n_freq, dtype=F32) / n_freq)
    ang_row = row.astype(F32)[:, None] * inv_freq[None, :]
    ang_col = col.astype(F32)[:, None] * inv_freq[None, :]
    cr, sr, cc, sc = jnp.cos(ang_row), jnp.sin(ang_row), jnp.cos(ang_col), jnp.sin(ang_col)
    cos64 = jnp.concatenate([cr, cr, cc, cc], axis=-1)
    sin64 = jnp.concatenate([-sr, sr, -sc, sc], axis=-1)
    return jnp.tile(cos64, (1, 2)), jnp.tile(sin64, (1, 2))


def _t5_bucket(rel):
    nb = N_BUCKETS // 2
    max_exact = nb // 2
    n = jnp.abs(rel)
    large = max_exact + (jnp.log(jnp.maximum(n, 1).astype(F32) / max_exact)
                         / math.log(MAX_DISTANCE / max_exact) * (nb - max_exact)).astype(jnp.int32)
    large = jnp.minimum(large, nb - 1)
    return jnp.where(rel > 0, nb, 0) + jnp.where(n < max_exact, n, large)


def _window_bias(rel_bias):
    w = WINDOW
    r = jnp.arange(w, dtype=jnp.int32)[:, None]
    j = jnp.arange(3 * w, dtype=jnp.int32)[None, :]
    rel = j - w - r
    bucket = _t5_bucket(rel)
    hit = bucket[None, :, :, None] == jnp.arange(N_BUCKETS, dtype=jnp.int32)
    bias = jnp.sum(jnp.where(hit, rel_bias.astype(F32).T[:, None, None, :], 0.0), axis=-1)
    bias = jnp.where(jnp.abs(rel)[None] <= w, bias, NEG_INF)
    return jnp.transpose(bias.reshape(N_KV, GROUP * w, 3 * w), (0, 2, 1))


def kernel(x, w_in, b_gate, qn_a, kn_a, qn_b, kn_b, w_o_a, w_o_b, w_out,
           sink_b, rel_bias, norm_mix, norm_mlp, w_mlp1, w_mlp2):
    batch, seq, d = x.shape
    depth = w_in.shape[0]
    n_qkv = 2 * (N_HEADS + 2 * N_KV) * HEAD_DIM
    cos_t, sin_t = _rope_tables(seq)
    bias = _window_bias(rel_bias)
    x2 = x.reshape(batch * seq, d)
    pair = lambda v: jnp.tile(v, 2)
    for l in range(depth):
        gains = jnp.zeros((8, LANES), F32).at[:4].set(
            jnp.stack([pair(qn_a[l]), pair(kn_a[l]), pair(qn_b[l]), pair(kn_b[l])]))
        slab, vt_a, vt_b = _qkv_call(x2, norm_mix[l][None, :], w_in[l, :, :n_qkv].astype(BF16),
                                     gains, cos_t, sin_t, seq, tm=256)
        oa = _global_attn_call(slab, vt_a, batch, seq)
        sink = jnp.repeat(sink_b[l].astype(F32), WINDOW).reshape(N_KV, 1, GROUP * WINDOW)
        ob = _window_attn_call(slab, vt_b, bias, sink, batch, seq)
        x2 = _mix_mlp_call(
            x2, oa, ob, norm_mix[l][None, :], w_in[l, :, n_qkv:].astype(BF16),
            b_gate[l][None, :], w_o_a[l].astype(BF16), w_o_b[l].astype(BF16),
            w_out[l].astype(BF16), norm_mlp[l][None, :], w_mlp1[l].astype(BF16),
            w_mlp2[l].astype(BF16), tm=512)
    return x2.reshape(batch, seq, d)
```

```python
import math

import jax
import jax.numpy as jnp
from jax import lax
from jax.experimental import pallas as pl
from jax.experimental.pallas import tpu as pltpu

F32 = jnp.float32
BF16 = jnp.bfloat16

HEAD_DIM = 64
N_HEADS = 8
N_KV = 2
GROUP = N_HEADS // N_KV
GRID_W = 64
ROPE_THETA = 10000.0
WINDOW = 128
N_BUCKETS = 32
MAX_DISTANCE = 128
EPS = 1e-6
NEG_INF = -1e30

LANES = 128
VMEM_LIMIT = 56 * 1024 * 1024

ROW_BLOCK = 128
VT_ROWS = 80


def _rms(x, gain):
    ms = jnp.mean(x * x, axis=-1, keepdims=True)
    return x * lax.rsqrt(ms + EPS) * gain


def _qkv_body(x_ref, gn_ref, w_ref, gain_ref, cos_ref, sin_ref, ones_ref,
              qta_ref, qtb_ref, kk_ref, vta_ref, vtb_ref):
    tm = x_ref.shape[0]
    h = _rms(x_ref[...], gn_ref[...]).astype(BF16)
    p = jnp.dot(h, w_ref[...], preferred_element_type=F32)
    lane = lax.broadcasted_iota(jnp.int32, (tm, LANES), 1)
    first16 = (lane & 16) == 0
    cos = cos_ref[...]
    sin = sin_ref[...]

    def chunk(c):
        return p[:, c * LANES:(c + 1) * LANES]

    def head_norm_pair(ca, gain_a, cb, gain_b):
        c2 = jnp.concatenate([ca, cb], axis=1)
        sq = c2 * c2
        hi = sq.astype(BF16)
        lo = (sq - hi.astype(F32)).astype(BF16)
        ss = (jnp.dot(hi, ones_ref[...], preferred_element_type=F32)
              + jnp.dot(lo, ones_ref[...], preferred_element_type=F32))
        y = c2 * lax.rsqrt(ss * (1.0 / HEAD_DIM) + EPS)
        return y[:, :LANES] * gain_a, y[:, LANES:] * gain_b

    def rope(c):
        ahead = pltpu.roll(c, LANES - 16, 1)
        behind = pltpu.roll(c, 16, 1)
        return c * cos + jnp.where(first16, ahead, behind) * sin

    def put_qt(ref, c, val):
        for r in range(tm // ROW_BLOCK):
            blk = val[r * ROW_BLOCK:(r + 1) * ROW_BLOCK, :]
            ref[r, c * LANES:(c + 1) * LANES, :] = blk.T.astype(BF16)

    scale = HEAD_DIM ** -0.5
    qn_a, kn_a, qn_b, kn_b = (gain_ref[i:i + 1, :] for i in range(4))
    for c in (0, 2):
        qa = head_norm_pair(chunk(c), qn_a, chunk(c + 1), qn_a)
        qb = head_norm_pair(chunk(6 + c), qn_b, chunk(7 + c), qn_b)
        for d in range(2):
            put_qt(qta_ref, c + d, rope(qa[d]) * scale)
            put_qt(qtb_ref, c + d, qb[d] * scale)
    ka, kb = head_norm_pair(chunk(4), kn_a, chunk(10), kn_b)
    kk_ref[:, :LANES] = rope(ka).astype(BF16)
    kk_ref[:, LANES:] = kb.astype(BF16)
    vta = chunk(5).T
    pad_rows = VT_ROWS - HEAD_DIM
    ones_row = (lax.broadcasted_iota(jnp.int32, (pad_rows, tm), 0) == 0).astype(F32)
    vta_ref[...] = jnp.concatenate(
        [vta[:HEAD_DIM], ones_row, vta[HEAD_DIM:], ones_row], axis=0).astype(BF16)
    vtb_ref[...] = chunk(11).T.astype(BF16)


def _qkv_call(x2, gn, w_qkv, gains, cos_t, sin_t, seq, tm):
    n, d = x2.shape
    nseq = seq // tm
    qd = N_HEADS * HEAD_DIM
    head_ones = jnp.kron(jnp.eye(2 * LANES // HEAD_DIM, dtype=F32),
                         jnp.ones((HEAD_DIM, HEAD_DIM), F32)).astype(BF16)
    qt_spec = pl.BlockSpec((tm // ROW_BLOCK, qd, ROW_BLOCK), lambda i: (i, 0, 0))
    qt_shape = jax.ShapeDtypeStruct((n // ROW_BLOCK, qd, ROW_BLOCK), BF16)
    return pl.pallas_call(
        _qkv_body,
        grid=(n // tm,),
        in_specs=[
            pl.BlockSpec((tm, d), lambda i: (i, 0)),
            pl.BlockSpec((1, d), lambda i: (0, 0)),
            pl.BlockSpec(w_qkv.shape, lambda i: (0, 0)),
            pl.BlockSpec(gains.shape, lambda i: (0, 0)),
            pl.BlockSpec((tm, LANES), lambda i: (i % nseq, 0)),
            pl.BlockSpec((tm, LANES), lambda i: (i % nseq, 0)),
            pl.BlockSpec(head_ones.shape, lambda i: (0, 0)),
        ],
        out_specs=[qt_spec, qt_spec,
                   pl.BlockSpec((tm, 2 * LANES), lambda i: (i, 0)),
                   pl.BlockSpec((N_KV * VT_ROWS, tm), lambda i: (0, i)),
                   pl.BlockSpec((N_KV * HEAD_DIM, tm), lambda i: (0, i))],
        out_shape=[qt_shape, qt_shape,
                   jax.ShapeDtypeStruct((n, 2 * LANES), BF16),
                   jax.ShapeDtypeStruct((N_KV * VT_ROWS, n), BF16),
                   jax.ShapeDtypeStruct((N_KV * HEAD_DIM, n), BF16)],
        compiler_params=pltpu.CompilerParams(
            dimension_semantics=("arbitrary",), vmem_limit_bytes=VMEM_LIMIT),
        name="qkv_proj",
    )(x2, gn, w_qkv, gains, cos_t, sin_t, head_ones)


def _query_rhs(q_ref, j, g):
    qt = jnp.concatenate(
        [q_ref[j, h * HEAD_DIM:(h + 1) * HEAD_DIM, :] for h in range(GROUP)], axis=1)
    zero = jnp.zeros_like(qt)
    return jnp.concatenate([jnp.where(g == 0, qt, zero), jnp.where(g == 0, zero, qt)], axis=0)


def _store_heads_t(o_ref, row0, o):
    rb = ROW_BLOCK
    o = jnp.concatenate([o[:, k * rb:(k + 1) * rb] for k in range(GROUP)], axis=0)
    o_ref[pl.ds(row0, rb), :] = o.T.astype(o_ref.dtype)


KEY_CHUNK = 512


def _global_attn_body(q_ref, k_ref, vt_ref, o_ref, s0, s1, p0, p1, m0, m1):
    s_buf, p_buf, m_buf = (s0, s1), (p0, p1), (m0, m1)
    g = pl.program_id(1)
    n_blocks = q_ref.shape[0]
    n_chunks = k_ref.shape[0] // KEY_CHUNK

    def keys(c):
        return slice(c * KEY_CHUNK, (c + 1) * KEY_CHUNK)

    def step(j_scores, slot_scores, slot_probs, j_out, slot_out):
        m = None
        if j_scores is not None:
            rhs = _query_rhs(q_ref, j_scores, g)
        for c in range(n_chunks):
            if j_scores is not None:
                s = jnp.dot(k_ref[keys(c), :], rhs, preferred_element_type=F32)
                s_buf[slot_scores][keys(c), :] = s
                mc = jnp.max(s, axis=0, keepdims=True)
                m = mc if m is None else jnp.maximum(m, mc)
            if slot_probs is not None:
                p_buf[slot_probs][keys(c), :] = jnp.exp(
                    s_buf[slot_probs][keys(c), :] - m_buf[slot_probs][...]).astype(BF16)
        if j_scores is not None:
            m_buf[slot_scores][...] = m
        if j_out is not None:
            o = jnp.dot(vt_ref[...], p_buf[slot_out][...],
                        preferred_element_type=F32)
            row0 = j_out * ROW_BLOCK
            if not isinstance(row0, int):
                row0 = pl.multiple_of(row0, ROW_BLOCK)
            _store_heads_t(o_ref, row0, o[:HEAD_DIM] / o[HEAD_DIM:HEAD_DIM + 1])

    step(0, 0, None, None, None)
    step(1, 1, 0, None, None)

    def steady(jj, carry):
        j = 2 + 2 * jj
        step(j, 0, 1, j - 2, 0)
        step(j + 1, 1, 0, j - 1, 1)
        return carry

    lax.fori_loop(0, (n_blocks - 2) // 2, steady, 0)
    step(None, None, 1, n_blocks - 2, 0)
    step(None, None, None, n_blocks - 1, 1)


def _global_attn_call(qt, kk, vt, batch, seq):
    n = kk.shape[0]
    nblk = seq // ROW_BLOCK
    cols = GROUP * ROW_BLOCK
    assert nblk % 2 == 0 and seq % KEY_CHUNK == 0
    return pl.pallas_call(
        _global_attn_body,
        grid=(batch, N_KV),
        in_specs=[
            pl.BlockSpec((nblk, GROUP * HEAD_DIM, ROW_BLOCK), lambda b, g: (b, g, 0)),
            pl.BlockSpec((seq, LANES), lambda b, g: (b, 0)),
            pl.BlockSpec((VT_ROWS, seq), lambda b, g: (g, b)),
        ],
        out_specs=pl.BlockSpec((seq, GROUP * HEAD_DIM), lambda b, g: (b, g)),
        out_shape=jax.ShapeDtypeStruct((n, N_HEADS * HEAD_DIM), BF16),
        scratch_shapes=(
            [pltpu.VMEM((seq, cols), F32)] * 2
            + [pltpu.VMEM((seq, cols), BF16)] * 2
            + [pltpu.VMEM((1, cols), F32)] * 2),
        compiler_params=pltpu.CompilerParams(
            dimension_semantics=("arbitrary",) * 2, vmem_limit_bytes=VMEM_LIMIT),
        name="global_attn",
    )(qt, kk, vt)


WINDOW_BLOCKS = 8


def _window_attn_body(q_ref, k_ref, vl_ref, vm_ref, vr_ref, bias_ref, sink_ref, o_ref):
    w = WINDOW
    nb = WINDOW_BLOCKS
    nblk = k_ref.shape[0] // w
    g = pl.program_id(1)
    i = pl.program_id(2)
    n_steps = pl.num_programs(2)
    bias = bias_ref[...]
    sink = sink_ref[...]

    kt = []
    for t in range(-1, nb + 1):
        start = pl.multiple_of(jnp.clip(nb * i + t, 0, nblk - 1) * w, w)
        kt.append(k_ref[pl.ds(start, w), :])
    vt = [vl_ref[...]] + [vm_ref[:, t * w:(t + 1) * w] for t in range(nb)] + [vr_ref[...]]
    for jb in range(nb):
        kw = jnp.concatenate(kt[jb:jb + 3], axis=0)
        s = jnp.dot(kw, _query_rhs(q_ref, jb, g), preferred_element_type=F32) + bias
        rows = [s[t * w:(t + 1) * w] for t in range(3)]
        if jb == 0:
            rows[0] = jnp.where(i > 0, rows[0], NEG_INF)
        if jb == nb - 1:
            rows[2] = jnp.where(i < n_steps - 1, rows[2], NEG_INF)
        m = jnp.maximum(jnp.maximum(rows[0], rows[1]), rows[2])
        m = jnp.maximum(jnp.max(m, axis=0, keepdims=True), sink)
        ps = [jnp.exp(r - m) for r in rows]
        denom = jnp.sum(ps[0] + ps[1] + ps[2], axis=0, keepdims=True) + jnp.exp(sink - m)
        p = jnp.concatenate([r.astype(BF16) for r in ps], axis=0)
        vw = jnp.concatenate(vt[jb:jb + 3], axis=1)
        o = jnp.dot(vw, p, preferred_element_type=F32) / denom
        _store_heads_t(o_ref, jb * w, o)


def _window_attn_call(qt, kk, vt, bias, sink, batch, seq):
    n = kk.shape[0]
    w = WINDOW
    nb = WINDOW_BLOCKS
    tq = nb * w
    nq = seq // tq
    nblk = seq // w
    return pl.pallas_call(
        _window_attn_body,
        grid=(batch, N_KV, nq),
        in_specs=[
            pl.BlockSpec((nb, GROUP * HEAD_DIM, w), lambda b, g, i: (b * nq + i, g, 0)),
            pl.BlockSpec((seq, LANES), lambda b, g, i: (b, 1)),
            pl.BlockSpec((HEAD_DIM, w),
                         lambda b, g, i: (g, b * nblk + jnp.maximum(nb * i - 1, 0))),
            pl.BlockSpec((HEAD_DIM, tq), lambda b, g, i: (g, b * nq + i)),
            pl.BlockSpec((HEAD_DIM, w),
                         lambda b, g, i: (g, b * nblk + jnp.minimum(nb * i + nb, nblk - 1))),
            pl.BlockSpec((None, 3 * w, GROUP * w), lambda b, g, i: (g, 0, 0)),
            pl.BlockSpec((None, 1, GROUP * w), lambda b, g, i: (g, 0, 0)),
        ],
        out_specs=pl.BlockSpec((tq, GROUP * HEAD_DIM), lambda b, g, i: (b * nq + i, g)),
        out_shape=jax.ShapeDtypeStruct((n, N_HEADS * HEAD_DIM), BF16),
        compiler_params=pltpu.CompilerParams(
            dimension_semantics=("arbitrary",) * 3, vmem_limit_bytes=VMEM_LIMIT),
        name="window_attn",
    )(qt, kk, vt, vt, vt, bias, sink)


def _mix_mlp_body(x_ref, oa_ref, ob_ref, gn_ref, wg_ref, bg_ref, woa_ref, wob_ref, wout_ref,
                  gm_ref, w1_ref, w2_ref, o_ref):
    d = x_ref.shape[1]
    x = x_ref[...]
    h = _rms(x, gn_ref[...]).astype(BF16)
    gates = jnp.dot(h, wg_ref[...], preferred_element_type=F32) + bg_ref[...]
    g = jax.nn.sigmoid(gates)
    ya = jnp.dot(oa_ref[...], woa_ref[...], preferred_element_type=F32)
    yb = jnp.dot(ob_ref[...], wob_ref[...], preferred_element_type=F32)
    mixed = g[:, :d] * ya + g[:, d:] * yb
    x = x + jnp.dot(mixed.astype(BF16), wout_ref[...], preferred_element_type=F32)
    h = _rms(x, gm_ref[...]).astype(BF16)
    u = jnp.square(jnp.maximum(jnp.dot(h, w1_ref[...], preferred_element_type=F32), 0.0))
    o_ref[...] = x + jnp.dot(u.astype(BF16), w2_ref[...], preferred_element_type=F32)


def _resident(arr):
    return pl.BlockSpec(arr.shape, lambda i: (0,) * arr.ndim, pipeline_mode=pl.Buffered(1))


def _mix_mlp_call(x2, oa, ob, gn, wg, bg, woa, wob, wout, gm, w1, w2, tm):
    n, d = x2.shape
    row = lambda width: pl.BlockSpec((tm, width), lambda i: (i, 0))
    consts = (gn, wg, bg, woa, wob, wout, gm, w1, w2)
    return pl.pallas_call(
        _mix_mlp_body,
        grid=(n // tm,),
        in_specs=[row(d), row(oa.shape[1]), row(ob.shape[1])] + [_resident(a) for a in consts],
        out_specs=row(d),
        out_shape=jax.ShapeDtypeStruct((n, d), F32),
        compiler_params=pltpu.CompilerParams(
            dimension_semantics=("arbitrary",), vmem_limit_bytes=VMEM_LIMIT),
        name="mix_mlp",
    )(x2, oa, ob, *consts)


def _rope_tables(seq):
    rows = seq // GRID_W
    row = jnp.repeat(jnp.arange(rows, dtype=jnp.int32), GRID_W)
    col = jnp.tile(jnp.arange(GRID_W, dtype=jnp.int32), rows)
    n_freq = HEAD_DIM // 4
    inv_freq = ROPE_THETA ** (-jnp.arange(n_freq, dtype=F32) / n_freq)
    ang_row = row.astype(F32)[:, None] * inv_freq[None, :]
    ang_col = col.astype(F32)[:, None] * inv_freq[None, :]
    cr, sr, cc, sc = jnp.cos(ang_row), jnp.sin(ang_row), jnp.cos(ang_col), jnp.sin(ang_col)
    cos64 = jnp.concatenate([cr, cr, cc, cc], axis=-1)
    sin64 = jnp.concatenate([-sr, sr, -sc, sc], axis=-1)
    return jnp.tile(cos64, (1, 2)), jnp.tile(sin64, (1, 2))


def _t5_bucket(rel):
    nb = N_BUCKETS // 2
    max_exact = nb // 2
    n = jnp.abs(rel)
    large = max_exact + (jnp.log(jnp.maximum(n, 1).astype(F32) / max_exact)
                         / math.log(MAX_DISTANCE / max_exact) * (nb - max_exact)).astype(jnp.int32)
    large = jnp.minimum(large, nb - 1)
    return jnp.where(rel > 0, nb, 0) + jnp.where(n < max_exact, n, large)


def _window_bias(rel_bias):
    w = WINDOW
    r = jnp.arange(w, dtype=jnp.int32)[:, None]
    j = jnp.arange(3 * w, dtype=jnp.int32)[None, :]
    rel = j - w - r
    bucket = _t5_bucket(rel)
    hit = bucket[None, :, :, None] == jnp.arange(N_BUCKETS, dtype=jnp.int32)
    bias = jnp.sum(jnp.where(hit, rel_bias.astype(F32).T[:, None, None, :], 0.0), axis=-1)
    bias = jnp.where(jnp.abs(rel)[None] <= w, bias, NEG_INF)
    return jnp.transpose(bias.reshape(N_KV, GROUP * w, 3 * w), (0, 2, 1))


def kernel(x, w_in, b_gate, qn_a, kn_a, qn_b, kn_b, w_o_a, w_o_b, w_out,
           sink_b, rel_bias, norm_mix, norm_mlp, w_mlp1, w_mlp2):
    batch, seq, d = x.shape
    depth = w_in.shape[0]
    n_qkv = 2 * (N_HEADS + 2 * N_KV) * HEAD_DIM
    cos_t, sin_t = _rope_tables(seq)
    bias = _window_bias(rel_bias)
    x2 = x.reshape(batch * seq, d)
    pair = lambda v: jnp.tile(v, 2)
    for l in range(depth):
        gains = jnp.zeros((8, LANES), F32).at[:4].set(
            jnp.stack([pair(qn_a[l]), pair(kn_a[l]), pair(qn_b[l]), pair(kn_b[l])]))
        qt_a, qt_b, kk, vt_a, vt_b = _qkv_call(
            x2, norm_mix[l][None, :], w_in[l, :, :n_qkv].astype(BF16),
            gains, cos_t, sin_t, seq, tm=256)
        oa = _global_attn_call(qt_a, kk, vt_a, batch, seq)
        sink = jnp.repeat(sink_b[l].astype(F32), WINDOW).reshape(N_KV, 1, GROUP * WINDOW)
        ob = _window_attn_call(qt_b, kk, vt_b, bias, sink, batch, seq)
        x2 = _mix_mlp_call(
            x2, oa, ob, norm_mix[l][None, :], w_in[l, :, n_qkv:].astype(BF16),
            b_gate[l][None, :], w_o_a[l].astype(BF16), w_o_b[l].astype(BF16),
            w_out[l].astype(BF16), norm_mlp[l][None, :], w_mlp1[l].astype(BF16),
            w_mlp2[l].astype(BF16), tm=512)
    return x2.reshape(batch, seq, d)
```

```python
import math

import jax
import jax.numpy as jnp
from jax import lax
from jax.experimental import pallas as pl
from jax.experimental.pallas import tpu as pltpu

F32 = jnp.float32
BF16 = jnp.bfloat16

HEAD_DIM = 64
N_HEADS = 8
N_KV = 2
GROUP = N_HEADS // N_KV
GRID_W = 64
ROPE_THETA = 10000.0
WINDOW = 128
N_BUCKETS = 32
MAX_DISTANCE = 128
EPS = 1e-6
NEG_INF = -1e30

LANES = 128
VMEM_LIMIT = 56 * 1024 * 1024

ROW_BLOCK = 128
VT_ROWS = 80
LOG2E = math.log2(math.e)
Q_SCALE = HEAD_DIM ** -0.5 * LOG2E


def _rms(x, gain):
    ms = jnp.mean(x * x, axis=-1, keepdims=True)
    return x * lax.rsqrt(ms + EPS) * gain


def _qkv_body(x_ref, gn_ref, w_ref, gain_ref, cos_ref, sin_ref, ones_ref,
              qta_ref, qtb_ref, kk_ref, vta_ref, vtb_ref):
    tm = x_ref.shape[0]
    h = _rms(x_ref[...], gn_ref[...]).astype(BF16)
    p = jnp.dot(h, w_ref[...], preferred_element_type=F32)
    lane = lax.broadcasted_iota(jnp.int32, (tm, LANES), 1)
    first16 = (lane & 16) == 0
    cos = cos_ref[...]
    sin = sin_ref[...]

    def chunk(c):
        return p[:, c * LANES:(c + 1) * LANES]

    def head_norm_pair(ca, gain_a, cb, gain_b):
        c2 = jnp.concatenate([ca, cb], axis=1)
        sq = c2 * c2
        hi = sq.astype(BF16)
        lo = (sq - hi.astype(F32)).astype(BF16)
        ss = (jnp.dot(hi, ones_ref[...], preferred_element_type=F32)
              + jnp.dot(lo, ones_ref[...], preferred_element_type=F32))
        y = c2 * lax.rsqrt(ss * (1.0 / HEAD_DIM) + EPS)
        return y[:, :LANES] * gain_a, y[:, LANES:] * gain_b

    def rope(c):
        ahead = pltpu.roll(c, LANES - 16, 1)
        behind = pltpu.roll(c, 16, 1)
        return c * cos + jnp.where(first16, ahead, behind) * sin

    def put_qt(ref, c, val):
        for r in range(tm // ROW_BLOCK):
            blk = val[r * ROW_BLOCK:(r + 1) * ROW_BLOCK, :]
            ref[r, c * LANES:(c + 1) * LANES, :] = blk.T.astype(BF16)

    scale = Q_SCALE
    qn_a, kn_a, qn_b, kn_b = (gain_ref[i:i + 1, :] for i in range(4))
    for c in (0, 2):
        qa = head_norm_pair(chunk(c), qn_a, chunk(c + 1), qn_a)
        qb = head_norm_pair(chunk(6 + c), qn_b, chunk(7 + c), qn_b)
        for d in range(2):
            put_qt(qta_ref, c + d, rope(qa[d]) * scale)
            put_qt(qtb_ref, c + d, qb[d] * scale)
    ka, kb = head_norm_pair(chunk(4), kn_a, chunk(10), kn_b)
    kk_ref[:, :LANES] = rope(ka).astype(BF16)
    kk_ref[:, LANES:] = kb.astype(BF16)
    pad_rows = VT_ROWS - HEAD_DIM
    ones_row = (lax.broadcasted_iota(jnp.int32, (pad_rows, tm), 0) == 0).astype(F32)
    for ref, c in ((vta_ref, 5), (vtb_ref, 11)):
        vt = chunk(c).T
        ref[...] = jnp.concatenate(
            [vt[:HEAD_DIM], ones_row, vt[HEAD_DIM:], ones_row], axis=0).astype(BF16)


def _qkv_call(x2, gn, w_qkv, gains, cos_t, sin_t, seq, tm):
    n, d = x2.shape
    nseq = seq // tm
    qd = N_HEADS * HEAD_DIM
    head_ones = jnp.kron(jnp.eye(2 * LANES // HEAD_DIM, dtype=F32),
                         jnp.ones((HEAD_DIM, HEAD_DIM), F32)).astype(BF16)
    qt_spec = pl.BlockSpec((tm // ROW_BLOCK, qd, ROW_BLOCK), lambda i: (i, 0, 0))
    qt_shape = jax.ShapeDtypeStruct((n // ROW_BLOCK, qd, ROW_BLOCK), BF16)
    return pl.pallas_call(
        _qkv_body,
        grid=(n // tm,),
        in_specs=[
            pl.BlockSpec((tm, d), lambda i: (i, 0)),
            pl.BlockSpec((1, d), lambda i: (0, 0)),
            pl.BlockSpec(w_qkv.shape, lambda i: (0, 0)),
            pl.BlockSpec(gains.shape, lambda i: (0, 0)),
            pl.BlockSpec((tm, LANES), lambda i: (i % nseq, 0)),
            pl.BlockSpec((tm, LANES), lambda i: (i % nseq, 0)),
            pl.BlockSpec(head_ones.shape, lambda i: (0, 0)),
        ],
        out_specs=[qt_spec, qt_spec,
                   pl.BlockSpec((tm, 2 * LANES), lambda i: (i, 0)),
                   pl.BlockSpec((N_KV * VT_ROWS, tm), lambda i: (0, i)),
                   pl.BlockSpec((N_KV * VT_ROWS, tm), lambda i: (0, i))],
        out_shape=[qt_shape, qt_shape,
                   jax.ShapeDtypeStruct((n, 2 * LANES), BF16),
                   jax.ShapeDtypeStruct((N_KV * VT_ROWS, n), BF16),
                   jax.ShapeDtypeStruct((N_KV * VT_ROWS, n), BF16)],
        compiler_params=pltpu.CompilerParams(
            dimension_semantics=("arbitrary",), vmem_limit_bytes=VMEM_LIMIT),
        name="qkv_proj",
    )(x2, gn, w_qkv, gains, cos_t, sin_t, head_ones)


def _query_rhs(q_ref, j, g):
    qt = jnp.concatenate(
        [q_ref[j, h * HEAD_DIM:(h + 1) * HEAD_DIM, :] for h in range(GROUP)], axis=1)
    zero = jnp.zeros_like(qt)
    return jnp.concatenate([jnp.where(g == 0, qt, zero), jnp.where(g == 0, zero, qt)], axis=0)


def _store_heads_t(o_ref, row0, o):
    rb = ROW_BLOCK
    o = jnp.concatenate([o[:, k * rb:(k + 1) * rb] for k in range(GROUP)], axis=0)
    o_ref[pl.ds(row0, rb), :] = o.T.astype(o_ref.dtype)


KEY_CHUNK = 256


def _global_attn_body(q_ref, k_ref, vt_ref, o_ref, s0, s1, p0, p1, m0, m1):
    s_buf, p_buf, m_buf = (s0, s1), (p0, p1), (m0, m1)
    g = pl.program_id(1)
    n_blocks = q_ref.shape[0]
    n_chunks = k_ref.shape[0] // KEY_CHUNK

    def keys(c):
        return slice(c * KEY_CHUNK, (c + 1) * KEY_CHUNK)

    def step(j_scores, slot_scores, slot_probs, j_out, slot_out):
        m = None
        if j_scores is not None:
            rhs = _query_rhs(q_ref, j_scores, g)
        for c in range(n_chunks):
            if j_scores is not None:
                s = jnp.dot(k_ref[keys(c), :], rhs, preferred_element_type=F32)
                s_buf[slot_scores][keys(c), :] = s
                mc = jnp.max(s, axis=0, keepdims=True)
                m = mc if m is None else jnp.maximum(m, mc)
            if slot_probs is not None:
                p_buf[slot_probs][keys(c), :] = jnp.exp2(
                    s_buf[slot_probs][keys(c), :] - m_buf[slot_probs][...]).astype(BF16)
        if j_scores is not None:
            m_buf[slot_scores][...] = m
        if j_out is not None:
            o = jnp.dot(vt_ref[...], p_buf[slot_out][...],
                        preferred_element_type=F32)
            row0 = j_out * ROW_BLOCK
            if not isinstance(row0, int):
                row0 = pl.multiple_of(row0, ROW_BLOCK)
            _store_heads_t(o_ref, row0, o[:HEAD_DIM] / o[HEAD_DIM:HEAD_DIM + 1])

    step(0, 0, None, None, None)
    step(1, 1, 0, None, None)

    def steady(jj, carry):
        j = 2 + 2 * jj
        step(j, 0, 1, j - 2, 0)
        step(j + 1, 1, 0, j - 1, 1)
        return carry

    lax.fori_loop(0, (n_blocks - 2) // 2, steady, 0)
    step(None, None, 1, n_blocks - 2, 0)
    step(None, None, None, n_blocks - 1, 1)


def _global_attn_call(qt, kk, vt, batch, seq):
    n = kk.shape[0]
    nblk = seq // ROW_BLOCK
    cols = GROUP * ROW_BLOCK
    assert nblk % 2 == 0 and seq % KEY_CHUNK == 0
    return pl.pallas_call(
        _global_attn_body,
        grid=(batch, N_KV),
        in_specs=[
            pl.BlockSpec((nblk, GROUP * HEAD_DIM, ROW_BLOCK), lambda b, g: (b, g, 0)),
            pl.BlockSpec((seq, LANES), lambda b, g: (b, 0)),
            pl.BlockSpec((VT_ROWS, seq), lambda b, g: (g, b)),
        ],
        out_specs=pl.BlockSpec((seq, GROUP * HEAD_DIM), lambda b, g: (b, g)),
        out_shape=jax.ShapeDtypeStruct((n, N_HEADS * HEAD_DIM), BF16),
        scratch_shapes=(
            [pltpu.VMEM((seq, cols), F32)] * 2
            + [pltpu.VMEM((seq, cols), BF16)] * 2
            + [pltpu.VMEM((1, cols), F32)] * 2),
        compiler_params=pltpu.CompilerParams(
            dimension_semantics=("arbitrary",) * 2, vmem_limit_bytes=VMEM_LIMIT),
        name="global_attn",
    )(qt, kk, vt)


WINDOW_BLOCKS = 8


def _window_attn_body(q_ref, k_ref, vl_ref, vm_ref, vr_ref, bias_ref, sink_ref, o_ref):
    w = WINDOW
    nb = WINDOW_BLOCKS
    nblk = k_ref.shape[0] // w
    g = pl.program_id(1)
    i = pl.program_id(2)
    n_steps = pl.num_programs(2)
    bias = bias_ref[...]
    sink = sink_ref[...]

    kt = []
    for t in range(-1, nb + 1):
        start = pl.multiple_of(jnp.clip(nb * i + t, 0, nblk - 1) * w, w)
        kt.append(k_ref[pl.ds(start, w), :])
    vt = [vl_ref[...]] + [vm_ref[:, t * w:(t + 1) * w] for t in range(nb)] + [vr_ref[...]]
    def scores(jb):
        kw = jnp.concatenate(kt[jb:jb + 3], axis=0)
        return jnp.dot(kw, _query_rhs(q_ref, jb, g), preferred_element_type=F32)

    s_next = scores(0)
    for jb in range(nb):
        s = s_next + bias
        if jb + 1 < nb:
            s_next = scores(jb + 1)
        rows = [s[t * w:(t + 1) * w] for t in range(3)]
        if jb == 0:
            rows[0] = jnp.where(i > 0, rows[0], NEG_INF)
        if jb == nb - 1:
            rows[2] = jnp.where(i < n_steps - 1, rows[2], NEG_INF)
        m = jnp.maximum(jnp.maximum(rows[0], rows[1]), rows[2])
        m = jnp.maximum(jnp.max(m, axis=0, keepdims=True), sink)
        p = jnp.concatenate([jnp.exp2(r - m).astype(BF16) for r in rows], axis=0)
        vw = jnp.concatenate(vt[jb:jb + 3], axis=1)
        o = jnp.dot(vw, p, preferred_element_type=F32)
        denom = o[HEAD_DIM:HEAD_DIM + 1] + jnp.exp2(sink - m)
        _store_heads_t(o_ref, jb * w, o[:HEAD_DIM] / denom)


def _window_attn_call(qt, kk, vt, bias, sink, batch, seq):
    n = kk.shape[0]
    w = WINDOW
    nb = WINDOW_BLOCKS
    tq = nb * w
    nq = seq // tq
    nblk = seq // w
    return pl.pallas_call(
        _window_attn_body,
        grid=(batch, N_KV, nq),
        in_specs=[
            pl.BlockSpec((nb, GROUP * HEAD_DIM, w), lambda b, g, i: (b * nq + i, g, 0)),
            pl.BlockSpec((seq, LANES), lambda b, g, i: (b, 1)),
            pl.BlockSpec((VT_ROWS, w),
                         lambda b, g, i: (g, b * nblk + jnp.maximum(nb * i - 1, 0))),
            pl.BlockSpec((VT_ROWS, tq), lambda b, g, i: (g, b * nq + i)),
            pl.BlockSpec((VT_ROWS, w),
                         lambda b, g, i: (g, b * nblk + jnp.minimum(nb * i + nb, nblk - 1))),
            pl.BlockSpec((None, 3 * w, GROUP * w), lambda b, g, i: (g, 0, 0)),
            pl.BlockSpec((None, 1, GROUP * w), lambda b, g, i: (g, 0, 0)),
        ],
        out_specs=pl.BlockSpec((tq, GROUP * HEAD_DIM), lambda b, g, i: (b * nq + i, g)),
        out_shape=jax.ShapeDtypeStruct((n, N_HEADS * HEAD_DIM), BF16),
        compiler_params=pltpu.CompilerParams(
            dimension_semantics=("arbitrary",) * 3, vmem_limit_bytes=VMEM_LIMIT),
        name="window_attn",
    )(qt, kk, vt, vt, vt, bias, sink)


def _mix_mlp_body(x_ref, oa_ref, ob_ref, gn_ref, wg_ref, bg_ref, woa_ref, wob_ref, wout_ref,
                  gm_ref, w1_ref, w2_ref, o_ref):
    d = x_ref.shape[1]
    x = x_ref[...]
    h = _rms(x, gn_ref[...]).astype(BF16)
    gates = jnp.dot(h, wg_ref[...], preferred_element_type=F32) + bg_ref[...]
    g = jax.nn.sigmoid(gates)
    ya = jnp.dot(oa_ref[...], woa_ref[...], preferred_element_type=F32)
    yb = jnp.dot(ob_ref[...], wob_ref[...], preferred_element_type=F32)
    mixed = g[:, :d] * ya + g[:, d:] * yb
    x = x + jnp.dot(mixed.astype(BF16), wout_ref[...], preferred_element_type=F32)
    h = _rms(x, gm_ref[...]).astype(BF16)
    u = jnp.square(jnp.maximum(jnp.dot(h, w1_ref[...], preferred_element_type=F32), 0.0))
    o_ref[...] = x + jnp.dot(u.astype(BF16), w2_ref[...], preferred_element_type=F32)


def _resident(arr):
    return pl.BlockSpec(arr.shape, lambda i: (0,) * arr.ndim, pipeline_mode=pl.Buffered(1))


def _mix_mlp_call(x2, oa, ob, gn, wg, bg, woa, wob, wout, gm, w1, w2, tm):
    n, d = x2.shape
    row = lambda width: pl.BlockSpec((tm, width), lambda i: (i, 0))
    consts = (gn, wg, bg, woa, wob, wout, gm, w1, w2)
    return pl.pallas_call(
        _mix_mlp_body,
        grid=(n // tm,),
        in_specs=[row(d), row(oa.shape[1]), row(ob.shape[1])] + [_resident(a) for a in consts],
        out_specs=row(d),
        out_shape=jax.ShapeDtypeStruct((n, d), F32),
        compiler_params=pltpu.CompilerParams(
            dimension_semantics=("arbitrary",), vmem_limit_bytes=VMEM_LIMIT),
        name="mix_mlp",
    )(x2, oa, ob, *consts)


def _rope_tables(seq):
    rows = seq // GRID_W
    row = jnp.repeat(jnp.arange(rows, dtype=jnp.int32), GRID_W)
    col = jnp.tile(jnp.arange(GRID_W, dtype=jnp.int32), rows)
    n_freq = HEAD_DIM // 4
    inv_freq = ROPE_THETA ** (-jnp.arange(n_freq, dtype=F32) / n_freq)
    ang_row = row.astype(F32)[:, None] * inv_freq[None, :]
    ang_col = col.astype(F32)[:, None] * inv_freq[None, :]
    cr, sr, cc, sc = jnp.cos(ang_row), jnp.sin(ang_row), jnp.cos(ang_col), jnp.sin(ang_col)
    cos64 = jnp.concatenate([cr, cr, cc, cc], axis=-1)
    sin64 = jnp.concatenate([-sr, sr, -sc, sc], axis=-1)
    return jnp.tile(cos64, (1, 2)), jnp.tile(sin64, (1, 2))


def _t5_bucket(rel):
    nb = N_BUCKETS // 2
    max_exact = nb // 2
    n = jnp.abs(rel)
    large = max_exact + (jnp.log(jnp.maximum(n, 1).astype(F32) / max_exact)
                         / math.log(MAX_DISTANCE / max_exact) * (nb - max_exact)).astype(jnp.int32)
    large = jnp.minimum(large, nb - 1)
    return jnp.where(rel > 0, nb, 0) + jnp.where(n < max_exact, n, large)


def _window_bias(rel_bias):
    w = WINDOW
    r = jnp.arange(w, dtype=jnp.int32)[:, None]
    j = jnp.arange(3 * w, dtype=jnp.int32)[None, :]
    rel = j - w - r
    bucket = _t5_bucket(rel)
    hit = bucket[None, :, :, None] == jnp.arange(N_BUCKETS, dtype=jnp.int32)
    bias = jnp.sum(jnp.where(hit, rel_bias.astype(F32).T[:, None, None, :], 0.0), axis=-1) * LOG2E
    bias = jnp.where(jnp.abs(rel)[None] <= w, bias, NEG_INF)
    return jnp.transpose(bias.reshape(N_KV, GROUP * w, 3 * w), (0, 2, 1))


def kernel(x, w_in, b_gate, qn_a, kn_a, qn_b, kn_b, w_o_a, w_o_b, w_out,
           sink_b, rel_bias, norm_mix, norm_mlp, w_mlp1, w_mlp2):
    batch, seq, d = x.shape
    depth = w_in.shape[0]
    n_qkv = 2 * (N_HEADS + 2 * N_KV) * HEAD_DIM
    cos_t, sin_t = _rope_tables(seq)
    bias = _window_bias(rel_bias)
    x2 = x.reshape(batch * seq, d)
    pair = lambda v: jnp.tile(v, 2)
    for l in range(depth):
        gains = jnp.zeros((8, LANES), F32).at[:4].set(
            jnp.stack([pair(qn_a[l]), pair(kn_a[l]), pair(qn_b[l]), pair(kn_b[l])]))
        qt_a, qt_b, kk, vt_a, vt_b = _qkv_call(
            x2, norm_mix[l][None, :], w_in[l, :, :n_qkv].astype(BF16),
            gains, cos_t, sin_t, seq, tm=256)
        oa = _global_attn_call(qt_a, kk, vt_a, batch, seq)
        sink = jnp.repeat(sink_b[l].astype(F32) * LOG2E, WINDOW).reshape(
            N_KV, 1, GROUP * WINDOW)
        ob = _window_attn_call(qt_b, kk, vt_b, bias, sink, batch, seq)
        x2 = _mix_mlp_call(
            x2, oa, ob, norm_mix[l][None, :], w_in[l, :, n_qkv:].astype(BF16),
            b_gate[l][None, :], w_o_a[l].astype(BF16), w_o_b[l].astype(BF16),
            w_out[l].astype(BF16), norm_mlp[l][None, :], w_mlp1[l].astype(BF16),
            w_mlp2[l].astype(BF16), tm=512)
    return x2.reshape(batch, seq, d)
```

```python
import math

import jax
import jax.numpy as jnp
from jax import lax
from jax.experimental import pallas as pl
from jax.experimental.pallas import tpu as pltpu

F32 = jnp.float32
BF16 = jnp.bfloat16

HEAD_DIM = 64
N_HEADS = 8
N_KV = 2
GROUP = N_HEADS // N_KV
GRID_W = 64
ROPE_THETA = 10000.0
WINDOW = 128
N_BUCKETS = 32
MAX_DISTANCE = 128
EPS = 1e-6
NEG_INF = -1e30

LANES = 128
VMEM_LIMIT = 56 * 1024 * 1024

ROW_BLOCK = 128
VT_ROWS = 80
LOG2E = math.log2(math.e)
Q_SCALE = HEAD_DIM ** -0.5 * LOG2E


def _rms(x, gain):
    ms = jnp.mean(x * x, axis=-1, keepdims=True)
    return x * lax.rsqrt(ms + EPS) * gain


def _qkv_body(x_ref, gn_ref, w_ref, gain_ref, cos_ref, sin_ref, ones_ref,
              qta_ref, qtb_ref, kk_ref, vta_ref, vtb_ref, p0, p1):
    i = pl.program_id(0)

    @pl.when(i == 0)
    def _():
        p1[...] = jnp.zeros_like(p1)

    @pl.when(i % 2 == 0)
    def _():
        _qkv_step(x_ref, gn_ref, w_ref, gain_ref, cos_ref, sin_ref, ones_ref,
                  qta_ref, qtb_ref, kk_ref, vta_ref, vtb_ref, p0, p1)

    @pl.when(i % 2 == 1)
    def _():
        _qkv_step(x_ref, gn_ref, w_ref, gain_ref, cos_ref, sin_ref, ones_ref,
                  qta_ref, qtb_ref, kk_ref, vta_ref, vtb_ref, p1, p0)


def _qkv_step(x_ref, gn_ref, w_ref, gain_ref, cos_ref, sin_ref, ones_ref,
              qta_ref, qtb_ref, kk_ref, vta_ref, vtb_ref, p_new, p_ref):
    tm = x_ref.shape[0]
    lane = lax.broadcasted_iota(jnp.int32, (tm, LANES), 1)
    first16 = (lane & 16) == 0
    cos = cos_ref[...]
    sin = sin_ref[...]

    def chunk(c):
        return p_ref[:, c * LANES:(c + 1) * LANES]

    def head_norm_pair(ca, gain_a, cb, gain_b):
        c2 = jnp.concatenate([ca, cb], axis=1)
        sq = c2 * c2
        hi = sq.astype(BF16)
        lo = (sq - hi.astype(F32)).astype(BF16)
        ss = (jnp.dot(hi, ones_ref[...], preferred_element_type=F32)
              + jnp.dot(lo, ones_ref[...], preferred_element_type=F32))
        y = c2 * lax.rsqrt(ss * (1.0 / HEAD_DIM) + EPS)
        return y[:, :LANES] * gain_a, y[:, LANES:] * gain_b

    def rope(c):
        ahead = pltpu.roll(c, LANES - 16, 1)
        behind = pltpu.roll(c, 16, 1)
        return c * cos + jnp.where(first16, ahead, behind) * sin

    def put_qt(ref, c, val):
        for r in range(tm // ROW_BLOCK):
            blk = val[r * ROW_BLOCK:(r + 1) * ROW_BLOCK, :]
            ref[r, c * LANES:(c + 1) * LANES, :] = blk.T.astype(BF16)

    scale = Q_SCALE
    qn_a, kn_a, qn_b, kn_b = (gain_ref[i:i + 1, :] for i in range(4))
    for c in (0, 2):
        qa = head_norm_pair(chunk(c), qn_a, chunk(c + 1), qn_a)
        qb = head_norm_pair(chunk(6 + c), qn_b, chunk(7 + c), qn_b)
        for d in range(2):
            put_qt(qta_ref, c + d, rope(qa[d]) * scale)
            put_qt(qtb_ref, c + d, qb[d] * scale)
    ka, kb = head_norm_pair(chunk(4), kn_a, chunk(10), kn_b)
    kk_ref[:, :LANES] = rope(ka).astype(BF16)
    kk_ref[:, LANES:] = kb.astype(BF16)
    pad_rows = VT_ROWS - HEAD_DIM
    ones_row = (lax.broadcasted_iota(jnp.int32, (pad_rows, tm), 0) == 0).astype(F32)
    for ref, c in ((vta_ref, 5), (vtb_ref, 11)):
        vt = chunk(c).T
        ref[...] = jnp.concatenate(
            [vt[:HEAD_DIM], ones_row, vt[HEAD_DIM:], ones_row], axis=0).astype(BF16)
    h = _rms(x_ref[...], gn_ref[...]).astype(BF16)
    p_new[...] = jnp.dot(h, w_ref[...], preferred_element_type=F32)


def _qkv_call(x2, gn, w_qkv, gains, cos_t, sin_t, seq, tm):
    n, d = x2.shape
    nseq = seq // tm
    qd = N_HEADS * HEAD_DIM
    head_ones = jnp.kron(jnp.eye(2 * LANES // HEAD_DIM, dtype=F32),
                         jnp.ones((HEAD_DIM, HEAD_DIM), F32)).astype(BF16)
    n_tiles = n // tm
    tile_in = lambda i: jnp.minimum(i, n_tiles - 1)
    tile_out = lambda i: jnp.maximum(i - 1, 0)
    qt_spec = pl.BlockSpec((tm // ROW_BLOCK, qd, ROW_BLOCK), lambda i: (tile_out(i), 0, 0))
    qt_shape = jax.ShapeDtypeStruct((n // ROW_BLOCK, qd, ROW_BLOCK), BF16)
    return pl.pallas_call(
        _qkv_body,
        grid=(n_tiles + 1,),
        in_specs=[
            pl.BlockSpec((tm, d), lambda i: (tile_in(i), 0)),
            pl.BlockSpec((1, d), lambda i: (0, 0)),
            pl.BlockSpec(w_qkv.shape, lambda i: (0, 0)),
            pl.BlockSpec(gains.shape, lambda i: (0, 0)),
            pl.BlockSpec((tm, LANES), lambda i: (tile_out(i) % nseq, 0)),
            pl.BlockSpec((tm, LANES), lambda i: (tile_out(i) % nseq, 0)),
            pl.BlockSpec(head_ones.shape, lambda i: (0, 0)),
        ],
        out_specs=[qt_spec, qt_spec,
                   pl.BlockSpec((tm, 2 * LANES), lambda i: (tile_out(i), 0)),
                   pl.BlockSpec((N_KV * VT_ROWS, tm), lambda i: (0, tile_out(i))),
                   pl.BlockSpec((N_KV * VT_ROWS, tm), lambda i: (0, tile_out(i)))],
        out_shape=[qt_shape, qt_shape,
                   jax.ShapeDtypeStruct((n, 2 * LANES), BF16),
                   jax.ShapeDtypeStruct((N_KV * VT_ROWS, n), BF16),
                   jax.ShapeDtypeStruct((N_KV * VT_ROWS, n), BF16)],
        scratch_shapes=[pltpu.VMEM((tm, w_qkv.shape[1]), F32)] * 2,
        compiler_params=pltpu.CompilerParams(
            dimension_semantics=("arbitrary",), vmem_limit_bytes=VMEM_LIMIT),
        name="qkv_proj",
    )(x2, gn, w_qkv, gains, cos_t, sin_t, head_ones)


def _query_rhs(q_ref, j, g):
    qt = jnp.concatenate(
        [q_ref[j, h * HEAD_DIM:(h + 1) * HEAD_DIM, :] for h in range(GROUP)], axis=1)
    zero = jnp.zeros_like(qt)
    return jnp.concatenate([jnp.where(g == 0, qt, zero), jnp.where(g == 0, zero, qt)], axis=0)


def _store_heads_t(o_ref, row0, o):
    rb = ROW_BLOCK
    o = jnp.concatenate([o[:, k * rb:(k + 1) * rb] for k in range(GROUP)], axis=0)
    o_ref[pl.ds(row0, rb), :] = o.T.astype(o_ref.dtype)


KEY_CHUNK = 256


def _global_attn_body(q_ref, k_ref, vt_ref, o_ref, s0, s1, m0, m1):
    s_buf, m_buf = (s0, s1), (m0, m1)
    g = pl.program_id(1)
    n_blocks = q_ref.shape[0]
    n_chunks = k_ref.shape[0] // KEY_CHUNK

    def keys(c):
        return slice(c * KEY_CHUNK, (c + 1) * KEY_CHUNK)

    def step(j_scores, j_finish, slot):
        m = None
        o = None
        if j_scores is not None:
            rhs = _query_rhs(q_ref, j_scores, g)
        for c in range(n_chunks):
            if j_scores is not None:
                s = jnp.dot(k_ref[keys(c), :], rhs, preferred_element_type=F32)
                s_buf[slot][keys(c), :] = s
                mc = jnp.max(s, axis=0, keepdims=True)
                m = mc if m is None else jnp.maximum(m, mc)
            if j_finish is not None:
                p = jnp.exp2(s_buf[1 - slot][keys(c), :] - m_buf[1 - slot][...]).astype(BF16)
                oc = jnp.dot(vt_ref[:, keys(c)], p, preferred_element_type=F32)
                o = oc if o is None else o + oc
        if j_scores is not None:
            m_buf[slot][...] = m
        if j_finish is not None:
            row0 = j_finish * ROW_BLOCK
            if not isinstance(row0, int):
                row0 = pl.multiple_of(row0, ROW_BLOCK)
            _store_heads_t(o_ref, row0, o[:HEAD_DIM] / o[HEAD_DIM:HEAD_DIM + 1])

    step(0, None, 0)

    def steady(jj, carry):
        j = 1 + 2 * jj
        step(j, j - 1, 1)
        step(j + 1, j, 0)
        return carry

    lax.fori_loop(0, (n_blocks - 2) // 2, steady, 0)
    step(n_blocks - 1, n_blocks - 2, 1)
    step(None, n_blocks - 1, 0)


def _global_attn_call(qt, kk, vt, batch, seq):
    n = kk.shape[0]
    nblk = seq // ROW_BLOCK
    cols = GROUP * ROW_BLOCK
    assert nblk % 2 == 0 and seq % KEY_CHUNK == 0
    return pl.pallas_call(
        _global_attn_body,
        grid=(batch, N_KV),
        in_specs=[
            pl.BlockSpec((nblk, GROUP * HEAD_DIM, ROW_BLOCK), lambda b, g: (b, g, 0)),
            pl.BlockSpec((seq, LANES), lambda b, g: (b, 0)),
            pl.BlockSpec((VT_ROWS, seq), lambda b, g: (g, b)),
        ],
        out_specs=pl.BlockSpec((seq, GROUP * HEAD_DIM), lambda b, g: (b, g)),
        out_shape=jax.ShapeDtypeStruct((n, N_HEADS * HEAD_DIM), BF16),
        scratch_shapes=(
            [pltpu.VMEM((seq, cols), F32)] * 2 + [pltpu.VMEM((1, cols), F32)] * 2),
        compiler_params=pltpu.CompilerParams(
            dimension_semantics=("arbitrary",) * 2, vmem_limit_bytes=VMEM_LIMIT),
        name="global_attn",
    )(qt, kk, vt)


WINDOW_BLOCKS = 8


def _window_attn_body(q_ref, k_ref, vl_ref, vm_ref, vr_ref, bias_ref, sink_ref, o_ref):
    w = WINDOW
    nb = WINDOW_BLOCKS
    nblk = k_ref.shape[0] // w
    g = pl.program_id(1)
    i = pl.program_id(2)
    n_steps = pl.num_programs(2)
    bias = bias_ref[...]
    sink = sink_ref[...]

    kt = []
    for t in range(-1, nb + 1):
        start = pl.multiple_of(jnp.clip(nb * i + t, 0, nblk - 1) * w, w)
        kt.append(k_ref[pl.ds(start, w), :])
    vt = [vl_ref[...]] + [vm_ref[:, t * w:(t + 1) * w] for t in range(nb)] + [vr_ref[...]]
    def scores(jb):
        kw = jnp.concatenate(kt[jb:jb + 3], axis=0)
        return jnp.dot(kw, _query_rhs(q_ref, jb, g), preferred_element_type=F32)

    s_next = scores(0)
    for jb in range(nb):
        s = s_next + bias
        if jb + 1 < nb:
            s_next = scores(jb + 1)
        rows = [s[t * w:(t + 1) * w] for t in range(3)]
        if jb == 0:
            rows[0] = jnp.where(i > 0, rows[0], NEG_INF)
        if jb == nb - 1:
            rows[2] = jnp.where(i < n_steps - 1, rows[2], NEG_INF)
        m = jnp.maximum(jnp.maximum(rows[0], rows[1]), rows[2])
        m = jnp.maximum(jnp.max(m, axis=0, keepdims=True), sink)
        p = jnp.concatenate([jnp.exp2(r - m).astype(BF16) for r in rows], axis=0)
        vw = jnp.concatenate(vt[jb:jb + 3], axis=1)
        o = jnp.dot(vw, p, preferred_element_type=F32)
        denom = o[HEAD_DIM:HEAD_DIM + 1] + jnp.exp2(sink - m)
        _store_heads_t(o_ref, jb * w, o[:HEAD_DIM] / denom)


def _window_attn_call(qt, kk, vt, bias, sink, batch, seq):
    n = kk.shape[0]
    w = WINDOW
    nb = WINDOW_BLOCKS
    tq = nb * w
    nq = seq // tq
    nblk = seq // w
    return pl.pallas_call(
        _window_attn_body,
        grid=(batch, N_KV, nq),
        in_specs=[
            pl.BlockSpec((nb, GROUP * HEAD_DIM, w), lambda b, g, i: (b * nq + i, g, 0)),
            pl.BlockSpec((seq, LANES), lambda b, g, i: (b, 1)),
            pl.BlockSpec((VT_ROWS, w),
                         lambda b, g, i: (g, b * nblk + jnp.maximum(nb * i - 1, 0))),
            pl.BlockSpec((VT_ROWS, tq), lambda b, g, i: (g, b * nq + i)),
            pl.BlockSpec((VT_ROWS, w),
                         lambda b, g, i: (g, b * nblk + jnp.minimum(nb * i + nb, nblk - 1))),
            pl.BlockSpec((None, 3 * w, GROUP * w), lambda b, g, i: (g, 0, 0)),
            pl.BlockSpec((None, 1, GROUP * w), lambda b, g, i: (g, 0, 0)),
        ],
        out_specs=pl.BlockSpec((tq, GROUP * HEAD_DIM), lambda b, g, i: (b * nq + i, g)),
        out_shape=jax.ShapeDtypeStruct((n, N_HEADS * HEAD_DIM), BF16),
        compiler_params=pltpu.CompilerParams(
            dimension_semantics=("arbitrary",) * 3, vmem_limit_bytes=VMEM_LIMIT),
        name="window_attn",
    )(qt, kk, vt, vt, vt, bias, sink)


def _mix_mlp_body(x_ref, oa_ref, ob_ref, gn_ref, wg_ref, bg_ref, woa_ref, wob_ref, wout_ref,
                  gm_ref, w1_ref, w2_ref, o_ref):
    d = x_ref.shape[1]
    x = x_ref[...]
    h = _rms(x, gn_ref[...]).astype(BF16)
    gates = jnp.dot(h, wg_ref[...], preferred_element_type=F32) + bg_ref[...]
    g = jax.nn.sigmoid(gates)
    ya = jnp.dot(oa_ref[...], woa_ref[...], preferred_element_type=F32)
    yb = jnp.dot(ob_ref[...], wob_ref[...], preferred_element_type=F32)
    mixed = g[:, :d] * ya + g[:, d:] * yb
    x = x + jnp.dot(mixed.astype(BF16), wout_ref[...], preferred_element_type=F32)
    h = _rms(x, gm_ref[...]).astype(BF16)
    u = jnp.square(jnp.maximum(jnp.dot(h, w1_ref[...], preferred_element_type=F32), 0.0))
    o_ref[...] = x + jnp.dot(u.astype(BF16), w2_ref[...], preferred_element_type=F32)


def _resident(arr):
    return pl.BlockSpec(arr.shape, lambda i: (0,) * arr.ndim, pipeline_mode=pl.Buffered(1))


def _mix_mlp_call(x2, oa, ob, gn, wg, bg, woa, wob, wout, gm, w1, w2, tm):
    n, d = x2.shape
    row = lambda width: pl.BlockSpec((tm, width), lambda i: (i, 0))
    consts = (gn, wg, bg, woa, wob, wout, gm, w1, w2)
    return pl.pallas_call(
        _mix_mlp_body,
        grid=(n // tm,),
        in_specs=[row(d), row(oa.shape[1]), row(ob.shape[1])] + [_resident(a) for a in consts],
        out_specs=row(d),
        out_shape=jax.ShapeDtypeStruct((n, d), F32),
        compiler_params=pltpu.CompilerParams(
            dimension_semantics=("arbitrary",), vmem_limit_bytes=VMEM_LIMIT),
        name="mix_mlp",
    )(x2, oa, ob, *consts)


def _rope_tables(seq):
    rows = seq // GRID_W
    row = jnp.repeat(jnp.arange(rows, dtype=jnp.int32), GRID_W)
    col = jnp.tile(jnp.arange(GRID_W, dtype=jnp.int32), rows)
    n_freq = HEAD_DIM // 4
    inv_freq = ROPE_THETA ** (-jnp.arange(n_freq, dtype=F32) / n_freq)
    ang_row = row.astype(F32)[:, None] * inv_freq[None, :]
    ang_col = col.astype(F32)[:, None] * inv_freq[None, :]
    cr, sr, cc, sc = jnp.cos(ang_row), jnp.sin(ang_row), jnp.cos(ang_col), jnp.sin(ang_col)
    cos64 = jnp.concatenate([cr, cr, cc, cc], axis=-1)
    sin64 = jnp.concatenate([-sr, sr, -sc, sc], axis=-1)
    return jnp.tile(cos64, (1, 2)), jnp.tile(sin64, (1, 2))


def _t5_bucket(rel):
    nb = N_BUCKETS // 2
    max_exact = nb // 2
    n = jnp.abs(rel)
    large = max_exact + (jnp.log(jnp.maximum(n, 1).astype(F32) / max_exact)
                         / math.log(MAX_DISTANCE / max_exact) * (nb - max_exact)).astype(jnp.int32)
    large = jnp.minimum(large, nb - 1)
    return jnp.where(rel > 0, nb, 0) + jnp.where(n < max_exact, n, large)


def _window_bias(rel_bias):
    w = WINDOW
    r = jnp.arange(w, dtype=jnp.int32)[:, None]
    j = jnp.arange(3 * w, dtype=jnp.int32)[None, :]
    rel = j - w - r
    bucket = _t5_bucket(rel)
    hit = bucket[None, :, :, None] == jnp.arange(N_BUCKETS, dtype=jnp.int32)
    bias = jnp.sum(jnp.where(hit, rel_bias.astype(F32).T[:, None, None, :], 0.0), axis=-1) * LOG2E
    bias = jnp.where(jnp.abs(rel)[None] <= w, bias, NEG_INF)
    return jnp.transpose(bias.reshape(N_KV, GROUP * w, 3 * w), (0, 2, 1))


def kernel(x, w_in, b_gate, qn_a, kn_a, qn_b, kn_b, w_o_a, w_o_b, w_out,
           sink_b, rel_bias, norm_mix, norm_mlp, w_mlp1, w_mlp2):
    batch, seq, d = x.shape
    depth = w_in.shape[0]
    n_qkv = 2 * (N_HEADS + 2 * N_KV) * HEAD_DIM
    cos_t, sin_t = _rope_tables(seq)
    bias = _window_bias(rel_bias)
    x2 = x.reshape(batch * seq, d)
    pair = lambda v: jnp.tile(v, 2)
    for l in range(depth):
        gains = jnp.zeros((8, LANES), F32).at[:4].set(
            jnp.stack([pair(qn_a[l]), pair(kn_a[l]), pair(qn_b[l]), pair(kn_b[l])]))
        qt_a, qt_b, kk, vt_a, vt_b = _qkv_call(
            x2, norm_mix[l][None, :], w_in[l, :, :n_qkv].astype(BF16),
            gains, cos_t, sin_t, seq, tm=256)
        oa = _global_attn_call(qt_a, kk, vt_a, batch, seq)
        sink = jnp.repeat(sink_b[l].astype(F32) * LOG2E, WINDOW).reshape(
            N_KV, 1, GROUP * WINDOW)
        ob = _window_attn_call(qt_b, kk, vt_b, bias, sink, batch, seq)
        x2 = _mix_mlp_call(
            x2, oa, ob, norm_mix[l][None, :], w_in[l, :, n_qkv:].astype(BF16),
            b_gate[l][None, :], w_o_a[l].astype(BF16), w_o_b[l].astype(BF16),
            w_out[l].astype(BF16), norm_mlp[l][None, :], w_mlp1[l].astype(BF16),
            w_mlp2[l].astype(BF16), tm=512)
    return x2.reshape(batch, seq, d)
```

```python
import math

import jax
import jax.numpy as jnp
from jax import lax
from jax.experimental import pallas as pl
from jax.experimental.pallas import tpu as pltpu

F32 = jnp.float32
BF16 = jnp.bfloat16

HEAD_DIM = 64
N_HEADS = 8
N_KV = 2
GROUP = N_HEADS // N_KV
GRID_W = 64
ROPE_THETA = 10000.0
WINDOW = 128
N_BUCKETS = 32
MAX_DISTANCE = 128
EPS = 1e-6
NEG_INF = -1e30

LANES = 128
VMEM_LIMIT = 56 * 1024 * 1024

ROW_BLOCK = 128
VT_ROWS = 80
LOG2E = math.log2(math.e)
Q_SCALE = HEAD_DIM ** -0.5 * LOG2E


def _rms(x, gain):
    ms = jnp.mean(x * x, axis=-1, keepdims=True)
    return x * lax.rsqrt(ms + EPS) * gain


def _qkv_body(x_ref, gn_ref, w_ref, gain_ref, cos_ref, sin_ref, ones_ref,
              qta_ref, qtb_ref, kk_ref, vta_ref, vtb_ref, p0, p1):
    i = pl.program_id(0)

    @pl.when(i == 0)
    def _():
        p1[...] = jnp.zeros_like(p1)

    @pl.when(i % 2 == 0)
    def _():
        _qkv_step(x_ref, gn_ref, w_ref, gain_ref, cos_ref, sin_ref, ones_ref,
                  qta_ref, qtb_ref, kk_ref, vta_ref, vtb_ref, p0, p1)

    @pl.when(i % 2 == 1)
    def _():
        _qkv_step(x_ref, gn_ref, w_ref, gain_ref, cos_ref, sin_ref, ones_ref,
                  qta_ref, qtb_ref, kk_ref, vta_ref, vtb_ref, p1, p0)


def _qkv_step(x_ref, gn_ref, w_ref, gain_ref, cos_ref, sin_ref, ones_ref,
              qta_ref, qtb_ref, kk_ref, vta_ref, vtb_ref, p_new, p_ref):
    tm = x_ref.shape[0]
    lane = lax.broadcasted_iota(jnp.int32, (tm, LANES), 1)
    first16 = (lane & 16) == 0
    cos = cos_ref[...]
    sin = sin_ref[...]

    def chunk(c):
        return p_ref[:, c * LANES:(c + 1) * LANES]

    def head_norm_pair(ca, gain_a, cb, gain_b):
        c2 = jnp.concatenate([ca, cb], axis=1)
        sq = c2 * c2
        hi = sq.astype(BF16)
        lo = (sq - hi.astype(F32)).astype(BF16)
        ss = (jnp.dot(hi, ones_ref[...], preferred_element_type=F32)
              + jnp.dot(lo, ones_ref[...], preferred_element_type=F32))
        y = c2 * lax.rsqrt(ss * (1.0 / HEAD_DIM) + EPS)
        return y[:, :LANES] * gain_a, y[:, LANES:] * gain_b

    def rope(c):
        ahead = pltpu.roll(c, LANES - 16, 1)
        behind = pltpu.roll(c, 16, 1)
        return c * cos + jnp.where(first16, ahead, behind) * sin

    def put_qt(ref, c, val):
        for r in range(tm // ROW_BLOCK):
            blk = val[r * ROW_BLOCK:(r + 1) * ROW_BLOCK, :]
            ref[r, c * LANES:(c + 1) * LANES, :] = blk.T.astype(BF16)

    scale = Q_SCALE
    qn_a, kn_a, qn_b, kn_b = (gain_ref[i:i + 1, :] for i in range(4))
    for c in (0, 2):
        qa = head_norm_pair(chunk(c), qn_a, chunk(c + 1), qn_a)
        qb = head_norm_pair(chunk(6 + c), qn_b, chunk(7 + c), qn_b)
        for d in range(2):
            put_qt(qta_ref, c + d, rope(qa[d]) * scale)
            put_qt(qtb_ref, c + d, qb[d] * scale)
    ka, kb = head_norm_pair(chunk(4), kn_a, chunk(10), kn_b)
    kk_ref[:, :LANES] = rope(ka).astype(BF16)
    kk_ref[:, LANES:] = kb.astype(BF16)
    pad_rows = VT_ROWS - HEAD_DIM
    ones_row = (lax.broadcasted_iota(jnp.int32, (pad_rows, tm), 0) == 0).astype(F32)
    for ref, c in ((vta_ref, 5), (vtb_ref, 11)):
        vt = chunk(c).T
        ref[...] = jnp.concatenate(
            [vt[:HEAD_DIM], ones_row, vt[HEAD_DIM:], ones_row], axis=0).astype(BF16)
    h = _rms(x_ref[...], gn_ref[...]).astype(BF16)
    p_new[...] = jnp.dot(h, w_ref[...], preferred_element_type=F32)


def _qkv_call(x2, gn, w_qkv, gains, cos_t, sin_t, seq, tm):
    n, d = x2.shape
    nseq = seq // tm
    qd = N_HEADS * HEAD_DIM
    head_ones = jnp.kron(jnp.eye(2 * LANES // HEAD_DIM, dtype=F32),
                         jnp.ones((HEAD_DIM, HEAD_DIM), F32)).astype(BF16)
    n_tiles = n // tm
    tile_in = lambda i: jnp.minimum(i, n_tiles - 1)
    tile_out = lambda i: jnp.maximum(i - 1, 0)
    qt_spec = pl.BlockSpec((tm // ROW_BLOCK, qd, ROW_BLOCK), lambda i: (tile_out(i), 0, 0))
    qt_shape = jax.ShapeDtypeStruct((n // ROW_BLOCK, qd, ROW_BLOCK), BF16)
    return pl.pallas_call(
        _qkv_body,
        grid=(n_tiles + 1,),
        in_specs=[
            pl.BlockSpec((tm, d), lambda i: (tile_in(i), 0)),
            pl.BlockSpec((1, d), lambda i: (0, 0)),
            pl.BlockSpec(w_qkv.shape, lambda i: (0, 0)),
            pl.BlockSpec(gains.shape, lambda i: (0, 0)),
            pl.BlockSpec((tm, LANES), lambda i: (tile_out(i) % nseq, 0)),
            pl.BlockSpec((tm, LANES), lambda i: (tile_out(i) % nseq, 0)),
            pl.BlockSpec(head_ones.shape, lambda i: (0, 0)),
        ],
        out_specs=[qt_spec, qt_spec,
                   pl.BlockSpec((tm, 2 * LANES), lambda i: (tile_out(i), 0)),
                   pl.BlockSpec((N_KV * VT_ROWS, tm), lambda i: (0, tile_out(i))),
                   pl.BlockSpec((N_KV * VT_ROWS, tm), lambda i: (0, tile_out(i)))],
        out_shape=[qt_shape, qt_shape,
                   jax.ShapeDtypeStruct((n, 2 * LANES), BF16),
                   jax.ShapeDtypeStruct((N_KV * VT_ROWS, n), BF16),
                   jax.ShapeDtypeStruct((N_KV * VT_ROWS, n), BF16)],
        scratch_shapes=[pltpu.VMEM((tm, w_qkv.shape[1]), F32)] * 2,
        compiler_params=pltpu.CompilerParams(
            dimension_semantics=("arbitrary",), vmem_limit_bytes=VMEM_LIMIT),
        name="qkv_proj",
    )(x2, gn, w_qkv, gains, cos_t, sin_t, head_ones)


def _query_rhs(q_ref, j, g):
    qt = jnp.concatenate(
        [q_ref[j, h * HEAD_DIM:(h + 1) * HEAD_DIM, :] for h in range(GROUP)], axis=1)
    zero = jnp.zeros_like(qt)
    return jnp.concatenate([jnp.where(g == 0, qt, zero), jnp.where(g == 0, zero, qt)], axis=0)


def _store_heads_t(o_ref, row0, o):
    rb = ROW_BLOCK
    o = jnp.concatenate([o[:, k * rb:(k + 1) * rb] for k in range(GROUP)], axis=0)
    o_ref[pl.ds(row0, rb), :] = o.T.astype(o_ref.dtype)


KEY_CHUNK = 256


def _global_attn_body(q_ref, k_ref, vt_ref, o_ref, s0, s1, m0, m1):
    s_buf, m_buf = (s0, s1), (m0, m1)
    g = pl.program_id(1)
    n_blocks = q_ref.shape[0]
    n_chunks = k_ref.shape[0] // KEY_CHUNK

    def keys(c):
        return slice(c * KEY_CHUNK, (c + 1) * KEY_CHUNK)

    def step(j_scores, j_finish, slot):
        m = None
        o = None
        if j_scores is not None:
            rhs = _query_rhs(q_ref, j_scores, g)
        for c in range(n_chunks):
            if j_scores is not None:
                s = jnp.dot(k_ref[keys(c), :], rhs, preferred_element_type=F32)
                s_buf[slot][keys(c), :] = s
                mc = jnp.max(s, axis=0, keepdims=True)
                m = mc if m is None else jnp.maximum(m, mc)
            if j_finish is not None:
                p = jnp.exp2(s_buf[1 - slot][keys(c), :] - m_buf[1 - slot][...]).astype(BF16)
                oc = jnp.dot(vt_ref[:, keys(c)], p, preferred_element_type=F32)
                o = oc if o is None else o + oc
        if j_scores is not None:
            m_buf[slot][...] = m
        if j_finish is not None:
            row0 = j_finish * ROW_BLOCK
            if not isinstance(row0, int):
                row0 = pl.multiple_of(row0, ROW_BLOCK)
            _store_heads_t(o_ref, row0, o[:HEAD_DIM] / o[HEAD_DIM:HEAD_DIM + 1])

    step(0, None, 0)

    def steady(jj, carry):
        j = 1 + 2 * jj
        step(j, j - 1, 1)
        step(j + 1, j, 0)
        return carry

    lax.fori_loop(0, (n_blocks - 2) // 2, steady, 0)
    step(n_blocks - 1, n_blocks - 2, 1)
    step(None, n_blocks - 1, 0)


def _global_attn_call(qt, kk, vt, batch, seq):
    n = kk.shape[0]
    nblk = seq // ROW_BLOCK
    cols = GROUP * ROW_BLOCK
    assert nblk % 2 == 0 and seq % KEY_CHUNK == 0
    return pl.pallas_call(
        _global_attn_body,
        grid=(batch, N_KV),
        in_specs=[
            pl.BlockSpec((nblk, GROUP * HEAD_DIM, ROW_BLOCK), lambda b, g: (b, g, 0)),
            pl.BlockSpec((seq, LANES), lambda b, g: (b, 0)),
            pl.BlockSpec((VT_ROWS, seq), lambda b, g: (g, b)),
        ],
        out_specs=pl.BlockSpec((seq, GROUP * HEAD_DIM), lambda b, g: (b, g)),
        out_shape=jax.ShapeDtypeStruct((n, N_HEADS * HEAD_DIM), BF16),
        scratch_shapes=(
            [pltpu.VMEM((seq, cols), F32)] * 2 + [pltpu.VMEM((1, cols), F32)] * 2),
        compiler_params=pltpu.CompilerParams(
            dimension_semantics=("arbitrary",) * 2, vmem_limit_bytes=VMEM_LIMIT),
        name="global_attn",
    )(qt, kk, vt)


def _window_attn_body(q_ref, k_ref, vt_ref, bias_ref, sink_ref, o_ref, s0, s1, m0, m1):
    w = WINDOW
    n_blocks = q_ref.shape[0]
    s_buf, m_buf = (s0, s1), (m0, m1)
    g = pl.program_id(1)
    sink = sink_ref[...]

    def band(j):
        lo, hi = max(j - 1, 0), min(j + 2, n_blocks)
        return slice(lo * w, hi * w), slice((lo - (j - 1)) * w, (hi - (j - 1)) * w)

    def step(j_scores, j_finish):
        if j_scores is not None:
            keys, rows = band(j_scores)
            n_keys = keys.stop - keys.start
            s = jnp.dot(k_ref[keys, :], _query_rhs(q_ref, j_scores, g),
                        preferred_element_type=F32) + bias_ref[rows, :]
            s_buf[j_scores % 2][:n_keys, :] = s
            m_buf[j_scores % 2][...] = jnp.maximum(jnp.max(s, axis=0, keepdims=True), sink)
        if j_finish is not None:
            keys, _ = band(j_finish)
            n_keys = keys.stop - keys.start
            m = m_buf[j_finish % 2][...]
            p = jnp.exp2(s_buf[j_finish % 2][:n_keys, :] - m).astype(BF16)
            o = jnp.dot(vt_ref[:, keys], p, preferred_element_type=F32)
            denom = o[HEAD_DIM:HEAD_DIM + 1] + jnp.exp2(sink - m)
            _store_heads_t(o_ref, j_finish * w, o[:HEAD_DIM] / denom)

    step(0, None)
    for j in range(1, n_blocks):
        step(j, j - 1)
    step(None, n_blocks - 1)


def _window_attn_call(qt, kk, vt, bias, sink, batch, seq):
    n = kk.shape[0]
    w = WINDOW
    nblk = seq // w
    cols = GROUP * w
    return pl.pallas_call(
        _window_attn_body,
        grid=(batch, N_KV),
        in_specs=[
            pl.BlockSpec((nblk, GROUP * HEAD_DIM, w), lambda b, g: (b, g, 0)),
            pl.BlockSpec((seq, LANES), lambda b, g: (b, 1)),
            pl.BlockSpec((VT_ROWS, seq), lambda b, g: (g, b)),
            pl.BlockSpec((None, 3 * w, cols), lambda b, g: (g, 0, 0)),
            pl.BlockSpec((None, 1, cols), lambda b, g: (g, 0, 0)),
        ],
        out_specs=pl.BlockSpec((seq, GROUP * HEAD_DIM), lambda b, g: (b, g)),
        out_shape=jax.ShapeDtypeStruct((n, N_HEADS * HEAD_DIM), BF16),
        scratch_shapes=(
            [pltpu.VMEM((3 * w, cols), F32)] * 2 + [pltpu.VMEM((1, cols), F32)] * 2),
        compiler_params=pltpu.CompilerParams(
            dimension_semantics=("arbitrary",) * 2, vmem_limit_bytes=VMEM_LIMIT),
        name="window_attn",
    )(qt, kk, vt, bias, sink)


def _mix_mlp_body(x_ref, oa_ref, ob_ref, gn_ref, wg_ref, bg_ref, woa_ref, wob_ref, wout_ref,
                  gm_ref, w1_ref, w2_ref, o_ref):
    d = x_ref.shape[1]
    x = x_ref[...]
    h = _rms(x, gn_ref[...]).astype(BF16)
    gates = jnp.dot(h, wg_ref[...], preferred_element_type=F32) + bg_ref[...]
    g = jax.nn.sigmoid(gates)
    ya = jnp.dot(oa_ref[...], woa_ref[...], preferred_element_type=F32)
    yb = jnp.dot(ob_ref[...], wob_ref[...], preferred_element_type=F32)
    mixed = g[:, :d] * ya + g[:, d:] * yb
    x = x + jnp.dot(mixed.astype(BF16), wout_ref[...], preferred_element_type=F32)
    h = _rms(x, gm_ref[...]).astype(BF16)
    u = jnp.square(jnp.maximum(jnp.dot(h, w1_ref[...], preferred_element_type=F32), 0.0))
    o_ref[...] = x + jnp.dot(u.astype(BF16), w2_ref[...], preferred_element_type=F32)


def _resident(arr):
    return pl.BlockSpec(arr.shape, lambda i: (0,) * arr.ndim, pipeline_mode=pl.Buffered(1))


def _mix_mlp_call(x2, oa, ob, gn, wg, bg, woa, wob, wout, gm, w1, w2, tm):
    n, d = x2.shape
    row = lambda width: pl.BlockSpec((tm, width), lambda i: (i, 0))
    consts = (gn, wg, bg, woa, wob, wout, gm, w1, w2)
    return pl.pallas_call(
        _mix_mlp_body,
        grid=(n // tm,),
        in_specs=[row(d), row(oa.shape[1]), row(ob.shape[1])] + [_resident(a) for a in consts],
        out_specs=row(d),
        out_shape=jax.ShapeDtypeStruct((n, d), F32),
        compiler_params=pltpu.CompilerParams(
            dimension_semantics=("arbitrary",), vmem_limit_bytes=VMEM_LIMIT),
        name="mix_mlp",
    )(x2, oa, ob, *consts)


def _rope_tables(seq):
    rows = seq // GRID_W
    row = jnp.repeat(jnp.arange(rows, dtype=jnp.int32), GRID_W)
    col = jnp.tile(jnp.arange(GRID_W, dtype=jnp.int32), rows)
    n_freq = HEAD_DIM // 4
    inv_freq = ROPE_THETA ** (-jnp.arange(n_freq, dtype=F32) / n_freq)
    ang_row = row.astype(F32)[:, None] * inv_freq[None, :]
    ang_col = col.astype(F32)[:, None] * inv_freq[None, :]
    cr, sr, cc, sc = jnp.cos(ang_row), jnp.sin(ang_row), jnp.cos(ang_col), jnp.sin(ang_col)
    cos64 = jnp.concatenate([cr, cr, cc, cc], axis=-1)
    sin64 = jnp.concatenate([-sr, sr, -sc, sc], axis=-1)
    return jnp.tile(cos64, (1, 2)), jnp.tile(sin64, (1, 2))


def _t5_bucket(rel):
    nb = N_BUCKETS // 2
    max_exact = nb // 2
    n = jnp.abs(rel)
    large = max_exact + (jnp.log(jnp.maximum(n, 1).astype(F32) / max_exact)
                         / math.log(MAX_DISTANCE / max_exact) * (nb - max_exact)).astype(jnp.int32)
    large = jnp.minimum(large, nb - 1)
    return jnp.where(rel > 0, nb, 0) + jnp.where(n < max_exact, n, large)


def _window_bias(rel_bias):
    w = WINDOW
    r = jnp.arange(w, dtype=jnp.int32)[:, None]
    j = jnp.arange(3 * w, dtype=jnp.int32)[None, :]
    rel = j - w - r
    bucket = _t5_bucket(rel)
    hit = bucket[None, :, :, None] == jnp.arange(N_BUCKETS, dtype=jnp.int32)
    bias = jnp.sum(jnp.where(hit, rel_bias.astype(F32).T[:, None, None, :], 0.0), axis=-1) * LOG2E
    bias = jnp.where(jnp.abs(rel)[None] <= w, bias, NEG_INF)
    return jnp.transpose(bias.reshape(N_KV, GROUP * w, 3 * w), (0, 2, 1))


def kernel(x, w_in, b_gate, qn_a, kn_a, qn_b, kn_b, w_o_a, w_o_b, w_out,
           sink_b, rel_bias, norm_mix, norm_mlp, w_mlp1, w_mlp2):
    batch, seq, d = x.shape
    depth = w_in.shape[0]
    n_qkv = 2 * (N_HEADS + 2 * N_KV) * HEAD_DIM
    cos_t, sin_t = _rope_tables(seq)
    bias = _window_bias(rel_bias)
    x2 = x.reshape(batch * seq, d)
    pair = lambda v: jnp.tile(v, 2)
    for l in range(depth):
        gains = jnp.zeros((8, LANES), F32).at[:4].set(
            jnp.stack([pair(qn_a[l]), pair(kn_a[l]), pair(qn_b[l]), pair(kn_b[l])]))
        qt_a, qt_b, kk, vt_a, vt_b = _qkv_call(
            x2, norm_mix[l][None, :], w_in[l, :, :n_qkv].astype(BF16),
            gains, cos_t, sin_t, seq, tm=256)
        oa = _global_attn_call(qt_a, kk, vt_a, batch, seq)
        sink = jnp.repeat(sink_b[l].astype(F32) * LOG2E, WINDOW).reshape(
            N_KV, 1, GROUP * WINDOW)
        ob = _window_attn_call(qt_b, kk, vt_b, bias, sink, batch, seq)
        x2 = _mix_mlp_call(
            x2, oa, ob, norm_mix[l][None, :], w_in[l, :, n_qkv:].astype(BF16),
            b_gate[l][None, :], w_o_a[l].astype(BF16), w_o_b[l].astype(BF16),
            w_out[l].astype(BF16), norm_mlp[l][None, :], w_mlp1[l].astype(BF16),
            w_mlp2[l].astype(BF16), tm=512)
    return x2.reshape(batch, seq, d)
```

```python
import math

import jax
import jax.numpy as jnp
from jax import lax
from jax.experimental import pallas as pl
from jax.experimental.pallas import tpu as pltpu

F32 = jnp.float32
BF16 = jnp.bfloat16

HEAD_DIM = 64
N_HEADS = 8
N_KV = 2
GROUP = N_HEADS // N_KV
GRID_W = 64
ROPE_THETA = 10000.0
WINDOW = 128
N_BUCKETS = 32
MAX_DISTANCE = 128
EPS = 1e-6
NEG_INF = -1e30

LANES = 128
VMEM_LIMIT = 56 * 1024 * 1024

ROW_BLOCK = 128
VT_ROWS = 80
LOG2E = math.log2(math.e)
Q_SCALE = HEAD_DIM ** -0.5 * LOG2E


def _rms(x, gain):
    ms = jnp.mean(x * x, axis=-1, keepdims=True)
    return x * lax.rsqrt(ms + EPS) * gain


def _qkv_body(*refs):
    *io_refs, p0, p1 = refs
    i = pl.program_id(0)

    @pl.when(i == 0)
    def _():
        p1[...] = jnp.zeros_like(p1)

    @pl.when(i % 2 == 0)
    def _():
        _qkv_step(*io_refs, p0, p1)

    @pl.when(i % 2 == 1)
    def _():
        _qkv_step(*io_refs, p1, p0)


def _qkv_step(x_ref, gn_ref, w_ref, gain_ref, gq_ref, cos_ref, sin_ref, cost_ref, sint_ref,
              ones_ref, qta_ref, qtb_ref, kk_ref, vta_ref, vtb_ref, p_new, p_ref):
    tm = x_ref.shape[0]
    lane = lax.broadcasted_iota(jnp.int32, (tm, LANES), 1)
    first16 = (lane & 16) == 0
    cos = cos_ref[...]
    sin = sin_ref[...]

    def chunk(c):
        return p_ref[:, c * LANES:(c + 1) * LANES]

    def head_norm_pair(ca, gain_a, cb, gain_b):
        c2 = jnp.concatenate([ca, cb], axis=1)
        sq = c2 * c2
        hi = sq.astype(BF16)
        lo = (sq - hi.astype(F32)).astype(BF16)
        ss = (jnp.dot(hi, ones_ref[...], preferred_element_type=F32)
              + jnp.dot(lo, ones_ref[...], preferred_element_type=F32))
        y = c2 * lax.rsqrt(ss * (1.0 / HEAD_DIM) + EPS)
        return y[:, :LANES] * gain_a, y[:, LANES:] * gain_b

    def rope(c):
        ahead = pltpu.roll(c, LANES - 16, 1)
        behind = pltpu.roll(c, 16, 1)
        return c * cos + jnp.where(first16, ahead, behind) * sin

    def finish_q(ref, c, pc, gain_t, rotary):
        for r in range(tm // ROW_BLOCK):
            cols = slice(r * ROW_BLOCK, (r + 1) * ROW_BLOCK)
            bt = p_ref[cols, pc * LANES:(pc + 1) * LANES].T
            halves = []
            for hh in range(LANES // HEAD_DIM):
                xh = bt[hh * HEAD_DIM:(hh + 1) * HEAD_DIM]
                ms = jnp.sum(xh * xh, axis=0, keepdims=True) * (1.0 / HEAD_DIM)
                halves.append(xh * lax.rsqrt(ms + EPS))
            y = jnp.concatenate(halves, axis=0) * gain_t
            if rotary:
                slabs = [y[t * 16:(t + 1) * 16] for t in range(LANES // 16)]
                partner = jnp.concatenate(
                    [slabs[t ^ 1] for t in range(LANES // 16)], axis=0)
                y = y * cost_ref[:, cols] + partner * sint_ref[:, cols]
            ref[r, c * LANES:(c + 1) * LANES, :] = y.astype(BF16)

    kn_a, kn_b = gain_ref[0:1, :], gain_ref[1:2, :]
    for c in range(4):
        finish_q(qta_ref, c, c, gq_ref[:LANES, :], True)
        finish_q(qtb_ref, c, 6 + c, gq_ref[LANES:, :], False)
    ka, kb = head_norm_pair(chunk(4), kn_a, chunk(10), kn_b)
    kk_ref[:, :LANES] = rope(ka).astype(BF16)
    kk_ref[:, LANES:] = kb.astype(BF16)
    pad_rows = VT_ROWS - HEAD_DIM
    ones_row = (lax.broadcasted_iota(jnp.int32, (pad_rows, tm), 0) == 0).astype(F32)
    for ref, c in ((vta_ref, 5), (vtb_ref, 11)):
        vt = chunk(c).T
        ref[...] = jnp.concatenate(
            [vt[:HEAD_DIM], ones_row, vt[HEAD_DIM:], ones_row], axis=0).astype(BF16)
    h = _rms(x_ref[...], gn_ref[...]).astype(BF16)
    p_new[...] = jnp.dot(h, w_ref[...], preferred_element_type=F32)


def _qkv_call(x2, gn, w_qkv, gains, gq, cos_t, sin_t, seq, tm):
    n, d = x2.shape
    cos_tt, sin_tt = cos_t.T, sin_t.T
    nseq = seq // tm
    qd = N_HEADS * HEAD_DIM
    head_ones = jnp.kron(jnp.eye(2 * LANES // HEAD_DIM, dtype=F32),
                         jnp.ones((HEAD_DIM, HEAD_DIM), F32)).astype(BF16)
    n_tiles = n // tm
    tile_in = lambda i: jnp.minimum(i, n_tiles - 1)
    tile_out = lambda i: jnp.maximum(i - 1, 0)
    qt_spec = pl.BlockSpec((tm // ROW_BLOCK, qd, ROW_BLOCK), lambda i: (tile_out(i), 0, 0))
    qt_shape = jax.ShapeDtypeStruct((n // ROW_BLOCK, qd, ROW_BLOCK), BF16)
    return pl.pallas_call(
        _qkv_body,
        grid=(n_tiles + 1,),
        in_specs=[
            pl.BlockSpec((tm, d), lambda i: (tile_in(i), 0)),
            pl.BlockSpec((1, d), lambda i: (0, 0)),
            pl.BlockSpec(w_qkv.shape, lambda i: (0, 0)),
            pl.BlockSpec(gains.shape, lambda i: (0, 0)),
            pl.BlockSpec(gq.shape, lambda i: (0, 0)),
            pl.BlockSpec((tm, LANES), lambda i: (tile_out(i) % nseq, 0)),
            pl.BlockSpec((tm, LANES), lambda i: (tile_out(i) % nseq, 0)),
            pl.BlockSpec((LANES, tm), lambda i: (0, tile_out(i) % nseq)),
            pl.BlockSpec((LANES, tm), lambda i: (0, tile_out(i) % nseq)),
            pl.BlockSpec(head_ones.shape, lambda i: (0, 0)),
        ],
        out_specs=[qt_spec, qt_spec,
                   pl.BlockSpec((tm, 2 * LANES), lambda i: (tile_out(i), 0)),
                   pl.BlockSpec((N_KV * VT_ROWS, tm), lambda i: (0, tile_out(i))),
                   pl.BlockSpec((N_KV * VT_ROWS, tm), lambda i: (0, tile_out(i)))],
        out_shape=[qt_shape, qt_shape,
                   jax.ShapeDtypeStruct((n, 2 * LANES), BF16),
                   jax.ShapeDtypeStruct((N_KV * VT_ROWS, n), BF16),
                   jax.ShapeDtypeStruct((N_KV * VT_ROWS, n), BF16)],
        scratch_shapes=[pltpu.VMEM((tm, w_qkv.shape[1]), F32)] * 2,
        compiler_params=pltpu.CompilerParams(
            dimension_semantics=("arbitrary",), vmem_limit_bytes=VMEM_LIMIT),
        name="qkv_proj",
    )(x2, gn, w_qkv, gains, gq, cos_t, sin_t, cos_tt, sin_tt, head_ones)


def _query_rhs(q_ref, j, g):
    qt = jnp.concatenate(
        [q_ref[j, h * HEAD_DIM:(h + 1) * HEAD_DIM, :] for h in range(GROUP)], axis=1)
    zero = jnp.zeros_like(qt)
    return jnp.concatenate([jnp.where(g == 0, qt, zero), jnp.where(g == 0, zero, qt)], axis=0)


def _store_heads_t(o_ref, row0, o):
    rb = ROW_BLOCK
    o = jnp.concatenate([o[:, k * rb:(k + 1) * rb] for k in range(GROUP)], axis=0)
    o_ref[pl.ds(row0, rb), :] = o.T.astype(o_ref.dtype)


KEY_CHUNK = 256
GLOBAL_UNROLL = 4


def _global_attn_body(q_ref, k_ref, vt_ref, o_ref, s0, s1, m0, m1):
    s_buf, m_buf = (s0, s1), (m0, m1)
    g = pl.program_id(1)
    n_blocks = q_ref.shape[0]
    n_chunks = k_ref.shape[0] // KEY_CHUNK

    def keys(c):
        return slice(c * KEY_CHUNK, (c + 1) * KEY_CHUNK)

    def step(j_scores, j_finish, slot):
        m = None
        o = None
        if j_scores is not None:
            rhs = _query_rhs(q_ref, j_scores, g)
        for c in range(n_chunks):
            if j_scores is not None:
                s = jnp.dot(k_ref[keys(c), :], rhs, preferred_element_type=F32)
                s_buf[slot][keys(c), :] = s
                mc = jnp.max(s, axis=0, keepdims=True)
                m = mc if m is None else jnp.maximum(m, mc)
            if j_finish is not None:
                p = jnp.exp2(s_buf[1 - slot][keys(c), :] - m_buf[1 - slot][...]).astype(BF16)
                oc = jnp.dot(vt_ref[:, keys(c)], p, preferred_element_type=F32)
                o = oc if o is None else o + oc
        if j_scores is not None:
            m_buf[slot][...] = m
        if j_finish is not None:
            row0 = j_finish * ROW_BLOCK
            if not isinstance(row0, int):
                row0 = pl.multiple_of(row0, ROW_BLOCK)
            _store_heads_t(o_ref, row0, o[:HEAD_DIM] / o[HEAD_DIM:HEAD_DIM + 1])

    step(0, None, 0)
    unroll = GLOBAL_UNROLL
    n_loop = (n_blocks - 1) // unroll

    def steady(jj, carry):
        j = 1 + unroll * jj
        for u in range(unroll):
            step(j + u, j + u - 1, (1 + u) % 2)
        return carry

    lax.fori_loop(0, n_loop, steady, 0)
    for j in range(1 + unroll * n_loop, n_blocks):
        step(j, j - 1, j % 2)
    step(None, n_blocks - 1, n_blocks % 2)


def _global_attn_call(qt, kk, vt, batch, seq):
    n = kk.shape[0]
    nblk = seq // ROW_BLOCK
    cols = GROUP * ROW_BLOCK
    assert nblk % 2 == 0 and seq % KEY_CHUNK == 0
    return pl.pallas_call(
        _global_attn_body,
        grid=(batch, N_KV),
        in_specs=[
            pl.BlockSpec((nblk, GROUP * HEAD_DIM, ROW_BLOCK), lambda b, g: (b, g, 0)),
            pl.BlockSpec((seq, LANES), lambda b, g: (b, 0)),
            pl.BlockSpec((VT_ROWS, seq), lambda b, g: (g, b)),
        ],
        out_specs=pl.BlockSpec((seq, GROUP * HEAD_DIM), lambda b, g: (b, g)),
        out_shape=jax.ShapeDtypeStruct((n, N_HEADS * HEAD_DIM), BF16),
        scratch_shapes=(
            [pltpu.VMEM((seq, cols), F32)] * 2 + [pltpu.VMEM((1, cols), F32)] * 2),
        compiler_params=pltpu.CompilerParams(
            dimension_semantics=("arbitrary",) * 2, vmem_limit_bytes=VMEM_LIMIT),
        name="global_attn",
    )(qt, kk, vt)


def _window_attn_body(q_ref, k_ref, vt_ref, bias_ref, sink_ref, o_ref, s0, s1, m0, m1):
    w = WINDOW
    n_blocks = q_ref.shape[0]
    s_buf, m_buf = (s0, s1), (m0, m1)
    g = pl.program_id(1)
    sink = sink_ref[...]

    def band(j):
        lo, hi = max(j - 1, 0), min(j + 2, n_blocks)
        return slice(lo * w, hi * w), slice((lo - (j - 1)) * w, (hi - (j - 1)) * w)

    def step(j_scores, j_finish):
        if j_scores is not None:
            keys, rows = band(j_scores)
            n_keys = keys.stop - keys.start
            s = jnp.dot(k_ref[keys, :], _query_rhs(q_ref, j_scores, g),
                        preferred_element_type=F32) + bias_ref[rows, :]
            s_buf[j_scores % 2][:n_keys, :] = s
            m_buf[j_scores % 2][...] = jnp.maximum(jnp.max(s, axis=0, keepdims=True), sink)
        if j_finish is not None:
            keys, _ = band(j_finish)
            n_keys = keys.stop - keys.start
            m = m_buf[j_finish % 2][...]
            p = jnp.exp2(s_buf[j_finish % 2][:n_keys, :] - m).astype(BF16)
            o = jnp.dot(vt_ref[:, keys], p, preferred_element_type=F32)
            denom = o[HEAD_DIM:HEAD_DIM + 1] + jnp.exp2(sink - m)
            _store_heads_t(o_ref, j_finish * w, o[:HEAD_DIM] / denom)

    step(0, None)
    for j in range(1, n_blocks):
        step(j, j - 1)
    step(None, n_blocks - 1)


def _window_attn_call(qt, kk, vt, bias, sink, batch, seq):
    n = kk.shape[0]
    w = WINDOW
    nblk = seq // w
    cols = GROUP * w
    return pl.pallas_call(
        _window_attn_body,
        grid=(batch, N_KV),
        in_specs=[
            pl.BlockSpec((nblk, GROUP * HEAD_DIM, w), lambda b, g: (b, g, 0)),
            pl.BlockSpec((seq, LANES), lambda b, g: (b, 1)),
            pl.BlockSpec((VT_ROWS, seq), lambda b, g: (g, b)),
            pl.BlockSpec((None, 3 * w, cols), lambda b, g: (g, 0, 0)),
            pl.BlockSpec((None, 1, cols), lambda b, g: (g, 0, 0)),
        ],
        out_specs=pl.BlockSpec((seq, GROUP * HEAD_DIM), lambda b, g: (b, g)),
        out_shape=jax.ShapeDtypeStruct((n, N_HEADS * HEAD_DIM), BF16),
        scratch_shapes=(
            [pltpu.VMEM((3 * w, cols), F32)] * 2 + [pltpu.VMEM((1, cols), F32)] * 2),
        compiler_params=pltpu.CompilerParams(
            dimension_semantics=("arbitrary",) * 2, vmem_limit_bytes=VMEM_LIMIT),
        name="window_attn",
    )(qt, kk, vt, bias, sink)


def _mix_mlp_body(x_ref, oa_ref, ob_ref, gn_ref, wg_ref, bg_ref, woa_ref, wob_ref, wout_ref,
                  gm_ref, w1_ref, w2_ref, o_ref):
    d = x_ref.shape[1]
    x = x_ref[...]
    h = _rms(x, gn_ref[...]).astype(BF16)
    gates = jnp.dot(h, wg_ref[...], preferred_element_type=F32) + bg_ref[...]
    g = jax.nn.sigmoid(gates)
    ya = jnp.dot(oa_ref[...], woa_ref[...], preferred_element_type=F32)
    yb = jnp.dot(ob_ref[...], wob_ref[...], preferred_element_type=F32)
    mixed = g[:, :d] * ya + g[:, d:] * yb
    x = x + jnp.dot(mixed.astype(BF16), wout_ref[...], preferred_element_type=F32)
    h = _rms(x, gm_ref[...]).astype(BF16)
    u = jnp.square(jnp.maximum(jnp.dot(h, w1_ref[...], preferred_element_type=F32), 0.0))
    o_ref[...] = x + jnp.dot(u.astype(BF16), w2_ref[...], preferred_element_type=F32)


def _resident(arr):
    return pl.BlockSpec(arr.shape, lambda i: (0,) * arr.ndim, pipeline_mode=pl.Buffered(1))


def _mix_mlp_call(x2, oa, ob, gn, wg, bg, woa, wob, wout, gm, w1, w2, tm):
    n, d = x2.shape
    row = lambda width: pl.BlockSpec((tm, width), lambda i: (i, 0))
    consts = (gn, wg, bg, woa, wob, wout, gm, w1, w2)
    return pl.pallas_call(
        _mix_mlp_body,
        grid=(n // tm,),
        in_specs=[row(d), row(oa.shape[1]), row(ob.shape[1])] + [_resident(a) for a in consts],
        out_specs=row(d),
        out_shape=jax.ShapeDtypeStruct((n, d), F32),
        compiler_params=pltpu.CompilerParams(
            dimension_semantics=("arbitrary",), vmem_limit_bytes=VMEM_LIMIT),
        name="mix_mlp",
    )(x2, oa, ob, *consts)


def _rope_tables(seq):
    rows = seq // GRID_W
    row = jnp.repeat(jnp.arange(rows, dtype=jnp.int32), GRID_W)
    col = jnp.tile(jnp.arange(GRID_W, dtype=jnp.int32), rows)
    n_freq = HEAD_DIM // 4
    inv_freq = ROPE_THETA ** (-jnp.arange(n_freq, dtype=F32) / n_freq)
    ang_row = row.astype(F32)[:, None] * inv_freq[None, :]
    ang_col = col.astype(F32)[:, None] * inv_freq[None, :]
    cr, sr, cc, sc = jnp.cos(ang_row), jnp.sin(ang_row), jnp.cos(ang_col), jnp.sin(ang_col)
    cos64 = jnp.concatenate([cr, cr, cc, cc], axis=-1)
    sin64 = jnp.concatenate([-sr, sr, -sc, sc], axis=-1)
    return jnp.tile(cos64, (1, 2)), jnp.tile(sin64, (1, 2))


def _t5_bucket(rel):
    nb = N_BUCKETS // 2
    max_exact = nb // 2
    n = jnp.abs(rel)
    large = max_exact + (jnp.log(jnp.maximum(n, 1).astype(F32) / max_exact)
                         / math.log(MAX_DISTANCE / max_exact) * (nb - max_exact)).astype(jnp.int32)
    large = jnp.minimum(large, nb - 1)
    return jnp.where(rel > 0, nb, 0) + jnp.where(n < max_exact, n, large)


def _window_bias(rel_bias):
    w = WINDOW
    period = 4 * w
    slot = jnp.arange(period, dtype=jnp.int32)
    rel = jnp.where(slot < 3 * w, slot - w, slot - 5 * w)
    table = rel_bias.astype(F32)[_t5_bucket(rel)].T * LOG2E
    table = jnp.where(jnp.abs(rel)[None] <= w, table, NEG_INF)
    band = jnp.tile(table, (1, w))[:, :w * (period - 1)].reshape(-1, w, period - 1)
    band = band[:, :, :3 * w]
    return jnp.transpose(band.reshape(N_KV, GROUP * w, 3 * w), (0, 2, 1))


def kernel(x, w_in, b_gate, qn_a, kn_a, qn_b, kn_b, w_o_a, w_o_b, w_out,
           sink_b, rel_bias, norm_mix, norm_mlp, w_mlp1, w_mlp2):
    batch, seq, d = x.shape
    depth = w_in.shape[0]
    n_qkv = 2 * (N_HEADS + 2 * N_KV) * HEAD_DIM
    cos_t, sin_t = _rope_tables(seq)
    bias = _window_bias(rel_bias)
    x2 = x.reshape(batch * seq, d)
    pair = lambda v: jnp.tile(v, 2)
    for l in range(depth):
        gains = jnp.zeros((8, LANES), F32).at[:2].set(
            jnp.stack([pair(kn_a[l]), pair(kn_b[l])]))
        gq = jnp.concatenate([
            jnp.broadcast_to((pair(qn) * Q_SCALE)[:, None], (LANES, LANES))
            for qn in (qn_a[l], qn_b[l])], axis=0)
        qt_a, qt_b, kk, vt_a, vt_b = _qkv_call(
            x2, norm_mix[l][None, :], w_in[l, :, :n_qkv].astype(BF16),
            gains, gq, cos_t, sin_t, seq, tm=256)
        oa = _global_attn_call(qt_a, kk, vt_a, batch, seq)
        sink = jnp.repeat(sink_b[l].astype(F32) * LOG2E, WINDOW).reshape(
            N_KV, 1, GROUP * WINDOW)
        ob = _window_attn_call(qt_b, kk, vt_b, bias, sink, batch, seq)
        x2 = _mix_mlp_call(
            x2, oa, ob, norm_mix[l][None, :], w_in[l, :, n_qkv:].astype(BF16),
            b_gate[l][None, :], w_o_a[l].astype(BF16), w_o_b[l].astype(BF16),
            w_out[l].astype(BF16), norm_mlp[l][None, :], w_mlp1[l].astype(BF16),
            w_mlp2[l].astype(BF16), tm=512)
    return x2.reshape(batch, seq, d)
```

```python
import math

import jax
import jax.numpy as jnp
from jax import lax
from jax.experimental import pallas as pl
from jax.experimental.pallas import tpu as pltpu

F32 = jnp.float32
BF16 = jnp.bfloat16

HEAD_DIM = 64
N_HEADS = 8
N_KV = 2
GROUP = N_HEADS // N_KV
GRID_W = 64
ROPE_THETA = 10000.0
WINDOW = 128
N_BUCKETS = 32
MAX_DISTANCE = 128
EPS = 1e-6
NEG_INF = -1e30

LANES = 128
SUBLANES = 8
VMEM_LIMIT = 56 * 1024 * 1024
ROT = HEAD_DIM // 4

QKV_TILE = 512
MLP_TILE = 512

ROW_BLOCK = 128
VT_ROWS = 80
LOG2E = math.log2(math.e)
Q_SCALE = HEAD_DIM ** -0.5 * LOG2E


def _rms(x, gain):
    ms = jnp.mean(x * x, axis=-1, keepdims=True)
    return x * lax.rsqrt(ms + EPS) * gain


def _qkv_body(*refs):
    *io_refs, p0, p1 = refs
    i = pl.program_id(0)

    @pl.when(i == 0)
    def _():
        p1[...] = jnp.zeros_like(p1)

    @pl.when(i % 2 == 0)
    def _():
        _qkv_step(*io_refs, p0, p1)

    @pl.when(i % 2 == 1)
    def _():
        _qkv_step(*io_refs, p1, p0)


def _qkv_step(x_ref, gn_ref, w_ref, gain_ref, gq_ref, cos_ref, sin_ref, cost_ref, sint_ref,
              ones_ref, qta_ref, qtb_ref, kk_ref, vta_ref, vtb_ref, p_new, p_ref):
    tm = x_ref.shape[0]
    lane = lax.broadcasted_iota(jnp.int32, (tm, LANES), 1)
    pair_first = (lane & ROT) == 0
    cos = cos_ref[...]
    sin = sin_ref[...]

    def chunk(c):
        return p_ref[:, c * LANES:(c + 1) * LANES]

    def head_norm_pair(ca, gain_a, cb, gain_b):
        c2 = jnp.concatenate([ca, cb], axis=1)
        sq = c2 * c2
        hi = sq.astype(BF16)
        lo = (sq - hi.astype(F32)).astype(BF16)
        ss = (jnp.dot(hi, ones_ref[...], preferred_element_type=F32)
              + jnp.dot(lo, ones_ref[...], preferred_element_type=F32))
        y = c2 * lax.rsqrt(ss * (1.0 / HEAD_DIM) + EPS)
        return y[:, :LANES] * gain_a, y[:, LANES:] * gain_b

    def rope(c):
        ahead = pltpu.roll(c, LANES - ROT, 1)
        behind = pltpu.roll(c, ROT, 1)
        return c * cos + jnp.where(pair_first, ahead, behind) * sin

    def finish_q(ref, c, pc, gain_t, rotary):
        for r in range(tm // ROW_BLOCK):
            cols = slice(r * ROW_BLOCK, (r + 1) * ROW_BLOCK)
            bt = p_ref[cols, pc * LANES:(pc + 1) * LANES].T
            halves = []
            for hh in range(LANES // HEAD_DIM):
                xh = bt[hh * HEAD_DIM:(hh + 1) * HEAD_DIM]
                ms = jnp.sum(xh * xh, axis=0, keepdims=True) * (1.0 / HEAD_DIM)
                halves.append(xh * lax.rsqrt(ms + EPS))
            y = jnp.concatenate(halves, axis=0) * gain_t
            if rotary:
                slabs = [y[t * ROT:(t + 1) * ROT] for t in range(LANES // ROT)]
                partner = jnp.concatenate(
                    [slabs[t ^ 1] for t in range(LANES // ROT)], axis=0)
                y = y * cost_ref[:, cols] + partner * sint_ref[:, cols]
            ref[r, c * LANES:(c + 1) * LANES, :] = y.astype(BF16)

    kn_a, kn_b = gain_ref[0:1, :], gain_ref[1:2, :]
    for c in range(4):
        finish_q(qta_ref, c, c, gq_ref[:LANES, :], True)
        finish_q(qtb_ref, c, 6 + c, gq_ref[LANES:, :], False)
    ka, kb = head_norm_pair(chunk(4), kn_a, chunk(10), kn_b)
    kk_ref[:, :LANES] = rope(ka).astype(BF16)
    kk_ref[:, LANES:] = kb.astype(BF16)
    pad_rows = VT_ROWS - HEAD_DIM
    ones_row = (lax.broadcasted_iota(jnp.int32, (pad_rows, tm), 0) == 0).astype(F32)
    for ref, c in ((vta_ref, 5), (vtb_ref, 11)):
        vt = chunk(c).T
        ref[...] = jnp.concatenate(
            [vt[:HEAD_DIM], ones_row, vt[HEAD_DIM:], ones_row], axis=0).astype(BF16)
    h = _rms(x_ref[...], gn_ref[...]).astype(BF16)
    p_new[...] = jnp.dot(h, w_ref[...], preferred_element_type=F32)


def _qkv_call(x2, gn, w_qkv, gains, gq, cos_t, sin_t, seq, tm):
    n, d = x2.shape
    cos_tt, sin_tt = cos_t.T, sin_t.T
    nseq = seq // tm
    qd = N_HEADS * HEAD_DIM
    head_ones = jnp.kron(jnp.eye(2 * LANES // HEAD_DIM, dtype=F32),
                         jnp.ones((HEAD_DIM, HEAD_DIM), F32)).astype(BF16)
    n_tiles = n // tm
    tile_in = lambda i: jnp.minimum(i, n_tiles - 1)
    tile_out = lambda i: jnp.maximum(i - 1, 0)
    qt_spec = pl.BlockSpec((tm // ROW_BLOCK, qd, ROW_BLOCK), lambda i: (tile_out(i), 0, 0))
    qt_shape = jax.ShapeDtypeStruct((n // ROW_BLOCK, qd, ROW_BLOCK), BF16)
    return pl.pallas_call(
        _qkv_body,
        grid=(n_tiles + 1,),
        in_specs=[
            pl.BlockSpec((tm, d), lambda i: (tile_in(i), 0)),
            pl.BlockSpec((1, d), lambda i: (0, 0)),
            pl.BlockSpec(w_qkv.shape, lambda i: (0, 0)),
            pl.BlockSpec(gains.shape, lambda i: (0, 0)),
            pl.BlockSpec(gq.shape, lambda i: (0, 0)),
            pl.BlockSpec((tm, LANES), lambda i: (tile_out(i) % nseq, 0)),
            pl.BlockSpec((tm, LANES), lambda i: (tile_out(i) % nseq, 0)),
            pl.BlockSpec((LANES, tm), lambda i: (0, tile_out(i) % nseq)),
            pl.BlockSpec((LANES, tm), lambda i: (0, tile_out(i) % nseq)),
            pl.BlockSpec(head_ones.shape, lambda i: (0, 0)),
        ],
        out_specs=[qt_spec, qt_spec,
                   pl.BlockSpec((tm, 2 * LANES), lambda i: (tile_out(i), 0)),
                   pl.BlockSpec((N_KV * VT_ROWS, tm), lambda i: (0, tile_out(i))),
                   pl.BlockSpec((N_KV * VT_ROWS, tm), lambda i: (0, tile_out(i)))],
        out_shape=[qt_shape, qt_shape,
                   jax.ShapeDtypeStruct((n, 2 * LANES), BF16),
                   jax.ShapeDtypeStruct((N_KV * VT_ROWS, n), BF16),
                   jax.ShapeDtypeStruct((N_KV * VT_ROWS, n), BF16)],
        scratch_shapes=[pltpu.VMEM((tm, w_qkv.shape[1]), F32)] * 2,
        compiler_params=pltpu.CompilerParams(
            dimension_semantics=("arbitrary",), vmem_limit_bytes=VMEM_LIMIT),
        name="qkv_proj",
    )(x2, gn, w_qkv, gains, gq, cos_t, sin_t, cos_tt, sin_tt, head_ones)


def _query_rhs(q_ref, j, g):
    qt = jnp.concatenate(
        [q_ref[j, h * HEAD_DIM:(h + 1) * HEAD_DIM, :] for h in range(GROUP)], axis=1)
    zero = jnp.zeros_like(qt)
    return jnp.concatenate([jnp.where(g == 0, qt, zero), jnp.where(g == 0, zero, qt)], axis=0)


def _store_heads_t(o_ref, row0, o):
    rb = ROW_BLOCK
    o = jnp.concatenate([o[:, k * rb:(k + 1) * rb] for k in range(GROUP)], axis=0)
    o_ref[pl.ds(row0, rb), :] = o.T.astype(o_ref.dtype)


KEY_CHUNK = 256
GLOBAL_UNROLL = 4


def _global_attn_body(q_ref, k_ref, vt_ref, o_ref, s0, s1, m0, m1):
    s_buf, m_buf = (s0, s1), (m0, m1)
    g = pl.program_id(1)
    n_blocks = q_ref.shape[0]
    n_chunks = k_ref.shape[0] // KEY_CHUNK

    def keys(c):
        return slice(c * KEY_CHUNK, (c + 1) * KEY_CHUNK)

    def step(j_scores, j_finish, slot):
        m = None
        o = None
        if j_scores is not None:
            rhs = _query_rhs(q_ref, j_scores, g)
        for c in range(n_chunks):
            if j_scores is not None:
                s = jnp.dot(k_ref[keys(c), :], rhs, preferred_element_type=F32)
                s_buf[slot][keys(c), :] = s
                mc = jnp.max(s, axis=0, keepdims=True)
                m = mc if m is None else jnp.maximum(m, mc)
            if j_finish is not None:
                p = jnp.exp2(s_buf[1 - slot][keys(c), :] - m_buf[1 - slot][...]).astype(BF16)
                oc = jnp.dot(vt_ref[:, keys(c)], p, preferred_element_type=F32)
                o = oc if o is None else o + oc
        if j_scores is not None:
            m_buf[slot][...] = m
        if j_finish is not None:
            row0 = j_finish * ROW_BLOCK
            if not isinstance(row0, int):
                row0 = pl.multiple_of(row0, ROW_BLOCK)
            _store_heads_t(o_ref, row0, o[:HEAD_DIM] / o[HEAD_DIM:HEAD_DIM + 1])

    step(0, None, 0)
    unroll = GLOBAL_UNROLL
    n_loop = (n_blocks - 1) // unroll

    def steady(jj, carry):
        j = 1 + unroll * jj
        for u in range(unroll):
            step(j + u, j + u - 1, (1 + u) % 2)
        return carry

    lax.fori_loop(0, n_loop, steady, 0)
    for j in range(1 + unroll * n_loop, n_blocks):
        step(j, j - 1, j % 2)
    step(None, n_blocks - 1, n_blocks % 2)


def _global_attn_call(qt, kk, vt, batch, seq):
    n = kk.shape[0]
    nblk = seq // ROW_BLOCK
    cols = GROUP * ROW_BLOCK
    assert nblk % 2 == 0 and seq % KEY_CHUNK == 0
    return pl.pallas_call(
        _global_attn_body,
        grid=(batch, N_KV),
        in_specs=[
            pl.BlockSpec((nblk, GROUP * HEAD_DIM, ROW_BLOCK), lambda b, g: (b, g, 0)),
            pl.BlockSpec((seq, LANES), lambda b, g: (b, 0)),
            pl.BlockSpec((VT_ROWS, seq), lambda b, g: (g, b)),
        ],
        out_specs=pl.BlockSpec((seq, GROUP * HEAD_DIM), lambda b, g: (b, g)),
        out_shape=jax.ShapeDtypeStruct((n, N_HEADS * HEAD_DIM), BF16),
        scratch_shapes=(
            [pltpu.VMEM((seq, cols), F32)] * 2 + [pltpu.VMEM((1, cols), F32)] * 2),
        compiler_params=pltpu.CompilerParams(
            dimension_semantics=("arbitrary",) * 2, vmem_limit_bytes=VMEM_LIMIT),
        name="global_attn",
    )(qt, kk, vt)


def _window_attn_body(q_ref, k_ref, vt_ref, bias_ref, sink_ref, o_ref, s0, s1, m0, m1):
    w = WINDOW
    n_blocks = q_ref.shape[0]
    s_buf, m_buf = (s0, s1), (m0, m1)
    g = pl.program_id(1)
    sink = sink_ref[...]

    def band(j):
        lo, hi = max(j - 1, 0), min(j + 2, n_blocks)
        return slice(lo * w, hi * w), slice((lo - (j - 1)) * w, (hi - (j - 1)) * w)

    def step(j_scores, j_finish):
        if j_scores is not None:
            keys, rows = band(j_scores)
            n_keys = keys.stop - keys.start
            s = jnp.dot(k_ref[keys, :], _query_rhs(q_ref, j_scores, g),
                        preferred_element_type=F32) + bias_ref[rows, :]
            s_buf[j_scores % 2][:n_keys, :] = s
            m_buf[j_scores % 2][...] = jnp.maximum(jnp.max(s, axis=0, keepdims=True), sink)
        if j_finish is not None:
            keys, _ = band(j_finish)
            n_keys = keys.stop - keys.start
            m = m_buf[j_finish % 2][...]
            p = jnp.exp2(s_buf[j_finish % 2][:n_keys, :] - m).astype(BF16)
            o = jnp.dot(vt_ref[:, keys], p, preferred_element_type=F32)
            denom = o[HEAD_DIM:HEAD_DIM + 1] + jnp.exp2(sink - m)
            _store_heads_t(o_ref, j_finish * w, o[:HEAD_DIM] / denom)

    step(0, None)
    for j in range(1, n_blocks):
        step(j, j - 1)
    step(None, n_blocks - 1)


def _window_attn_call(qt, kk, vt, bias, sink, batch, seq):
    n = kk.shape[0]
    w = WINDOW
    nblk = seq // w
    cols = GROUP * w
    return pl.pallas_call(
        _window_attn_body,
        grid=(batch, N_KV),
        in_specs=[
            pl.BlockSpec((nblk, GROUP * HEAD_DIM, w), lambda b, g: (b, g, 0)),
            pl.BlockSpec((seq, LANES), lambda b, g: (b, 1)),
            pl.BlockSpec((VT_ROWS, seq), lambda b, g: (g, b)),
            pl.BlockSpec((None, 3 * w, cols), lambda b, g: (g, 0, 0)),
            pl.BlockSpec((None, 1, cols), lambda b, g: (g, 0, 0)),
        ],
        out_specs=pl.BlockSpec((seq, GROUP * HEAD_DIM), lambda b, g: (b, g)),
        out_shape=jax.ShapeDtypeStruct((n, N_HEADS * HEAD_DIM), BF16),
        scratch_shapes=(
            [pltpu.VMEM((3 * w, cols), F32)] * 2 + [pltpu.VMEM((1, cols), F32)] * 2),
        compiler_params=pltpu.CompilerParams(
            dimension_semantics=("arbitrary",) * 2, vmem_limit_bytes=VMEM_LIMIT),
        name="window_attn",
    )(qt, kk, vt, bias, sink)


def _mix_mlp_body(x_ref, oa_ref, ob_ref, gn_ref, wg_ref, bg_ref, woa_ref, wob_ref, wout_ref,
                  gm_ref, w1_ref, w2_ref, o_ref):
    d = x_ref.shape[1]
    x = x_ref[...]
    h = _rms(x, gn_ref[...]).astype(BF16)
    gates = jnp.dot(h, wg_ref[...], preferred_element_type=F32) + bg_ref[...]
    g = jax.nn.sigmoid(gates)
    ya = jnp.dot(oa_ref[...], woa_ref[...], preferred_element_type=F32)
    yb = jnp.dot(ob_ref[...], wob_ref[...], preferred_element_type=F32)
    mixed = g[:, :d] * ya + g[:, d:] * yb
    x = x + jnp.dot(mixed.astype(BF16), wout_ref[...], preferred_element_type=F32)
    h = _rms(x, gm_ref[...]).astype(BF16)
    u = jnp.square(jnp.maximum(jnp.dot(h, w1_ref[...], preferred_element_type=F32), 0.0))
    o_ref[...] = x + jnp.dot(u.astype(BF16), w2_ref[...], preferred_element_type=F32)


def _resident(arr):
    return pl.BlockSpec(arr.shape, lambda i: (0,) * arr.ndim, pipeline_mode=pl.Buffered(1))


def _mix_mlp_call(x2, oa, ob, gn, wg, bg, woa, wob, wout, gm, w1, w2, tm):
    n, d = x2.shape
    row = lambda width: pl.BlockSpec((tm, width), lambda i: (i, 0))
    consts = (gn, wg, bg, woa, wob, wout, gm, w1, w2)
    return pl.pallas_call(
        _mix_mlp_body,
        grid=(n // tm,),
        in_specs=[row(d), row(oa.shape[1]), row(ob.shape[1])] + [_resident(a) for a in consts],
        out_specs=row(d),
        out_shape=jax.ShapeDtypeStruct((n, d), F32),
        compiler_params=pltpu.CompilerParams(
            dimension_semantics=("arbitrary",), vmem_limit_bytes=VMEM_LIMIT),
        name="mix_mlp",
    )(x2, oa, ob, *consts)


def _rope_tables(seq):
    rows = seq // GRID_W
    row = jnp.repeat(jnp.arange(rows, dtype=jnp.int32), GRID_W)
    col = jnp.tile(jnp.arange(GRID_W, dtype=jnp.int32), rows)
    n_freq = HEAD_DIM // 4
    inv_freq = ROPE_THETA ** (-jnp.arange(n_freq, dtype=F32) / n_freq)
    ang_row = row.astype(F32)[:, None] * inv_freq[None, :]
    ang_col = col.astype(F32)[:, None] * inv_freq[None, :]
    cr, sr, cc, sc = jnp.cos(ang_row), jnp.sin(ang_row), jnp.cos(ang_col), jnp.sin(ang_col)
    cos64 = jnp.concatenate([cr, cr, cc, cc], axis=-1)
    sin64 = jnp.concatenate([-sr, sr, -sc, sc], axis=-1)
    return jnp.tile(cos64, (1, 2)), jnp.tile(sin64, (1, 2))


def _t5_bucket(rel):
    nb = N_BUCKETS // 2
    max_exact = nb // 2
    n = jnp.abs(rel)
    large = max_exact + (jnp.log(jnp.maximum(n, 1).astype(F32) / max_exact)
                         / math.log(MAX_DISTANCE / max_exact) * (nb - max_exact)).astype(jnp.int32)
    large = jnp.minimum(large, nb - 1)
    return jnp.where(rel > 0, nb, 0) + jnp.where(n < max_exact, n, large)


def _window_bias(rel_bias):
    w = WINDOW
    period = 4 * w
    slot = jnp.arange(period, dtype=jnp.int32)
    rel = jnp.where(slot < 3 * w, slot - w, slot - 5 * w)
    table = rel_bias.astype(F32)[_t5_bucket(rel)].T * LOG2E
    table = jnp.where(jnp.abs(rel)[None] <= w, table, NEG_INF)
    band = jnp.tile(table, (1, w))[:, :w * (period - 1)].reshape(-1, w, period - 1)
    band = band[:, :, :3 * w]
    return jnp.transpose(band.reshape(N_KV, GROUP * w, 3 * w), (0, 2, 1))


def kernel(x, w_in, b_gate, qn_a, kn_a, qn_b, kn_b, w_o_a, w_o_b, w_out,
           sink_b, rel_bias, norm_mix, norm_mlp, w_mlp1, w_mlp2):
    batch, seq, d = x.shape
    depth = w_in.shape[0]
    n_qkv = 2 * (N_HEADS + 2 * N_KV) * HEAD_DIM
    cos_t, sin_t = _rope_tables(seq)
    bias = _window_bias(rel_bias)
    x2 = x.reshape(batch * seq, d)
    pair = lambda v: jnp.tile(v, 2)
    for l in range(depth):
        gains = jnp.zeros((SUBLANES, LANES), F32).at[:2].set(
            jnp.stack([pair(kn_a[l]), pair(kn_b[l])]))
        gq = jnp.concatenate([
            jnp.broadcast_to((pair(qn) * Q_SCALE)[:, None], (LANES, LANES))
            for qn in (qn_a[l], qn_b[l])], axis=0)
        qt_a, qt_b, kk, vt_a, vt_b = _qkv_call(
            x2, norm_mix[l][None, :], w_in[l, :, :n_qkv].astype(BF16),
            gains, gq, cos_t, sin_t, seq, tm=QKV_TILE)
        oa = _global_attn_call(qt_a, kk, vt_a, batch, seq)
        sink = jnp.repeat(sink_b[l].astype(F32) * LOG2E, WINDOW).reshape(
            N_KV, 1, GROUP * WINDOW)
        ob = _window_attn_call(qt_b, kk, vt_b, bias, sink, batch, seq)
        x2 = _mix_mlp_call(
            x2, oa, ob, norm_mix[l][None, :], w_in[l, :, n_qkv:].astype(BF16),
            b_gate[l][None, :], w_o_a[l].astype(BF16), w_o_b[l].astype(BF16),
            w_out[l].astype(BF16), norm_mlp[l][None, :], w_mlp1[l].astype(BF16),
            w_mlp2[l].astype(BF16), tm=MLP_TILE)
    return x2.reshape(batch, seq, d)
```

```python
import math
from functools import partial

import jax
import jax.numpy as jnp
from jax import lax
from jax.experimental import pallas as pl
from jax.experimental.pallas import tpu as pltpu

F32 = jnp.float32
BF16 = jnp.bfloat16

HEAD_DIM = 64
N_HEADS = 8
N_KV = 2
GROUP = N_HEADS // N_KV
GRID_W = 64
ROPE_THETA = 10000.0
WINDOW = 128
N_BUCKETS = 32
MAX_DISTANCE = 128
EPS = 1e-6
NEG_INF = -1e30

LANES = 128
SUBLANES = 8
VMEM_LIMIT = 56 * 1024 * 1024
ROT = HEAD_DIM // 4

QKV_TILE = 512
MLP_TILE = 512

ROW_BLOCK = 128
VT_ROWS = 80
LOG2E = math.log2(math.e)
Q_SCALE = HEAD_DIM ** -0.5 * LOG2E


def _rms(x, gain):
    ms = jnp.mean(x * x, axis=-1, keepdims=True)
    return x * lax.rsqrt(ms + EPS) * gain


def _qkv_body(*refs):
    *io_refs, p0, p1 = refs
    i = pl.program_id(0)

    @pl.when(i == 0)
    def _():
        p1[...] = jnp.zeros_like(p1)

    @pl.when(i % 2 == 0)
    def _():
        _qkv_step(*io_refs, p0, p1)

    @pl.when(i % 2 == 1)
    def _():
        _qkv_step(*io_refs, p1, p0)


def _qkv_step(x_ref, gn_ref, w_ref, gain_ref, gq_ref, cos_ref, sin_ref, cost_ref, sint_ref,
              ones_ref, qta_ref, qtb_ref, kk_ref, vta_ref, vtb_ref, p_new, p_ref):
    tm = x_ref.shape[0]
    lane = lax.broadcasted_iota(jnp.int32, (tm, LANES), 1)
    pair_first = (lane & ROT) == 0
    cos = cos_ref[...]
    sin = sin_ref[...]

    def chunk(c):
        return p_ref[:, c * LANES:(c + 1) * LANES]

    def head_norm_pair(ca, gain_a, cb, gain_b):
        c2 = jnp.concatenate([ca, cb], axis=1)
        sq = c2 * c2
        hi = sq.astype(BF16)
        lo = (sq - hi.astype(F32)).astype(BF16)
        ss = (jnp.dot(hi, ones_ref[...], preferred_element_type=F32)
              + jnp.dot(lo, ones_ref[...], preferred_element_type=F32))
        y = c2 * lax.rsqrt(ss * (1.0 / HEAD_DIM) + EPS)
        return y[:, :LANES] * gain_a, y[:, LANES:] * gain_b

    def rope(c):
        ahead = pltpu.roll(c, LANES - ROT, 1)
        behind = pltpu.roll(c, ROT, 1)
        return c * cos + jnp.where(pair_first, ahead, behind) * sin

    def finish_q(ref, c, pc, gain_t, rotary):
        for r in range(tm // ROW_BLOCK):
            cols = slice(r * ROW_BLOCK, (r + 1) * ROW_BLOCK)
            bt = p_ref[cols, pc * LANES:(pc + 1) * LANES].T
            halves = []
            for hh in range(LANES // HEAD_DIM):
                xh = bt[hh * HEAD_DIM:(hh + 1) * HEAD_DIM]
                ms = jnp.sum(xh * xh, axis=0, keepdims=True) * (1.0 / HEAD_DIM)
                halves.append(xh * lax.rsqrt(ms + EPS))
            y = jnp.concatenate(halves, axis=0) * gain_t
            if rotary:
                slabs = [y[t * ROT:(t + 1) * ROT] for t in range(LANES // ROT)]
                partner = jnp.concatenate(
                    [slabs[t ^ 1] for t in range(LANES // ROT)], axis=0)
                y = y * cost_ref[:, cols] + partner * sint_ref[:, cols]
            ref[r, c * LANES:(c + 1) * LANES, :] = y.astype(BF16)

    kn_a, kn_b = gain_ref[0:1, :], gain_ref[1:2, :]
    for c in range(4):
        finish_q(qta_ref, c, c, gq_ref[:LANES, :], True)
        finish_q(qtb_ref, c, 6 + c, gq_ref[LANES:, :], False)
    ka, kb = head_norm_pair(chunk(4), kn_a, chunk(10), kn_b)
    kk_ref[:, :LANES] = rope(ka).astype(BF16)
    kk_ref[:, LANES:] = kb.astype(BF16)
    pad_rows = VT_ROWS - HEAD_DIM
    ones_row = (lax.broadcasted_iota(jnp.int32, (pad_rows, tm), 0) == 0).astype(F32)
    for ref, c in ((vta_ref, 5), (vtb_ref, 11)):
        vt = chunk(c).T
        ref[...] = jnp.concatenate(
            [vt[:HEAD_DIM], ones_row, vt[HEAD_DIM:], ones_row], axis=0).astype(BF16)
    h = _rms(x_ref[...], gn_ref[...]).astype(BF16)
    p_new[...] = jnp.dot(h, w_ref[...], preferred_element_type=F32)


def _qkv_call(x2, gn, w_qkv, gains, gq, cos_t, sin_t, seq, tm):
    n, d = x2.shape
    cos_tt, sin_tt = cos_t.T, sin_t.T
    nseq = seq // tm
    qd = N_HEADS * HEAD_DIM
    head_ones = jnp.kron(jnp.eye(2 * LANES // HEAD_DIM, dtype=F32),
                         jnp.ones((HEAD_DIM, HEAD_DIM), F32)).astype(BF16)
    n_tiles = n // tm
    tile_in = lambda i: jnp.minimum(i, n_tiles - 1)
    tile_out = lambda i: jnp.maximum(i - 1, 0)
    qt_spec = pl.BlockSpec((tm // ROW_BLOCK, qd, ROW_BLOCK), lambda i: (tile_out(i), 0, 0))
    qt_shape = jax.ShapeDtypeStruct((n // ROW_BLOCK, qd, ROW_BLOCK), BF16)
    return pl.pallas_call(
        _qkv_body,
        grid=(n_tiles + 1,),
        in_specs=[
            pl.BlockSpec((tm, d), lambda i: (tile_in(i), 0)),
            pl.BlockSpec((1, d), lambda i: (0, 0)),
            pl.BlockSpec(w_qkv.shape, lambda i: (0, 0)),
            pl.BlockSpec(gains.shape, lambda i: (0, 0)),
            pl.BlockSpec(gq.shape, lambda i: (0, 0)),
            pl.BlockSpec((tm, LANES), lambda i: (tile_out(i) % nseq, 0)),
            pl.BlockSpec((tm, LANES), lambda i: (tile_out(i) % nseq, 0)),
            pl.BlockSpec((LANES, tm), lambda i: (0, tile_out(i) % nseq)),
            pl.BlockSpec((LANES, tm), lambda i: (0, tile_out(i) % nseq)),
            pl.BlockSpec(head_ones.shape, lambda i: (0, 0)),
        ],
        out_specs=[qt_spec, qt_spec,
                   pl.BlockSpec((tm, 2 * LANES), lambda i: (tile_out(i), 0)),
                   pl.BlockSpec((N_KV * VT_ROWS, tm), lambda i: (0, tile_out(i))),
                   pl.BlockSpec((N_KV * VT_ROWS, tm), lambda i: (0, tile_out(i)))],
        out_shape=[qt_shape, qt_shape,
                   jax.ShapeDtypeStruct((n, 2 * LANES), BF16),
                   jax.ShapeDtypeStruct((N_KV * VT_ROWS, n), BF16),
                   jax.ShapeDtypeStruct((N_KV * VT_ROWS, n), BF16)],
        scratch_shapes=[pltpu.VMEM((tm, w_qkv.shape[1]), F32)] * 2,
        compiler_params=pltpu.CompilerParams(
            dimension_semantics=("arbitrary",), vmem_limit_bytes=VMEM_LIMIT),
        name="qkv_proj",
    )(x2, gn, w_qkv, gains, gq, cos_t, sin_t, cos_tt, sin_tt, head_ones)


def _query_rhs(q_ref, j, g):
    qt = jnp.concatenate(
        [q_ref[j, h * HEAD_DIM:(h + 1) * HEAD_DIM, :] for h in range(GROUP)], axis=1)
    zero = jnp.zeros_like(qt)
    return jnp.concatenate([jnp.where(g == 0, qt, zero), jnp.where(g == 0, zero, qt)], axis=0)


def _store_heads_t(o_ref, row0, o):
    rb = ROW_BLOCK
    o = jnp.concatenate([o[:, k * rb:(k + 1) * rb] for k in range(GROUP)], axis=0)
    o_ref[pl.ds(row0, rb), :] = o.T.astype(o_ref.dtype)


KEY_CHUNK = 256
GLOBAL_UNROLL = 4


def _global_attn_body(q_ref, k_ref, vt_ref, o_ref, s0, s1, m0, m1):
    s_buf, m_buf = (s0, s1), (m0, m1)
    g = pl.program_id(1)
    n_blocks = q_ref.shape[0]
    n_chunks = k_ref.shape[0] // KEY_CHUNK

    def keys(c):
        return slice(c * KEY_CHUNK, (c + 1) * KEY_CHUNK)

    def step(j_scores, j_finish, slot):
        m = None
        o = None
        if j_scores is not None:
            rhs = _query_rhs(q_ref, j_scores, g)
        for c in range(n_chunks):
            if j_scores is not None:
                s = jnp.dot(k_ref[keys(c), :], rhs, preferred_element_type=F32)
                s_buf[slot][keys(c), :] = s
                mc = jnp.max(s, axis=0, keepdims=True)
                m = mc if m is None else jnp.maximum(m, mc)
            if j_finish is not None:
                p = jnp.exp2(s_buf[1 - slot][keys(c), :] - m_buf[1 - slot][...]).astype(BF16)
                oc = jnp.dot(vt_ref[:, keys(c)], p, preferred_element_type=F32)
                o = oc if o is None else o + oc
        if j_scores is not None:
            m_buf[slot][...] = m
        if j_finish is not None:
            row0 = j_finish * ROW_BLOCK
            if not isinstance(row0, int):
                row0 = pl.multiple_of(row0, ROW_BLOCK)
            _store_heads_t(o_ref, row0, o[:HEAD_DIM] / o[HEAD_DIM:HEAD_DIM + 1])

    step(0, None, 0)
    unroll = GLOBAL_UNROLL
    n_loop = (n_blocks - 1) // unroll

    def steady(jj, carry):
        j = 1 + unroll * jj
        for u in range(unroll):
            step(j + u, j + u - 1, (1 + u) % 2)
        return carry

    lax.fori_loop(0, n_loop, steady, 0)
    for j in range(1 + unroll * n_loop, n_blocks):
        step(j, j - 1, j % 2)
    step(None, n_blocks - 1, n_blocks % 2)


def _global_attn_call(qt, kk, vt, batch, seq):
    n = kk.shape[0]
    nblk = seq // ROW_BLOCK
    cols = GROUP * ROW_BLOCK
    assert nblk % 2 == 0 and seq % KEY_CHUNK == 0
    return pl.pallas_call(
        _global_attn_body,
        grid=(batch, N_KV),
        in_specs=[
            pl.BlockSpec((nblk, GROUP * HEAD_DIM, ROW_BLOCK), lambda b, g: (b, g, 0)),
            pl.BlockSpec((seq, LANES), lambda b, g: (b, 0)),
            pl.BlockSpec((VT_ROWS, seq), lambda b, g: (g, b)),
        ],
        out_specs=pl.BlockSpec((seq, GROUP * HEAD_DIM), lambda b, g: (b, g)),
        out_shape=jax.ShapeDtypeStruct((n, N_HEADS * HEAD_DIM), BF16),
        scratch_shapes=(
            [pltpu.VMEM((seq, cols), F32)] * 2 + [pltpu.VMEM((1, cols), F32)] * 2),
        compiler_params=pltpu.CompilerParams(
            dimension_semantics=("arbitrary",) * 2, vmem_limit_bytes=VMEM_LIMIT),
        name="global_attn",
    )(qt, kk, vt)


def _window_attn_body(q_ref, k_ref, vt_ref, bias_ref, sink_ref, o_ref, s0, s1, m0, m1):
    w = WINDOW
    n_blocks = q_ref.shape[0]
    s_buf, m_buf = (s0, s1), (m0, m1)
    g = pl.program_id(1)
    sink = sink_ref[...]

    def band(j):
        lo, hi = max(j - 1, 0), min(j + 2, n_blocks)
        return slice(lo * w, hi * w), slice((lo - (j - 1)) * w, (hi - (j - 1)) * w)

    def step(j_scores, j_finish):
        if j_scores is not None:
            keys, rows = band(j_scores)
            n_keys = keys.stop - keys.start
            s = jnp.dot(k_ref[keys, :], _query_rhs(q_ref, j_scores, g),
                        preferred_element_type=F32) + bias_ref[rows, :]
            s_buf[j_scores % 2][:n_keys, :] = s
            m_buf[j_scores % 2][...] = jnp.maximum(jnp.max(s, axis=0, keepdims=True), sink)
        if j_finish is not None:
            keys, _ = band(j_finish)
            n_keys = keys.stop - keys.start
            m = m_buf[j_finish % 2][...]
            p = jnp.exp2(s_buf[j_finish % 2][:n_keys, :] - m).astype(BF16)
            o = jnp.dot(vt_ref[:, keys], p, preferred_element_type=F32)
            denom = o[HEAD_DIM:HEAD_DIM + 1] + jnp.exp2(sink - m)
            _store_heads_t(o_ref, j_finish * w, o[:HEAD_DIM] / denom)

    step(0, None)
    for j in range(1, n_blocks):
        step(j, j - 1)
    step(None, n_blocks - 1)


def _window_attn_call(qt, kk, vt, bias, sink, batch, seq):
    n = kk.shape[0]
    w = WINDOW
    nblk = seq // w
    cols = GROUP * w
    return pl.pallas_call(
        _window_attn_body,
        grid=(batch, N_KV),
        in_specs=[
            pl.BlockSpec((nblk, GROUP * HEAD_DIM, w), lambda b, g: (b, g, 0)),
            pl.BlockSpec((seq, LANES), lambda b, g: (b, 1)),
            pl.BlockSpec((VT_ROWS, seq), lambda b, g: (g, b)),
            pl.BlockSpec((None, 3 * w, cols), lambda b, g: (g, 0, 0)),
            pl.BlockSpec((None, 1, cols), lambda b, g: (g, 0, 0)),
        ],
        out_specs=pl.BlockSpec((seq, GROUP * HEAD_DIM), lambda b, g: (b, g)),
        out_shape=jax.ShapeDtypeStruct((n, N_HEADS * HEAD_DIM), BF16),
        scratch_shapes=(
            [pltpu.VMEM((3 * w, cols), F32)] * 2 + [pltpu.VMEM((1, cols), F32)] * 2),
        compiler_params=pltpu.CompilerParams(
            dimension_semantics=("arbitrary",) * 2, vmem_limit_bytes=VMEM_LIMIT),
        name="window_attn",
    )(qt, kk, vt, bias, sink)


STAGE_ROWS, STAGE_COLS = 256, 1024


def _stage_weights(jobs, stage, sem):
    chunks = []
    for src, row0, col0, dst in jobs:
        for r in range(0, dst.shape[0], STAGE_ROWS):
            for c in range(0, dst.shape[1], STAGE_COLS):
                chunks.append((src.at[pl.ds(row0 + r, STAGE_ROWS), pl.ds(col0 + c, STAGE_COLS)],
                               dst.at[pl.ds(r, STAGE_ROWS), pl.ds(c, STAGE_COLS)]))

    def copy(k):
        return pltpu.make_async_copy(chunks[k][0], stage.at[k % 2], sem.at[k % 2])

    copy(0).start()
    for k in range(len(chunks)):
        if k + 1 < len(chunks):
            copy(k + 1).start()
        copy(k).wait()
        chunks[k][1][...] = stage[k % 2].astype(BF16)


def _mix_mlp_body(x_ref, oa_ref, ob_ref, gn_ref, bg_ref, gm_ref,
                  win_hbm, woa_hbm, wob_hbm, wout_hbm, w1_hbm, w2_hbm, o_ref,
                  wg_ref, woa_ref, wob_ref, wout_ref, w1_ref, w2_ref, stage, sem, *, layer, col0):
    @pl.when(pl.program_id(0) == 0)
    def _():
        _stage_weights(
            [(win_hbm.at[layer], 0, col0, wg_ref), (woa_hbm.at[layer], 0, 0, woa_ref),
             (wob_hbm.at[layer], 0, 0, wob_ref), (wout_hbm.at[layer], 0, 0, wout_ref),
             (w1_hbm.at[layer], 0, 0, w1_ref), (w2_hbm.at[layer], 0, 0, w2_ref)],
            stage, sem)

    d = x_ref.shape[1]
    x = x_ref[...]
    h = _rms(x, gn_ref[...]).astype(BF16)
    gates = jnp.dot(h, wg_ref[...], preferred_element_type=F32) + bg_ref[...]
    g = jax.nn.sigmoid(gates)
    ya = jnp.dot(oa_ref[...], woa_ref[...], preferred_element_type=F32)
    yb = jnp.dot(ob_ref[...], wob_ref[...], preferred_element_type=F32)
    mixed = g[:, :d] * ya + g[:, d:] * yb
    x = x + jnp.dot(mixed.astype(BF16), wout_ref[...], preferred_element_type=F32)
    h = _rms(x, gm_ref[...]).astype(BF16)
    u = jnp.square(jnp.maximum(jnp.dot(h, w1_ref[...], preferred_element_type=F32), 0.0))
    o_ref[...] = x + jnp.dot(u.astype(BF16), w2_ref[...], preferred_element_type=F32)


def _resident(arr):
    return pl.BlockSpec(arr.shape, lambda i: (0,) * arr.ndim, pipeline_mode=pl.Buffered(1))


def _mix_mlp_call(x2, oa, ob, gn, bg, gm, w_in, w_o_a, w_o_b, w_out, w_mlp1, w_mlp2,
                  layer, gate_col0, tm):
    n, d = x2.shape
    row = lambda width: pl.BlockSpec((tm, width), lambda i: (i, 0))
    consts = (gn, bg, gm)
    weights = (w_in, w_o_a, w_o_b, w_out, w_mlp1, w_mlp2)
    n_gate = w_in.shape[2] - gate_col0
    bf16_shapes = [(d, n_gate)] + [w.shape[1:] for w in weights[1:]]
    for shape in bf16_shapes:
        assert shape[0] % STAGE_ROWS == 0 and shape[1] % STAGE_COLS == 0
    return pl.pallas_call(
        partial(_mix_mlp_body, layer=layer, col0=gate_col0),
        grid=(n // tm,),
        in_specs=([row(d), row(oa.shape[1]), row(ob.shape[1])] + [_resident(a) for a in consts]
                  + [pl.BlockSpec(memory_space=pl.ANY)] * len(weights)),
        out_specs=row(d),
        out_shape=jax.ShapeDtypeStruct((n, d), F32),
        scratch_shapes=([pltpu.VMEM(shape, BF16) for shape in bf16_shapes]
                        + [pltpu.VMEM((2, STAGE_ROWS, STAGE_COLS), F32),
                           pltpu.SemaphoreType.DMA((2,))]),
        compiler_params=pltpu.CompilerParams(
            dimension_semantics=("arbitrary",), vmem_limit_bytes=VMEM_LIMIT),
        name="mix_mlp",
    )(x2, oa, ob, *consts, *weights)


def _rope_tables(seq):
    rows = seq // GRID_W
    row = jnp.repeat(jnp.arange(rows, dtype=jnp.int32), GRID_W)
    col = jnp.tile(jnp.arange(GRID_W, dtype=jnp.int32), rows)
    n_freq = HEAD_DIM // 4
    inv_freq = ROPE_THETA ** (-jnp.arange(n_freq, dtype=F32) / n_freq)
    ang_row = row.astype(F32)[:, None] * inv_freq[None, :]
    ang_col = col.astype(F32)[:, None] * inv_freq[None, :]
    cr, sr, cc, sc = jnp.cos(ang_row), jnp.sin(ang_row), jnp.cos(ang_col), jnp.sin(ang_col)
    cos64 = jnp.concatenate([cr, cr, cc, cc], axis=-1)
    sin64 = jnp.concatenate([-sr, sr, -sc, sc], axis=-1)
    return jnp.tile(cos64, (1, 2)), jnp.tile(sin64, (1, 2))


def _t5_bucket(rel):
    nb = N_BUCKETS // 2
    max_exact = nb // 2
    n = jnp.abs(rel)
    large = max_exact + (jnp.log(jnp.maximum(n, 1).astype(F32) / max_exact)
                         / math.log(MAX_DISTANCE / max_exact) * (nb - max_exact)).astype(jnp.int32)
    large = jnp.minimum(large, nb - 1)
    return jnp.where(rel > 0, nb, 0) + jnp.where(n < max_exact, n, large)


def _window_bias(rel_bias):
    w = WINDOW
    period = 4 * w
    slot = jnp.arange(period, dtype=jnp.int32)
    rel = jnp.where(slot < 3 * w, slot - w, slot - 5 * w)
    table = rel_bias.astype(F32)[_t5_bucket(rel)].T * LOG2E
    table = jnp.where(jnp.abs(rel)[None] <= w, table, NEG_INF)
    band = jnp.tile(table, (1, w))[:, :w * (period - 1)].reshape(-1, w, period - 1)
    band = band[:, :, :3 * w]
    return jnp.transpose(band.reshape(N_KV, GROUP * w, 3 * w), (0, 2, 1))


def kernel(x, w_in, b_gate, qn_a, kn_a, qn_b, kn_b, w_o_a, w_o_b, w_out,
           sink_b, rel_bias, norm_mix, norm_mlp, w_mlp1, w_mlp2):
    batch, seq, d = x.shape
    depth = w_in.shape[0]
    n_qkv = 2 * (N_HEADS + 2 * N_KV) * HEAD_DIM
    cos_t, sin_t = _rope_tables(seq)
    bias = _window_bias(rel_bias)
    x2 = x.reshape(batch * seq, d)
    pair = lambda v: jnp.tile(v, 2)
    for l in range(depth):
        gains = jnp.zeros((SUBLANES, LANES), F32).at[:2].set(
            jnp.stack([pair(kn_a[l]), pair(kn_b[l])]))
        gq = jnp.concatenate([
            jnp.broadcast_to((pair(qn) * Q_SCALE)[:, None], (LANES, LANES))
            for qn in (qn_a[l], qn_b[l])], axis=0)
        qt_a, qt_b, kk, vt_a, vt_b = _qkv_call(
            x2, norm_mix[l][None, :], w_in[l, :, :n_qkv].astype(BF16),
            gains, gq, cos_t, sin_t, seq, tm=QKV_TILE)
        oa = _global_attn_call(qt_a, kk, vt_a, batch, seq)
        sink = jnp.repeat(sink_b[l].astype(F32) * LOG2E, WINDOW).reshape(
            N_KV, 1, GROUP * WINDOW)
        ob = _window_attn_call(qt_b, kk, vt_b, bias, sink, batch, seq)
        x2 = _mix_mlp_call(
            x2, oa, ob, norm_mix[l][None, :], b_gate[l][None, :], norm_mlp[l][None, :],
            w_in, w_o_a, w_o_b, w_out, w_mlp1, w_mlp2,
            layer=l, gate_col0=n_qkv, tm=MLP_TILE)
    return x2.reshape(batch, seq, d)
```

```python
import math
from functools import partial

import jax
import jax.numpy as jnp
from jax import lax
from jax.experimental import pallas as pl
from jax.experimental.pallas import tpu as pltpu

F32 = jnp.float32
BF16 = jnp.bfloat16

HEAD_DIM = 64
N_HEADS = 8
N_KV = 2
GROUP = N_HEADS // N_KV
GRID_W = 64
ROPE_THETA = 10000.0
WINDOW = 128
N_BUCKETS = 32
MAX_DISTANCE = 128
EPS = 1e-6
NEG_INF = -1e30

LANES = 128
SUBLANES = 8
VMEM_LIMIT = 56 * 1024 * 1024
ROT = HEAD_DIM // 4

QKV_TILE = 512
MLP_TILE = 512

ROW_BLOCK = 128
VT_ROWS = 80
LOG2E = math.log2(math.e)
Q_SCALE = HEAD_DIM ** -0.5 * LOG2E


def _rms(x, gain):
    ms = jnp.mean(x * x, axis=-1, keepdims=True)
    return x * lax.rsqrt(ms + EPS) * gain


def _qkv_body(*refs):
    *io_refs, p0, p1 = refs
    i = pl.program_id(0)

    @pl.when(i == 0)
    def _():
        p1[...] = jnp.zeros_like(p1)

    @pl.when(i % 2 == 0)
    def _():
        _qkv_step(*io_refs, p0, p1)

    @pl.when(i % 2 == 1)
    def _():
        _qkv_step(*io_refs, p1, p0)


def _qkv_step(x_ref, gn_ref, w_ref, gain_ref, gq_ref, cos_ref, sin_ref, cost_ref, sint_ref,
              ones_ref, qta_ref, qtb_ref, kk_ref, vta_ref, vtb_ref, p_new, p_ref):
    tm = x_ref.shape[0]
    lane = lax.broadcasted_iota(jnp.int32, (tm, LANES), 1)
    pair_first = (lane & ROT) == 0
    cos = cos_ref[...]
    sin = sin_ref[...]

    def chunk(c):
        return p_ref[:, c * LANES:(c + 1) * LANES]

    def head_norm_pair(ca, gain_a, cb, gain_b):
        c2 = jnp.concatenate([ca, cb], axis=1)
        sq = c2 * c2
        hi = sq.astype(BF16)
        lo = (sq - hi.astype(F32)).astype(BF16)
        ss = (jnp.dot(hi, ones_ref[...], preferred_element_type=F32)
              + jnp.dot(lo, ones_ref[...], preferred_element_type=F32))
        y = c2 * lax.rsqrt(ss * (1.0 / HEAD_DIM) + EPS)
        return y[:, :LANES] * gain_a, y[:, LANES:] * gain_b

    def rope(c):
        ahead = pltpu.roll(c, LANES - ROT, 1)
        behind = pltpu.roll(c, ROT, 1)
        return c * cos + jnp.where(pair_first, ahead, behind) * sin

    def finish_q(ref, c, pc, gain_t, rotary):
        for r in range(tm // ROW_BLOCK):
            cols = slice(r * ROW_BLOCK, (r + 1) * ROW_BLOCK)
            bt = p_ref[cols, pc * LANES:(pc + 1) * LANES].T
            halves = []
            for hh in range(LANES // HEAD_DIM):
                xh = bt[hh * HEAD_DIM:(hh + 1) * HEAD_DIM]
                ms = jnp.sum(xh * xh, axis=0, keepdims=True) * (1.0 / HEAD_DIM)
                halves.append(xh * lax.rsqrt(ms + EPS))
            y = jnp.concatenate(halves, axis=0) * gain_t
            if rotary:
                slabs = [y[t * ROT:(t + 1) * ROT] for t in range(LANES // ROT)]
                partner = jnp.concatenate(
                    [slabs[t ^ 1] for t in range(LANES // ROT)], axis=0)
                y = y * cost_ref[:, cols] + partner * sint_ref[:, cols]
            ref[r, c * LANES:(c + 1) * LANES, :] = y.astype(BF16)

    kn_a, kn_b = gain_ref[0:1, :], gain_ref[1:2, :]
    for c in range(4):
        finish_q(qta_ref, c, c, gq_ref[:LANES, :], True)
        finish_q(qtb_ref, c, 6 + c, gq_ref[LANES:, :], False)
    ka, kb = head_norm_pair(chunk(4), kn_a, chunk(10), kn_b)
    kk_ref[:, :LANES] = rope(ka).astype(BF16)
    kk_ref[:, LANES:] = kb.astype(BF16)
    pad_rows = VT_ROWS - HEAD_DIM
    ones_row = (lax.broadcasted_iota(jnp.int32, (pad_rows, tm), 0) == 0).astype(F32)
    for ref, c in ((vta_ref, 5), (vtb_ref, 11)):
        vt = chunk(c).T
        ref[...] = jnp.concatenate(
            [vt[:HEAD_DIM], ones_row, vt[HEAD_DIM:], ones_row], axis=0).astype(BF16)
    h = _rms(x_ref[...], gn_ref[...]).astype(BF16)
    p_new[...] = jnp.dot(h, w_ref[...], preferred_element_type=F32)


def _qkv_call(x2, gn, w_qkv, gains, gq, cos_t, sin_t, seq, tm):
    n, d = x2.shape
    cos_tt, sin_tt = cos_t.T, sin_t.T
    nseq = seq // tm
    qd = N_HEADS * HEAD_DIM
    head_ones = jnp.kron(jnp.eye(2 * LANES // HEAD_DIM, dtype=F32),
                         jnp.ones((HEAD_DIM, HEAD_DIM), F32)).astype(BF16)
    n_tiles = n // tm
    tile_in = lambda i: jnp.minimum(i, n_tiles - 1)
    tile_out = lambda i: jnp.maximum(i - 1, 0)
    qt_spec = pl.BlockSpec((tm // ROW_BLOCK, qd, ROW_BLOCK), lambda i: (tile_out(i), 0, 0))
    qt_shape = jax.ShapeDtypeStruct((n // ROW_BLOCK, qd, ROW_BLOCK), BF16)
    return pl.pallas_call(
        _qkv_body,
        grid=(n_tiles + 1,),
        in_specs=[
            pl.BlockSpec((tm, d), lambda i: (tile_in(i), 0)),
            pl.BlockSpec((1, d), lambda i: (0, 0)),
            pl.BlockSpec(w_qkv.shape, lambda i: (0, 0)),
            pl.BlockSpec(gains.shape, lambda i: (0, 0)),
            pl.BlockSpec(gq.shape, lambda i: (0, 0)),
            pl.BlockSpec((tm, LANES), lambda i: (tile_out(i) % nseq, 0)),
            pl.BlockSpec((tm, LANES), lambda i: (tile_out(i) % nseq, 0)),
            pl.BlockSpec((LANES, tm), lambda i: (0, tile_out(i) % nseq)),
            pl.BlockSpec((LANES, tm), lambda i: (0, tile_out(i) % nseq)),
            pl.BlockSpec(head_ones.shape, lambda i: (0, 0)),
        ],
        out_specs=[qt_spec, qt_spec,
                   pl.BlockSpec((tm, 2 * LANES), lambda i: (tile_out(i), 0)),
                   pl.BlockSpec((N_KV * VT_ROWS, tm), lambda i: (0, tile_out(i))),
                   pl.BlockSpec((N_KV * VT_ROWS, tm), lambda i: (0, tile_out(i)))],
        out_shape=[qt_shape, qt_shape,
                   jax.ShapeDtypeStruct((n, 2 * LANES), BF16),
                   jax.ShapeDtypeStruct((N_KV * VT_ROWS, n), BF16),
                   jax.ShapeDtypeStruct((N_KV * VT_ROWS, n), BF16)],
        scratch_shapes=[pltpu.VMEM((tm, w_qkv.shape[1]), F32)] * 2,
        compiler_params=pltpu.CompilerParams(
            dimension_semantics=("arbitrary",), vmem_limit_bytes=VMEM_LIMIT),
        name="qkv_proj",
    )(x2, gn, w_qkv, gains, gq, cos_t, sin_t, cos_tt, sin_tt, head_ones)


def _query_rhs(q_ref, j, g):
    qt = jnp.concatenate(
        [q_ref[j, h * HEAD_DIM:(h + 1) * HEAD_DIM, :] for h in range(GROUP)], axis=1)
    zero = jnp.zeros_like(qt)
    return jnp.concatenate([jnp.where(g == 0, qt, zero), jnp.where(g == 0, zero, qt)], axis=0)


def _store_heads_t(o_ref, row0, o):
    rb = ROW_BLOCK
    o = jnp.concatenate([o[:, k * rb:(k + 1) * rb] for k in range(GROUP)], axis=0)
    o_ref[pl.ds(row0, rb), :] = o.T.astype(o_ref.dtype)


KEY_CHUNK = 256
GLOBAL_UNROLL = 4


def _global_attn_body(q_ref, k_ref, vt_ref, o_ref, s0, s1, m0, m1):
    s_buf, m_buf = (s0, s1), (m0, m1)
    g = pl.program_id(1)
    n_blocks = q_ref.shape[0]
    n_chunks = k_ref.shape[0] // KEY_CHUNK

    def keys(c):
        return slice(c * KEY_CHUNK, (c + 1) * KEY_CHUNK)

    def step(j_scores, j_finish, slot):
        m = None
        o = None
        if j_scores is not None:
            rhs = _query_rhs(q_ref, j_scores, g)
        for c in range(n_chunks):
            if j_scores is not None:
                s = jnp.dot(k_ref[keys(c), :], rhs, preferred_element_type=F32)
                s_buf[slot][keys(c), :] = s
                mc = jnp.max(s, axis=0, keepdims=True)
                m = mc if m is None else jnp.maximum(m, mc)
            if j_finish is not None:
                p = jnp.exp2(s_buf[1 - slot][keys(c), :] - m_buf[1 - slot][...]).astype(BF16)
                oc = jnp.dot(vt_ref[:, keys(c)], p, preferred_element_type=F32)
                o = oc if o is None else o + oc
        if j_scores is not None:
            m_buf[slot][...] = m
        if j_finish is not None:
            row0 = j_finish * ROW_BLOCK
            if not isinstance(row0, int):
                row0 = pl.multiple_of(row0, ROW_BLOCK)
            _store_heads_t(o_ref, row0, o[:HEAD_DIM] / o[HEAD_DIM:HEAD_DIM + 1])

    step(0, None, 0)
    unroll = GLOBAL_UNROLL
    n_loop = (n_blocks - 1) // unroll

    def steady(jj, carry):
        j = 1 + unroll * jj
        for u in range(unroll):
            step(j + u, j + u - 1, (1 + u) % 2)
        return carry

    lax.fori_loop(0, n_loop, steady, 0)
    for j in range(1 + unroll * n_loop, n_blocks):
        step(j, j - 1, j % 2)
    step(None, n_blocks - 1, n_blocks % 2)


def _global_attn_call(qt, kk, vt, batch, seq):
    n = kk.shape[0]
    nblk = seq // ROW_BLOCK
    cols = GROUP * ROW_BLOCK
    assert nblk % 2 == 0 and seq % KEY_CHUNK == 0
    return pl.pallas_call(
        _global_attn_body,
        grid=(batch, N_KV),
        in_specs=[
            pl.BlockSpec((nblk, GROUP * HEAD_DIM, ROW_BLOCK), lambda b, g: (b, g, 0)),
            pl.BlockSpec((seq, LANES), lambda b, g: (b, 0)),
            pl.BlockSpec((VT_ROWS, seq), lambda b, g: (g, b)),
        ],
        out_specs=pl.BlockSpec((seq, GROUP * HEAD_DIM), lambda b, g: (b, g)),
        out_shape=jax.ShapeDtypeStruct((n, N_HEADS * HEAD_DIM), BF16),
        scratch_shapes=(
            [pltpu.VMEM((seq, cols), F32)] * 2 + [pltpu.VMEM((1, cols), F32)] * 2),
        compiler_params=pltpu.CompilerParams(
            dimension_semantics=("arbitrary",) * 2, vmem_limit_bytes=VMEM_LIMIT),
        name="global_attn",
    )(qt, kk, vt)


def _window_attn_body(q_ref, k_ref, vt_ref, bias_ref, sink_ref, o_ref, s0, s1, m0, m1):
    w = WINDOW
    n_blocks = q_ref.shape[0]
    s_buf, m_buf = (s0, s1), (m0, m1)
    g = pl.program_id(1)
    sink = sink_ref[...]

    def band(j):
        lo, hi = max(j - 1, 0), min(j + 2, n_blocks)
        return slice(lo * w, hi * w), slice((lo - (j - 1)) * w, (hi - (j - 1)) * w)

    def step(j_scores, j_finish):
        if j_scores is not None:
            keys, rows = band(j_scores)
            n_keys = keys.stop - keys.start
            s = jnp.dot(k_ref[keys, :], _query_rhs(q_ref, j_scores, g),
                        preferred_element_type=F32) + bias_ref[rows, :]
            s_buf[j_scores % 2][:n_keys, :] = s
            m_buf[j_scores % 2][...] = jnp.maximum(jnp.max(s, axis=0, keepdims=True), sink)
        if j_finish is not None:
            keys, _ = band(j_finish)
            n_keys = keys.stop - keys.start
            m = m_buf[j_finish % 2][...]
            p = jnp.exp2(s_buf[j_finish % 2][:n_keys, :] - m).astype(BF16)
            o = jnp.dot(vt_ref[:, keys], p, preferred_element_type=F32)
            denom = o[HEAD_DIM:HEAD_DIM + 1] + jnp.exp2(sink - m)
            _store_heads_t(o_ref, j_finish * w, o[:HEAD_DIM] / denom)

    step(0, None)
    for j in range(1, n_blocks):
        step(j, j - 1)
    step(None, n_blocks - 1)


def _window_attn_call(qt, kk, vt, bias, sink, batch, seq):
    n = kk.shape[0]
    w = WINDOW
    nblk = seq // w
    cols = GROUP * w
    return pl.pallas_call(
        _window_attn_body,
        grid=(batch, N_KV),
        in_specs=[
            pl.BlockSpec((nblk, GROUP * HEAD_DIM, w), lambda b, g: (b, g, 0)),
            pl.BlockSpec((seq, LANES), lambda b, g: (b, 1)),
            pl.BlockSpec((VT_ROWS, seq), lambda b, g: (g, b)),
            pl.BlockSpec((None, 3 * w, cols), lambda b, g: (g, 0, 0)),
            pl.BlockSpec((None, 1, cols), lambda b, g: (g, 0, 0)),
        ],
        out_specs=pl.BlockSpec((seq, GROUP * HEAD_DIM), lambda b, g: (b, g)),
        out_shape=jax.ShapeDtypeStruct((n, N_HEADS * HEAD_DIM), BF16),
        scratch_shapes=(
            [pltpu.VMEM((3 * w, cols), F32)] * 2 + [pltpu.VMEM((1, cols), F32)] * 2),
        compiler_params=pltpu.CompilerParams(
            dimension_semantics=("arbitrary",) * 2, vmem_limit_bytes=VMEM_LIMIT),
        name="window_attn",
    )(qt, kk, vt, bias, sink)


STAGE_ROWS, STAGE_COLS = 256, 1024
STAGE_SLOTS = 4


def _stage_weights(jobs, stage, sem):
    chunks = []
    for src, row0, col0, dst in jobs:
        for r in range(0, dst.shape[0], STAGE_ROWS):
            for c in range(0, dst.shape[1], STAGE_COLS):
                chunks.append((src.at[pl.ds(row0 + r, STAGE_ROWS), pl.ds(col0 + c, STAGE_COLS)],
                               dst.at[pl.ds(r, STAGE_ROWS), pl.ds(c, STAGE_COLS)]))

    n_slots = stage.shape[0]

    def copy(k):
        return pltpu.make_async_copy(chunks[k][0], stage.at[k % n_slots], sem.at[k % n_slots])

    for k in range(min(n_slots - 1, len(chunks))):
        copy(k).start()
    for k in range(len(chunks)):
        if k + n_slots - 1 < len(chunks):
            copy(k + n_slots - 1).start()
        copy(k).wait()
        chunks[k][1][...] = stage[k % n_slots].astype(BF16)


def _mix_mlp_body(x_ref, oa_ref, ob_ref, gn_ref, bg_ref, gm_ref,
                  win_hbm, woa_hbm, wob_hbm, wout_hbm, w1_hbm, w2_hbm, o_ref,
                  wg_ref, woa_ref, wob_ref, wout_ref, w1_ref, w2_ref, stage, sem, *, layer, col0):
    @pl.when(pl.program_id(0) == 0)
    def _():
        _stage_weights(
            [(win_hbm.at[layer], 0, col0, wg_ref), (woa_hbm.at[layer], 0, 0, woa_ref),
             (wob_hbm.at[layer], 0, 0, wob_ref), (wout_hbm.at[layer], 0, 0, wout_ref),
             (w1_hbm.at[layer], 0, 0, w1_ref), (w2_hbm.at[layer], 0, 0, w2_ref)],
            stage, sem)

    d = x_ref.shape[1]
    x = x_ref[...]
    h = _rms(x, gn_ref[...]).astype(BF16)
    gates = jnp.dot(h, wg_ref[...], preferred_element_type=F32) + bg_ref[...]
    g = jax.nn.sigmoid(gates)
    ya = jnp.dot(oa_ref[...], woa_ref[...], preferred_element_type=F32)
    yb = jnp.dot(ob_ref[...], wob_ref[...], preferred_element_type=F32)
    mixed = g[:, :d] * ya + g[:, d:] * yb
    x = x + jnp.dot(mixed.astype(BF16), wout_ref[...], preferred_element_type=F32)
    h = _rms(x, gm_ref[...]).astype(BF16)
    u = jnp.square(jnp.maximum(jnp.dot(h, w1_ref[...], preferred_element_type=F32), 0.0))
    o_ref[...] = x + jnp.dot(u.astype(BF16), w2_ref[...], preferred_element_type=F32)


def _resident(arr):
    return pl.BlockSpec(arr.shape, lambda i: (0,) * arr.ndim, pipeline_mode=pl.Buffered(1))


def _mix_mlp_call(x2, oa, ob, gn, bg, gm, w_in, w_o_a, w_o_b, w_out, w_mlp1, w_mlp2,
                  layer, gate_col0, tm):
    n, d = x2.shape
    row = lambda width: pl.BlockSpec((tm, width), lambda i: (i, 0))
    consts = (gn, bg, gm)
    weights = (w_in, w_o_a, w_o_b, w_out, w_mlp1, w_mlp2)
    n_gate = w_in.shape[2] - gate_col0
    bf16_shapes = [(d, n_gate)] + [w.shape[1:] for w in weights[1:]]
    for shape in bf16_shapes:
        assert shape[0] % STAGE_ROWS == 0 and shape[1] % STAGE_COLS == 0
    return pl.pallas_call(
        partial(_mix_mlp_body, layer=layer, col0=gate_col0),
        grid=(n // tm,),
        in_specs=([row(d), row(oa.shape[1]), row(ob.shape[1])] + [_resident(a) for a in consts]
                  + [pl.BlockSpec(memory_space=pl.ANY)] * len(weights)),
        out_specs=row(d),
        out_shape=jax.ShapeDtypeStruct((n, d), F32),
        scratch_shapes=([pltpu.VMEM(shape, BF16) for shape in bf16_shapes]
                        + [pltpu.VMEM((STAGE_SLOTS, STAGE_ROWS, STAGE_COLS), F32),
                           pltpu.SemaphoreType.DMA((STAGE_SLOTS,))]),
        compiler_params=pltpu.CompilerParams(
            dimension_semantics=("arbitrary",), vmem_limit_bytes=VMEM_LIMIT),
        name="mix_mlp",
    )(x2, oa, ob, *consts, *weights)


def _rope_tables(seq):
    rows = seq // GRID_W
    row = jnp.repeat(jnp.arange(rows, dtype=jnp.int32), GRID_W)
    col = jnp.tile(jnp.arange(GRID_W, dtype=jnp.int32), rows)
    n_freq = HEAD_DIM // 4
    inv_freq = ROPE_THETA ** (-jnp.arange(n_freq, dtype=F32) / n_freq)
    ang_row = row.astype(F32)[:, None] * inv_freq[None, :]
    ang_col = col.astype(F32)[:, None] * inv_freq[None, :]
    cr, sr, cc, sc = jnp.cos(ang_row), jnp.sin(ang_row), jnp.cos(ang_col), jnp.sin(ang_col)
    cos64 = jnp.concatenate([cr, cr, cc, cc], axis=-1)
    sin64 = jnp.concatenate([-sr, sr, -sc, sc], axis=-1)
    return jnp.tile(cos64, (1, 2)), jnp.tile(sin64, (1, 2))


def _t5_bucket(rel):
    nb = N_BUCKETS // 2
    max_exact = nb // 2
    n = jnp.abs(rel)
    large = max_exact + (jnp.log(jnp.maximum(n, 1).astype(F32) / max_exact)
                         / math.log(MAX_DISTANCE / max_exact) * (nb - max_exact)).astype(jnp.int32)
    large = jnp.minimum(large, nb - 1)
    return jnp.where(rel > 0, nb, 0) + jnp.where(n < max_exact, n, large)


def _window_bias(rel_bias):
    w = WINDOW
    period = 4 * w
    slot = jnp.arange(period, dtype=jnp.int32)
    rel = jnp.where(slot < 3 * w, slot - w, slot - 5 * w)
    table = rel_bias.astype(F32)[_t5_bucket(rel)].T * LOG2E
    table = jnp.where(jnp.abs(rel)[None] <= w, table, NEG_INF)
    band = jnp.tile(table, (1, w))[:, :w * (period - 1)].reshape(-1, w, period - 1)
    band = band[:, :, :3 * w]
    return jnp.transpose(band.reshape(N_KV, GROUP * w, 3 * w), (0, 2, 1))


def kernel(x, w_in, b_gate, qn_a, kn_a, qn_b, kn_b, w_o_a, w_o_b, w_out,
           sink_b, rel_bias, norm_mix, norm_mlp, w_mlp1, w_mlp2):
    batch, seq, d = x.shape
    depth = w_in.shape[0]
    n_qkv = 2 * (N_HEADS + 2 * N_KV) * HEAD_DIM
    cos_t, sin_t = _rope_tables(seq)
    bias = _window_bias(rel_bias)
    x2 = x.reshape(batch * seq, d)
    pair = lambda v: jnp.tile(v, 2)
    for l in range(depth):
        gains = jnp.zeros((SUBLANES, LANES), F32).at[:2].set(
            jnp.stack([pair(kn_a[l]), pair(kn_b[l])]))
        gq = jnp.concatenate([
            jnp.broadcast_to((pair(qn) * Q_SCALE)[:, None], (LANES, LANES))
            for qn in (qn_a[l], qn_b[l])], axis=0)
        qt_a, qt_b, kk, vt_a, vt_b = _qkv_call(
            x2, norm_mix[l][None, :], w_in[l, :, :n_qkv].astype(BF16),
            gains, gq, cos_t, sin_t, seq, tm=QKV_TILE)
        oa = _global_attn_call(qt_a, kk, vt_a, batch, seq)
        sink = jnp.repeat(sink_b[l].astype(F32) * LOG2E, WINDOW).reshape(
            N_KV, 1, GROUP * WINDOW)
        ob = _window_attn_call(qt_b, kk, vt_b, bias, sink, batch, seq)
        x2 = _mix_mlp_call(
            x2, oa, ob, norm_mix[l][None, :], b_gate[l][None, :], norm_mlp[l][None, :],
            w_in, w_o_a, w_o_b, w_out, w_mlp1, w_mlp2,
            layer=l, gate_col0=n_qkv, tm=MLP_TILE)
    return x2.reshape(batch, seq, d)
```

```python
import math
from functools import partial

import jax
import jax.numpy as jnp
from jax import lax
from jax.experimental import pallas as pl
from jax.experimental.pallas import tpu as pltpu

F32 = jnp.float32
BF16 = jnp.bfloat16

HEAD_DIM = 64
N_HEADS = 8
N_KV = 2
GROUP = N_HEADS // N_KV
GRID_W = 64
ROPE_THETA = 10000.0
WINDOW = 128
N_BUCKETS = 32
MAX_DISTANCE = 128
EPS = 1e-6
NEG_INF = -1e30

LANES = 128
SUBLANES = 8
VMEM_LIMIT = 56 * 1024 * 1024
ROT = HEAD_DIM // 4

QKV_TILE = 512
MLP_TILE = 512

ROW_BLOCK = 128
VT_ROWS = 80
LOG2E = math.log2(math.e)
Q_SCALE = HEAD_DIM ** -0.5 * LOG2E


def _rms(x, gain):
    ms = jnp.mean(x * x, axis=-1, keepdims=True)
    return x * lax.rsqrt(ms + EPS) * gain


def _qkv_body(*refs, layer):
    *io_refs, p0, p1, w_ref, stage, sem = refs
    win_hbm = io_refs[2]
    i = pl.program_id(0)

    @pl.when(i == 0)
    def _():
        p1[...] = jnp.zeros_like(p1)
        _stage_weights([(win_hbm.at[layer], 0, 0, w_ref)], stage, sem)

    @pl.when(i % 2 == 0)
    def _():
        _qkv_step(*io_refs, w_ref, p0, p1)

    @pl.when(i % 2 == 1)
    def _():
        _qkv_step(*io_refs, w_ref, p1, p0)


def _qkv_step(x_ref, gn_ref, _, gain_ref, gq_ref, cos_ref, sin_ref, cost_ref, sint_ref,
              ones_ref, qta_ref, qtb_ref, kk_ref, vta_ref, vtb_ref, w_ref, p_new, p_ref):
    tm = x_ref.shape[0]
    lane = lax.broadcasted_iota(jnp.int32, (tm, LANES), 1)
    pair_first = (lane & ROT) == 0
    cos = cos_ref[...]
    sin = sin_ref[...]

    def chunk(c):
        return p_ref[:, c * LANES:(c + 1) * LANES]

    def head_norm_pair(ca, gain_a, cb, gain_b):
        c2 = jnp.concatenate([ca, cb], axis=1)
        sq = c2 * c2
        hi = sq.astype(BF16)
        lo = (sq - hi.astype(F32)).astype(BF16)
        ss = (jnp.dot(hi, ones_ref[...], preferred_element_type=F32)
              + jnp.dot(lo, ones_ref[...], preferred_element_type=F32))
        y = c2 * lax.rsqrt(ss * (1.0 / HEAD_DIM) + EPS)
        return y[:, :LANES] * gain_a, y[:, LANES:] * gain_b

    def rope(c):
        ahead = pltpu.roll(c, LANES - ROT, 1)
        behind = pltpu.roll(c, ROT, 1)
        return c * cos + jnp.where(pair_first, ahead, behind) * sin

    def finish_q(ref, c, pc, gain_t, rotary):
        for r in range(tm // ROW_BLOCK):
            cols = slice(r * ROW_BLOCK, (r + 1) * ROW_BLOCK)
            bt = p_ref[cols, pc * LANES:(pc + 1) * LANES].T
            halves = []
            for hh in range(LANES // HEAD_DIM):
                xh = bt[hh * HEAD_DIM:(hh + 1) * HEAD_DIM]
                ms = jnp.sum(xh * xh, axis=0, keepdims=True) * (1.0 / HEAD_DIM)
                halves.append(xh * lax.rsqrt(ms + EPS))
            y = jnp.concatenate(halves, axis=0) * gain_t
            if rotary:
                slabs = [y[t * ROT:(t + 1) * ROT] for t in range(LANES // ROT)]
                partner = jnp.concatenate(
                    [slabs[t ^ 1] for t in range(LANES // ROT)], axis=0)
                y = y * cost_ref[:, cols] + partner * sint_ref[:, cols]
            ref[r, c * LANES:(c + 1) * LANES, :] = y.astype(BF16)

    kn_a, kn_b = gain_ref[0:1, :], gain_ref[1:2, :]
    for c in range(4):
        finish_q(qta_ref, c, c, gq_ref[:LANES, :], True)
        finish_q(qtb_ref, c, 6 + c, gq_ref[LANES:, :], False)
    ka, kb = head_norm_pair(chunk(4), kn_a, chunk(10), kn_b)
    kk_ref[:, :LANES] = rope(ka).astype(BF16)
    kk_ref[:, LANES:] = kb.astype(BF16)
    pad_rows = VT_ROWS - HEAD_DIM
    ones_row = (lax.broadcasted_iota(jnp.int32, (pad_rows, tm), 0) == 0).astype(F32)
    for ref, c in ((vta_ref, 5), (vtb_ref, 11)):
        vt = chunk(c).T
        ref[...] = jnp.concatenate(
            [vt[:HEAD_DIM], ones_row, vt[HEAD_DIM:], ones_row], axis=0).astype(BF16)
    h = _rms(x_ref[...], gn_ref[...]).astype(BF16)
    p_new[...] = jnp.dot(h, w_ref[...], preferred_element_type=F32)


def _qkv_call(x2, gn, w_in, layer, n_qkv, gains, gq, cos_t, sin_t, seq, tm):
    n, d = x2.shape
    cos_tt, sin_tt = cos_t.T, sin_t.T
    nseq = seq // tm
    qd = N_HEADS * HEAD_DIM
    head_ones = jnp.kron(jnp.eye(2 * LANES // HEAD_DIM, dtype=F32),
                         jnp.ones((HEAD_DIM, HEAD_DIM), F32)).astype(BF16)
    n_tiles = n // tm
    tile_in = lambda i: jnp.minimum(i, n_tiles - 1)
    tile_out = lambda i: jnp.maximum(i - 1, 0)
    qt_spec = pl.BlockSpec((tm // ROW_BLOCK, qd, ROW_BLOCK), lambda i: (tile_out(i), 0, 0))
    qt_shape = jax.ShapeDtypeStruct((n // ROW_BLOCK, qd, ROW_BLOCK), BF16)
    return pl.pallas_call(
        partial(_qkv_body, layer=layer),
        grid=(n_tiles + 1,),
        in_specs=[
            pl.BlockSpec((tm, d), lambda i: (tile_in(i), 0)),
            pl.BlockSpec((1, d), lambda i: (0, 0)),
            pl.BlockSpec(memory_space=pl.ANY),
            pl.BlockSpec(gains.shape, lambda i: (0, 0)),
            pl.BlockSpec(gq.shape, lambda i: (0, 0)),
            pl.BlockSpec((tm, LANES), lambda i: (tile_out(i) % nseq, 0)),
            pl.BlockSpec((tm, LANES), lambda i: (tile_out(i) % nseq, 0)),
            pl.BlockSpec((LANES, tm), lambda i: (0, tile_out(i) % nseq)),
            pl.BlockSpec((LANES, tm), lambda i: (0, tile_out(i) % nseq)),
            pl.BlockSpec(head_ones.shape, lambda i: (0, 0)),
        ],
        out_specs=[qt_spec, qt_spec,
                   pl.BlockSpec((tm, 2 * LANES), lambda i: (tile_out(i), 0)),
                   pl.BlockSpec((N_KV * VT_ROWS, tm), lambda i: (0, tile_out(i))),
                   pl.BlockSpec((N_KV * VT_ROWS, tm), lambda i: (0, tile_out(i)))],
        out_shape=[qt_shape, qt_shape,
                   jax.ShapeDtypeStruct((n, 2 * LANES), BF16),
                   jax.ShapeDtypeStruct((N_KV * VT_ROWS, n), BF16),
                   jax.ShapeDtypeStruct((N_KV * VT_ROWS, n), BF16)],
        scratch_shapes=[pltpu.VMEM((tm, n_qkv), F32)] * 2 + [
            pltpu.VMEM((d, n_qkv), BF16),
            pltpu.VMEM((STAGE_SLOTS, STAGE_ROWS, STAGE_COLS), F32),
            pltpu.SemaphoreType.DMA((STAGE_SLOTS,))],
        compiler_params=pltpu.CompilerParams(
            dimension_semantics=("arbitrary",), vmem_limit_bytes=VMEM_LIMIT),
        name="qkv_proj",
    )(x2, gn, w_in, gains, gq, cos_t, sin_t, cos_tt, sin_tt, head_ones)


def _query_rhs(q_ref, j, g):
    qt = jnp.concatenate(
        [q_ref[j, h * HEAD_DIM:(h + 1) * HEAD_DIM, :] for h in range(GROUP)], axis=1)
    zero = jnp.zeros_like(qt)
    return jnp.concatenate([jnp.where(g == 0, qt, zero), jnp.where(g == 0, zero, qt)], axis=0)


def _store_heads_t(o_ref, row0, o):
    rb = ROW_BLOCK
    o = jnp.concatenate([o[:, k * rb:(k + 1) * rb] for k in range(GROUP)], axis=0)
    o_ref[pl.ds(row0, rb), :] = o.T.astype(o_ref.dtype)


KEY_CHUNK = 256
GLOBAL_UNROLL = 4


def _global_attn_body(q_ref, k_ref, vt_ref, o_ref, s0, s1, m0, m1):
    s_buf, m_buf = (s0, s1), (m0, m1)
    g = pl.program_id(1)
    n_blocks = q_ref.shape[0]
    n_chunks = k_ref.shape[0] // KEY_CHUNK

    def keys(c):
        return slice(c * KEY_CHUNK, (c + 1) * KEY_CHUNK)

    def step(j_scores, j_finish, slot):
        m = None
        o = None
        if j_scores is not None:
            rhs = _query_rhs(q_ref, j_scores, g)
        for c in range(n_chunks):
            if j_scores is not None:
                s = jnp.dot(k_ref[keys(c), :], rhs, preferred_element_type=F32)
                s_buf[slot][keys(c), :] = s
                mc = jnp.max(s, axis=0, keepdims=True)
                m = mc if m is None else jnp.maximum(m, mc)
            if j_finish is not None:
                p = jnp.exp2(s_buf[1 - slot][keys(c), :] - m_buf[1 - slot][...]).astype(BF16)
                oc = jnp.dot(vt_ref[:, keys(c)], p, preferred_element_type=F32)
                o = oc if o is None else o + oc
        if j_scores is not None:
            m_buf[slot][...] = m
        if j_finish is not None:
            row0 = j_finish * ROW_BLOCK
            if not isinstance(row0, int):
                row0 = pl.multiple_of(row0, ROW_BLOCK)
            _store_heads_t(o_ref, row0, o[:HEAD_DIM] / o[HEAD_DIM:HEAD_DIM + 1])

    step(0, None, 0)
    unroll = GLOBAL_UNROLL
    n_loop = (n_blocks - 1) // unroll

    def steady(jj, carry):
        j = 1 + unroll * jj
        for u in range(unroll):
            step(j + u, j + u - 1, (1 + u) % 2)
        return carry

    lax.fori_loop(0, n_loop, steady, 0)
    for j in range(1 + unroll * n_loop, n_blocks):
        step(j, j - 1, j % 2)
    step(None, n_blocks - 1, n_blocks % 2)


def _global_attn_call(qt, kk, vt, batch, seq):
    n = kk.shape[0]
    nblk = seq // ROW_BLOCK
    cols = GROUP * ROW_BLOCK
    assert nblk % 2 == 0 and seq % KEY_CHUNK == 0
    return pl.pallas_call(
        _global_attn_body,
        grid=(batch, N_KV),
        in_specs=[
            pl.BlockSpec((nblk, GROUP * HEAD_DIM, ROW_BLOCK), lambda b, g: (b, g, 0)),
            pl.BlockSpec((seq, LANES), lambda b, g: (b, 0)),
            pl.BlockSpec((VT_ROWS, seq), lambda b, g: (g, b)),
        ],
        out_specs=pl.BlockSpec((seq, GROUP * HEAD_DIM), lambda b, g: (b, g)),
        out_shape=jax.ShapeDtypeStruct((n, N_HEADS * HEAD_DIM), BF16),
        scratch_shapes=(
            [pltpu.VMEM((seq, cols), F32)] * 2 + [pltpu.VMEM((1, cols), F32)] * 2),
        compiler_params=pltpu.CompilerParams(
            dimension_semantics=("arbitrary",) * 2, vmem_limit_bytes=VMEM_LIMIT),
        name="global_attn",
    )(qt, kk, vt)


def _window_attn_body(q_ref, k_ref, vt_ref, bias_ref, sink_ref, o_ref, s0, s1, m0, m1):
    w = WINDOW
    n_blocks = q_ref.shape[0]
    s_buf, m_buf = (s0, s1), (m0, m1)
    g = pl.program_id(1)
    sink = sink_ref[...]

    def band(j):
        lo, hi = max(j - 1, 0), min(j + 2, n_blocks)
        return slice(lo * w, hi * w), slice((lo - (j - 1)) * w, (hi - (j - 1)) * w)

    def step(j_scores, j_finish):
        if j_scores is not None:
            keys, rows = band(j_scores)
            n_keys = keys.stop - keys.start
            s = jnp.dot(k_ref[keys, :], _query_rhs(q_ref, j_scores, g),
                        preferred_element_type=F32) + bias_ref[rows, :]
            s_buf[j_scores % 2][:n_keys, :] = s
            m_buf[j_scores % 2][...] = jnp.maximum(jnp.max(s, axis=0, keepdims=True), sink)
        if j_finish is not None:
            keys, _ = band(j_finish)
            n_keys = keys.stop - keys.start
            m = m_buf[j_finish % 2][...]
            p = jnp.exp2(s_buf[j_finish % 2][:n_keys, :] - m).astype(BF16)
            o = jnp.dot(vt_ref[:, keys], p, preferred_element_type=F32)
            denom = o[HEAD_DIM:HEAD_DIM + 1] + jnp.exp2(sink - m)
            _store_heads_t(o_ref, j_finish * w, o[:HEAD_DIM] / denom)

    step(0, None)
    for j in range(1, n_blocks):
        step(j, j - 1)
    step(None, n_blocks - 1)


def _window_attn_call(qt, kk, vt, bias, sink, batch, seq):
    n = kk.shape[0]
    w = WINDOW
    nblk = seq // w
    cols = GROUP * w
    return pl.pallas_call(
        _window_attn_body,
        grid=(batch, N_KV),
        in_specs=[
            pl.BlockSpec((nblk, GROUP * HEAD_DIM, w), lambda b, g: (b, g, 0)),
            pl.BlockSpec((seq, LANES), lambda b, g: (b, 1)),
            pl.BlockSpec((VT_ROWS, seq), lambda b, g: (g, b)),
            pl.BlockSpec((None, 3 * w, cols), lambda b, g: (g, 0, 0)),
            pl.BlockSpec((None, 1, cols), lambda b, g: (g, 0, 0)),
        ],
        out_specs=pl.BlockSpec((seq, GROUP * HEAD_DIM), lambda b, g: (b, g)),
        out_shape=jax.ShapeDtypeStruct((n, N_HEADS * HEAD_DIM), BF16),
        scratch_shapes=(
            [pltpu.VMEM((3 * w, cols), F32)] * 2 + [pltpu.VMEM((1, cols), F32)] * 2),
        compiler_params=pltpu.CompilerParams(
            dimension_semantics=("arbitrary",) * 2, vmem_limit_bytes=VMEM_LIMIT),
        name="window_attn",
    )(qt, kk, vt, bias, sink)


STAGE_ROWS, STAGE_COLS = 256, 1024
STAGE_SLOTS = 4


def _stage_weights(jobs, stage, sem):
    chunks = []
    for src, row0, col0, dst in jobs:
        rows, cols = dst.shape
        width = STAGE_COLS if cols % STAGE_COLS == 0 else STAGE_COLS // 2
        assert rows % STAGE_ROWS == 0 and cols % width == 0
        for r in range(0, rows, STAGE_ROWS):
            for c in range(0, cols, width):
                chunks.append((src.at[pl.ds(row0 + r, STAGE_ROWS), pl.ds(col0 + c, width)],
                               dst.at[pl.ds(r, STAGE_ROWS), pl.ds(c, width)], width))

    n_slots = stage.shape[0]

    def copy(k):
        src, _, width = chunks[k]
        return pltpu.make_async_copy(src, stage.at[k % n_slots, :, pl.ds(0, width)],
                                     sem.at[k % n_slots])

    for k in range(min(n_slots - 1, len(chunks))):
        copy(k).start()
    for k in range(len(chunks)):
        if k + n_slots - 1 < len(chunks):
            copy(k + n_slots - 1).start()
        copy(k).wait()
        _, dst, width = chunks[k]
        dst[...] = stage[k % n_slots, :, :width].astype(BF16)


def _mix_mlp_body(x_ref, oa_ref, ob_ref, gn_ref, bg_ref, gm_ref,
                  win_hbm, woa_hbm, wob_hbm, wout_hbm, w1_hbm, w2_hbm, o_ref,
                  wg_ref, woa_ref, wob_ref, wout_ref, w1_ref, w2_ref, stage, sem, *, layer, col0):
    @pl.when(pl.program_id(0) == 0)
    def _():
        _stage_weights(
            [(win_hbm.at[layer], 0, col0, wg_ref), (woa_hbm.at[layer], 0, 0, woa_ref),
             (wob_hbm.at[layer], 0, 0, wob_ref), (wout_hbm.at[layer], 0, 0, wout_ref),
             (w1_hbm.at[layer], 0, 0, w1_ref), (w2_hbm.at[layer], 0, 0, w2_ref)],
            stage, sem)

    d = x_ref.shape[1]
    x = x_ref[...]
    h = _rms(x, gn_ref[...]).astype(BF16)
    gates = jnp.dot(h, wg_ref[...], preferred_element_type=F32) + bg_ref[...]
    g = jax.nn.sigmoid(gates)
    ya = jnp.dot(oa_ref[...], woa_ref[...], preferred_element_type=F32)
    yb = jnp.dot(ob_ref[...], wob_ref[...], preferred_element_type=F32)
    mixed = g[:, :d] * ya + g[:, d:] * yb
    x = x + jnp.dot(mixed.astype(BF16), wout_ref[...], preferred_element_type=F32)
    h = _rms(x, gm_ref[...]).astype(BF16)
    u = jnp.square(jnp.maximum(jnp.dot(h, w1_ref[...], preferred_element_type=F32), 0.0))
    o_ref[...] = x + jnp.dot(u.astype(BF16), w2_ref[...], preferred_element_type=F32)


def _resident(arr):
    return pl.BlockSpec(arr.shape, lambda i: (0,) * arr.ndim, pipeline_mode=pl.Buffered(1))


def _mix_mlp_call(x2, oa, ob, gn, bg, gm, w_in, w_o_a, w_o_b, w_out, w_mlp1, w_mlp2,
                  layer, gate_col0, tm):
    n, d = x2.shape
    row = lambda width: pl.BlockSpec((tm, width), lambda i: (i, 0))
    consts = (gn, bg, gm)
    weights = (w_in, w_o_a, w_o_b, w_out, w_mlp1, w_mlp2)
    n_gate = w_in.shape[2] - gate_col0
    bf16_shapes = [(d, n_gate)] + [w.shape[1:] for w in weights[1:]]
    return pl.pallas_call(
        partial(_mix_mlp_body, layer=layer, col0=gate_col0),
        grid=(n // tm,),
        in_specs=([row(d), row(oa.shape[1]), row(ob.shape[1])] + [_resident(a) for a in consts]
                  + [pl.BlockSpec(memory_space=pl.ANY)] * len(weights)),
        out_specs=row(d),
        out_shape=jax.ShapeDtypeStruct((n, d), F32),
        scratch_shapes=([pltpu.VMEM(shape, BF16) for shape in bf16_shapes]
                        + [pltpu.VMEM((STAGE_SLOTS, STAGE_ROWS, STAGE_COLS), F32),
                           pltpu.SemaphoreType.DMA((STAGE_SLOTS,))]),
        compiler_params=pltpu.CompilerParams(
            dimension_semantics=("arbitrary",), vmem_limit_bytes=VMEM_LIMIT),
        name="mix_mlp",
    )(x2, oa, ob, *consts, *weights)


def _rope_tables(seq):
    rows = seq // GRID_W
    row = jnp.repeat(jnp.arange(rows, dtype=jnp.int32), GRID_W)
    col = jnp.tile(jnp.arange(GRID_W, dtype=jnp.int32), rows)
    n_freq = HEAD_DIM // 4
    inv_freq = ROPE_THETA ** (-jnp.arange(n_freq, dtype=F32) / n_freq)
    ang_row = row.astype(F32)[:, None] * inv_freq[None, :]
    ang_col = col.astype(F32)[:, None] * inv_freq[None, :]
    cr, sr, cc, sc = jnp.cos(ang_row), jnp.sin(ang_row), jnp.cos(ang_col), jnp.sin(ang_col)
    cos64 = jnp.concatenate([cr, cr, cc, cc], axis=-1)
    sin64 = jnp.concatenate([-sr, sr, -sc, sc], axis=-1)
    return jnp.tile(cos64, (1, 2)), jnp.tile(sin64, (1, 2))


def _t5_bucket(rel):
    nb = N_BUCKETS // 2
    max_exact = nb // 2
    n = jnp.abs(rel)
    large = max_exact + (jnp.log(jnp.maximum(n, 1).astype(F32) / max_exact)
                         / math.log(MAX_DISTANCE / max_exact) * (nb - max_exact)).astype(jnp.int32)
    large = jnp.minimum(large, nb - 1)
    return jnp.where(rel > 0, nb, 0) + jnp.where(n < max_exact, n, large)


def _window_bias(rel_bias):
    w = WINDOW
    period = 4 * w
    slot = jnp.arange(period, dtype=jnp.int32)
    rel = jnp.where(slot < 3 * w, slot - w, slot - 5 * w)
    table = rel_bias.astype(F32)[_t5_bucket(rel)].T * LOG2E
    table = jnp.where(jnp.abs(rel)[None] <= w, table, NEG_INF)
    band = jnp.tile(table, (1, w))[:, :w * (period - 1)].reshape(-1, w, period - 1)
    band = band[:, :, :3 * w]
    return jnp.transpose(band.reshape(N_KV, GROUP * w, 3 * w), (0, 2, 1))


def kernel(x, w_in, b_gate, qn_a, kn_a, qn_b, kn_b, w_o_a, w_o_b, w_out,
           sink_b, rel_bias, norm_mix, norm_mlp, w_mlp1, w_mlp2):
    batch, seq, d = x.shape
    depth = w_in.shape[0]
    n_qkv = 2 * (N_HEADS + 2 * N_KV) * HEAD_DIM
    cos_t, sin_t = _rope_tables(seq)
    bias = _window_bias(rel_bias)
    x2 = x.reshape(batch * seq, d)
    pair = lambda v: jnp.tile(v, 2)
    for l in range(depth):
        gains = jnp.zeros((SUBLANES, LANES), F32).at[:2].set(
            jnp.stack([pair(kn_a[l]), pair(kn_b[l])]))
        gq = jnp.concatenate([
            jnp.broadcast_to((pair(qn) * Q_SCALE)[:, None], (LANES, LANES))
            for qn in (qn_a[l], qn_b[l])], axis=0)
        qt_a, qt_b, kk, vt_a, vt_b = _qkv_call(
            x2, norm_mix[l][None, :], w_in, l, n_qkv,
            gains, gq, cos_t, sin_t, seq, tm=QKV_TILE)
        oa = _global_attn_call(qt_a, kk, vt_a, batch, seq)
        sink = jnp.repeat(sink_b[l].astype(F32) * LOG2E, WINDOW).reshape(
            N_KV, 1, GROUP * WINDOW)
        ob = _window_attn_call(qt_b, kk, vt_b, bias, sink, batch, seq)
        x2 = _mix_mlp_call(
            x2, oa, ob, norm_mix[l][None, :], b_gate[l][None, :], norm_mlp[l][None, :],
            w_in, w_o_a, w_o_b, w_out, w_mlp1, w_mlp2,
            layer=l, gate_col0=n_qkv, tm=MLP_TILE)
    return x2.reshape(batch, seq, d)
```

```python
import math
from functools import partial

import jax
import jax.numpy as jnp
from jax import lax
from jax.experimental import pallas as pl
from jax.experimental.pallas import tpu as pltpu

F32 = jnp.float32
BF16 = jnp.bfloat16

HEAD_DIM = 64
N_HEADS = 8
N_KV = 2
GROUP = N_HEADS // N_KV
GRID_W = 64
ROPE_THETA = 10000.0
WINDOW = 128
N_BUCKETS = 32
MAX_DISTANCE = 128
EPS = 1e-6
NEG_INF = -1e30

LANES = 128
SUBLANES = 8
VMEM_LIMIT = 56 * 1024 * 1024
ROT = HEAD_DIM // 4

QKV_TILE = 512
MLP_TILE = 512

ROW_BLOCK = 128
VT_ROWS = 80
LOG2E = math.log2(math.e)
Q_SCALE = HEAD_DIM ** -0.5 * LOG2E


def _rms(x, gain):
    ms = jnp.mean(x * x, axis=-1, keepdims=True)
    return x * lax.rsqrt(ms + EPS) * gain


def _qkv_body(*refs, layer):
    *io_refs, p0, p1, w_ref, stage, sem = refs
    win_hbm = io_refs[2]
    i = pl.program_id(0)

    @pl.when(i == 0)
    def _():
        p1[...] = jnp.zeros_like(p1)
        _stage_weights([(win_hbm.at[layer], 0, 0, w_ref)], stage, sem)

    @pl.when(i % 2 == 0)
    def _():
        _qkv_step(*io_refs, w_ref, p0, p1)

    @pl.when(i % 2 == 1)
    def _():
        _qkv_step(*io_refs, w_ref, p1, p0)


def _qkv_step(x_ref, gn_ref, _, gain_ref, gq_ref, cos_ref, sin_ref, cost_ref, sint_ref,
              ones_ref, qta_ref, qtb_ref, kk_ref, vta_ref, vtb_ref, w_ref, p_new, p_ref):
    tm = x_ref.shape[0]
    lane = lax.broadcasted_iota(jnp.int32, (tm, LANES), 1)
    pair_first = (lane & ROT) == 0
    cos = cos_ref[...]
    sin = sin_ref[...]

    def chunk(c):
        return p_ref[:, c * LANES:(c + 1) * LANES]

    def head_norm_pair(ca, gain_a, cb, gain_b):
        c2 = jnp.concatenate([ca, cb], axis=1)
        sq = c2 * c2
        hi = sq.astype(BF16)
        lo = (sq - hi.astype(F32)).astype(BF16)
        ss = (jnp.dot(hi, ones_ref[...], preferred_element_type=F32)
              + jnp.dot(lo, ones_ref[...], preferred_element_type=F32))
        y = c2 * lax.rsqrt(ss * (1.0 / HEAD_DIM) + EPS)
        return y[:, :LANES] * gain_a, y[:, LANES:] * gain_b

    def rope(c):
        ahead = pltpu.roll(c, LANES - ROT, 1)
        behind = pltpu.roll(c, ROT, 1)
        return c * cos + jnp.where(pair_first, ahead, behind) * sin

    def finish_q(ref, c, pc, gain_t, rotary):
        for r in range(tm // ROW_BLOCK):
            cols = slice(r * ROW_BLOCK, (r + 1) * ROW_BLOCK)
            bt = p_ref[cols, pc * LANES:(pc + 1) * LANES].T
            halves = []
            for hh in range(LANES // HEAD_DIM):
                xh = bt[hh * HEAD_DIM:(hh + 1) * HEAD_DIM]
                ms = jnp.sum(xh * xh, axis=0, keepdims=True) * (1.0 / HEAD_DIM)
                halves.append(xh * lax.rsqrt(ms + EPS))
            y = jnp.concatenate(halves, axis=0) * gain_t
            if rotary:
                slabs = [y[t * ROT:(t + 1) * ROT] for t in range(LANES // ROT)]
                partner = jnp.concatenate(
                    [slabs[t ^ 1] for t in range(LANES // ROT)], axis=0)
                y = y * cost_ref[:, cols] + partner * sint_ref[:, cols]
            ref[r, c * LANES:(c + 1) * LANES, :] = y.astype(BF16)

    kn_a, kn_b = gain_ref[0:1, :], gain_ref[1:2, :]
    for c in range(4):
        finish_q(qta_ref, c, c, gq_ref[:LANES, :], True)
        finish_q(qtb_ref, c, 6 + c, gq_ref[LANES:, :], False)
    ka, kb = head_norm_pair(chunk(4), kn_a, chunk(10), kn_b)
    kk_ref[:, :LANES] = rope(ka).astype(BF16)
    kk_ref[:, LANES:] = kb.astype(BF16)
    pad_rows = VT_ROWS - HEAD_DIM
    ones_row = (lax.broadcasted_iota(jnp.int32, (pad_rows, tm), 0) == 0).astype(F32)
    for ref, c in ((vta_ref, 5), (vtb_ref, 11)):
        vt = chunk(c).T
        ref[...] = jnp.concatenate(
            [vt[:HEAD_DIM], ones_row, vt[HEAD_DIM:], ones_row], axis=0).astype(BF16)
    h = _rms(x_ref[...], gn_ref[...]).astype(BF16)
    p_new[...] = jnp.dot(h, w_ref[...], preferred_element_type=F32)


def _qkv_call(x2, gn, w_in, layer, n_qkv, gains, gq, cos_t, sin_t, seq, tm):
    n, d = x2.shape
    cos_tt, sin_tt = cos_t.T, sin_t.T
    nseq = seq // tm
    qd = N_HEADS * HEAD_DIM
    head_ones = jnp.kron(jnp.eye(2 * LANES // HEAD_DIM, dtype=F32),
                         jnp.ones((HEAD_DIM, HEAD_DIM), F32)).astype(BF16)
    n_tiles = n // tm
    tile_in = lambda i: jnp.minimum(i, n_tiles - 1)
    tile_out = lambda i: jnp.maximum(i - 1, 0)
    qt_spec = pl.BlockSpec((tm // ROW_BLOCK, qd, ROW_BLOCK), lambda i: (tile_out(i), 0, 0))
    qt_shape = jax.ShapeDtypeStruct((n // ROW_BLOCK, qd, ROW_BLOCK), BF16)
    return pl.pallas_call(
        partial(_qkv_body, layer=layer),
        grid=(n_tiles + 1,),
        in_specs=[
            pl.BlockSpec((tm, d), lambda i: (tile_in(i), 0)),
            pl.BlockSpec((1, d), lambda i: (0, 0)),
            pl.BlockSpec(memory_space=pl.ANY),
            pl.BlockSpec(gains.shape, lambda i: (0, 0)),
            pl.BlockSpec(gq.shape, lambda i: (0, 0)),
            pl.BlockSpec((tm, LANES), lambda i: (tile_out(i) % nseq, 0)),
            pl.BlockSpec((tm, LANES), lambda i: (tile_out(i) % nseq, 0)),
            pl.BlockSpec((LANES, tm), lambda i: (0, tile_out(i) % nseq)),
            pl.BlockSpec((LANES, tm), lambda i: (0, tile_out(i) % nseq)),
            pl.BlockSpec(head_ones.shape, lambda i: (0, 0)),
        ],
        out_specs=[qt_spec, qt_spec,
                   pl.BlockSpec((tm, 2 * LANES), lambda i: (tile_out(i), 0)),
                   pl.BlockSpec((N_KV * VT_ROWS, tm), lambda i: (0, tile_out(i))),
                   pl.BlockSpec((N_KV * VT_ROWS, tm), lambda i: (0, tile_out(i)))],
        out_shape=[qt_shape, qt_shape,
                   jax.ShapeDtypeStruct((n, 2 * LANES), BF16),
                   jax.ShapeDtypeStruct((N_KV * VT_ROWS, n), BF16),
                   jax.ShapeDtypeStruct((N_KV * VT_ROWS, n), BF16)],
        scratch_shapes=[pltpu.VMEM((tm, n_qkv), F32)] * 2 + [
            pltpu.VMEM((d, n_qkv), BF16),
            pltpu.VMEM((STAGE_SLOTS, STAGE_ROWS, STAGE_COLS), F32),
            pltpu.SemaphoreType.DMA((STAGE_SLOTS,))],
        compiler_params=pltpu.CompilerParams(
            dimension_semantics=("arbitrary",), vmem_limit_bytes=VMEM_LIMIT),
        name="qkv_proj",
    )(x2, gn, w_in, gains, gq, cos_t, sin_t, cos_tt, sin_tt, head_ones)


def _query_rhs(q_ref, j, g):
    qt = jnp.concatenate(
        [q_ref[j, h * HEAD_DIM:(h + 1) * HEAD_DIM, :] for h in range(GROUP)], axis=1)
    zero = jnp.zeros_like(qt)
    return jnp.concatenate([jnp.where(g == 0, qt, zero), jnp.where(g == 0, zero, qt)], axis=0)


def _store_heads_t(o_ref, row0, o):
    rb = ROW_BLOCK
    o = jnp.concatenate([o[:, k * rb:(k + 1) * rb] for k in range(GROUP)], axis=0)
    o_ref[pl.ds(row0, rb), :] = o.T.astype(o_ref.dtype)


KEY_CHUNK = 256
GLOBAL_UNROLL = 4
PV_DELAY = 1


def _global_attn_body(q_ref, k_ref, vt_ref, o_ref, s0, s1, m0, m1):
    s_buf, m_buf = (s0, s1), (m0, m1)
    g = pl.program_id(1)
    n_blocks = q_ref.shape[0]
    n_chunks = k_ref.shape[0] // KEY_CHUNK

    def keys(c):
        return slice(c * KEY_CHUNK, (c + 1) * KEY_CHUNK)

    def step(j_scores, j_finish, slot):
        m = None
        o = None
        pending = []
        if j_scores is not None:
            rhs = _query_rhs(q_ref, j_scores, g)

        def accumulate(o, c, p):
            oc = jnp.dot(vt_ref[:, keys(c)], p, preferred_element_type=F32)
            return oc if o is None else o + oc

        for c in range(n_chunks):
            if j_scores is not None:
                s = jnp.dot(k_ref[keys(c), :], rhs, preferred_element_type=F32)
                s_buf[slot][keys(c), :] = s
                mc = jnp.max(s, axis=0, keepdims=True)
                m = mc if m is None else jnp.maximum(m, mc)
            if j_finish is not None:
                p = jnp.exp2(s_buf[1 - slot][keys(c), :] - m_buf[1 - slot][...]).astype(BF16)
                pending.append((c, p))
                if len(pending) > PV_DELAY:
                    o = accumulate(o, *pending.pop(0))
        for item in pending:
            o = accumulate(o, *item)
        if j_scores is not None:
            m_buf[slot][...] = m
        if j_finish is not None:
            row0 = j_finish * ROW_BLOCK
            if not isinstance(row0, int):
                row0 = pl.multiple_of(row0, ROW_BLOCK)
            _store_heads_t(o_ref, row0, o[:HEAD_DIM] / o[HEAD_DIM:HEAD_DIM + 1])

    step(0, None, 0)
    unroll = GLOBAL_UNROLL
    n_loop = (n_blocks - 1) // unroll

    def steady(jj, carry):
        j = 1 + unroll * jj
        for u in range(unroll):
            step(j + u, j + u - 1, (1 + u) % 2)
        return carry

    lax.fori_loop(0, n_loop, steady, 0)
    for j in range(1 + unroll * n_loop, n_blocks):
        step(j, j - 1, j % 2)
    step(None, n_blocks - 1, n_blocks % 2)


def _global_attn_call(qt, kk, vt, batch, seq):
    n = kk.shape[0]
    nblk = seq // ROW_BLOCK
    cols = GROUP * ROW_BLOCK
    assert nblk % 2 == 0 and seq % KEY_CHUNK == 0
    return pl.pallas_call(
        _global_attn_body,
        grid=(batch, N_KV),
        in_specs=[
            pl.BlockSpec((nblk, GROUP * HEAD_DIM, ROW_BLOCK), lambda b, g: (b, g, 0)),
            pl.BlockSpec((seq, LANES), lambda b, g: (b, 0)),
            pl.BlockSpec((VT_ROWS, seq), lambda b, g: (g, b)),
        ],
        out_specs=pl.BlockSpec((seq, GROUP * HEAD_DIM), lambda b, g: (b, g)),
        out_shape=jax.ShapeDtypeStruct((n, N_HEADS * HEAD_DIM), BF16),
        scratch_shapes=(
            [pltpu.VMEM((seq, cols), F32)] * 2 + [pltpu.VMEM((1, cols), F32)] * 2),
        compiler_params=pltpu.CompilerParams(
            dimension_semantics=("arbitrary",) * 2, vmem_limit_bytes=VMEM_LIMIT),
        name="global_attn",
    )(qt, kk, vt)


def _window_attn_body(q_ref, k_ref, vt_ref, bias_ref, sink_ref, o_ref, s0, s1, m0, m1):
    w = WINDOW
    n_blocks = q_ref.shape[0]
    s_buf, m_buf = (s0, s1), (m0, m1)
    g = pl.program_id(1)
    sink = sink_ref[...]

    def band(j):
        lo, hi = max(j - 1, 0), min(j + 2, n_blocks)
        return slice(lo * w, hi * w), slice((lo - (j - 1)) * w, (hi - (j - 1)) * w)

    def step(j_scores, j_finish):
        if j_scores is not None:
            keys, rows = band(j_scores)
            n_keys = keys.stop - keys.start
            s = jnp.dot(k_ref[keys, :], _query_rhs(q_ref, j_scores, g),
                        preferred_element_type=F32) + bias_ref[rows, :]
            s_buf[j_scores % 2][:n_keys, :] = s
            m_buf[j_scores % 2][...] = jnp.maximum(jnp.max(s, axis=0, keepdims=True), sink)
        if j_finish is not None:
            keys, _ = band(j_finish)
            n_keys = keys.stop - keys.start
            m = m_buf[j_finish % 2][...]
            p = jnp.exp2(s_buf[j_finish % 2][:n_keys, :] - m).astype(BF16)
            o = jnp.dot(vt_ref[:, keys], p, preferred_element_type=F32)
            denom = o[HEAD_DIM:HEAD_DIM + 1] + jnp.exp2(sink - m)
            _store_heads_t(o_ref, j_finish * w, o[:HEAD_DIM] / denom)

    step(0, None)
    for j in range(1, n_blocks):
        step(j, j - 1)
    step(None, n_blocks - 1)


def _window_attn_call(qt, kk, vt, bias, sink, batch, seq):
    n = kk.shape[0]
    w = WINDOW
    nblk = seq // w
    cols = GROUP * w
    return pl.pallas_call(
        _window_attn_body,
        grid=(batch, N_KV),
        in_specs=[
            pl.BlockSpec((nblk, GROUP * HEAD_DIM, w), lambda b, g: (b, g, 0)),
            pl.BlockSpec((seq, LANES), lambda b, g: (b, 1)),
            pl.BlockSpec((VT_ROWS, seq), lambda b, g: (g, b)),
            pl.BlockSpec((None, 3 * w, cols), lambda b, g: (g, 0, 0)),
            pl.BlockSpec((None, 1, cols), lambda b, g: (g, 0, 0)),
        ],
        out_specs=pl.BlockSpec((seq, GROUP * HEAD_DIM), lambda b, g: (b, g)),
        out_shape=jax.ShapeDtypeStruct((n, N_HEADS * HEAD_DIM), BF16),
        scratch_shapes=(
            [pltpu.VMEM((3 * w, cols), F32)] * 2 + [pltpu.VMEM((1, cols), F32)] * 2),
        compiler_params=pltpu.CompilerParams(
            dimension_semantics=("arbitrary",) * 2, vmem_limit_bytes=VMEM_LIMIT),
        name="window_attn",
    )(qt, kk, vt, bias, sink)


STAGE_ROWS, STAGE_COLS = 256, 1024
STAGE_SLOTS = 4


def _stage_weights(jobs, stage, sem):
    chunks = []
    for src, row0, col0, dst in jobs:
        rows, cols = dst.shape
        width = STAGE_COLS if cols % STAGE_COLS == 0 else STAGE_COLS // 2
        assert rows % STAGE_ROWS == 0 and cols % width == 0
        for r in range(0, rows, STAGE_ROWS):
            for c in range(0, cols, width):
                chunks.append((src.at[pl.ds(row0 + r, STAGE_ROWS), pl.ds(col0 + c, width)],
                               dst.at[pl.ds(r, STAGE_ROWS), pl.ds(c, width)], width))

    n_slots = stage.shape[0]

    def copy(k):
        src, _, width = chunks[k]
        return pltpu.make_async_copy(src, stage.at[k % n_slots, :, pl.ds(0, width)],
                                     sem.at[k % n_slots])

    for k in range(min(n_slots - 1, len(chunks))):
        copy(k).start()
    for k in range(len(chunks)):
        if k + n_slots - 1 < len(chunks):
            copy(k + n_slots - 1).start()
        copy(k).wait()
        _, dst, width = chunks[k]
        dst[...] = stage[k % n_slots, :, :width].astype(BF16)


def _mix_mlp_body(x_ref, oa_ref, ob_ref, gn_ref, bg_ref, gm_ref,
                  win_hbm, woa_hbm, wob_hbm, wout_hbm, w1_hbm, w2_hbm, o_ref,
                  wg_ref, woa_ref, wob_ref, wout_ref, w1_ref, w2_ref, stage, sem, *, layer, col0):
    @pl.when(pl.program_id(0) == 0)
    def _():
        _stage_weights(
            [(win_hbm.at[layer], 0, col0, wg_ref), (woa_hbm.at[layer], 0, 0, woa_ref),
             (wob_hbm.at[layer], 0, 0, wob_ref), (wout_hbm.at[layer], 0, 0, wout_ref),
             (w1_hbm.at[layer], 0, 0, w1_ref), (w2_hbm.at[layer], 0, 0, w2_ref)],
            stage, sem)

    d = x_ref.shape[1]
    x = x_ref[...]
    h = _rms(x, gn_ref[...]).astype(BF16)
    gates = jnp.dot(h, wg_ref[...], preferred_element_type=F32) + bg_ref[...]
    g = jax.nn.sigmoid(gates)
    ya = jnp.dot(oa_ref[...], woa_ref[...], preferred_element_type=F32)
    yb = jnp.dot(ob_ref[...], wob_ref[...], preferred_element_type=F32)
    mixed = g[:, :d] * ya + g[:, d:] * yb
    x = x + jnp.dot(mixed.astype(BF16), wout_ref[...], preferred_element_type=F32)
    h = _rms(x, gm_ref[...]).astype(BF16)
    u = jnp.square(jnp.maximum(jnp.dot(h, w1_ref[...], preferred_element_type=F32), 0.0))
    o_ref[...] = x + jnp.dot(u.astype(BF16), w2_ref[...], preferred_element_type=F32)


def _resident(arr):
    return pl.BlockSpec(arr.shape, lambda i: (0,) * arr.ndim, pipeline_mode=pl.Buffered(1))


def _mix_mlp_call(x2, oa, ob, gn, bg, gm, w_in, w_o_a, w_o_b, w_out, w_mlp1, w_mlp2,
                  layer, gate_col0, tm):
    n, d = x2.shape
    row = lambda width: pl.BlockSpec((tm, width), lambda i: (i, 0))
    consts = (gn, bg, gm)
    weights = (w_in, w_o_a, w_o_b, w_out, w_mlp1, w_mlp2)
    n_gate = w_in.shape[2] - gate_col0
    bf16_shapes = [(d, n_gate)] + [w.shape[1:] for w in weights[1:]]
    return pl.pallas_call(
        partial(_mix_mlp_body, layer=layer, col0=gate_col0),
        grid=(n // tm,),
        in_specs=([row(d), row(oa.shape[1]), row(ob.shape[1])] + [_resident(a) for a in consts]
                  + [pl.BlockSpec(memory_space=pl.ANY)] * len(weights)),
        out_specs=row(d),
        out_shape=jax.ShapeDtypeStruct((n, d), F32),
        scratch_shapes=([pltpu.VMEM(shape, BF16) for shape in bf16_shapes]
                        + [pltpu.VMEM((STAGE_SLOTS, STAGE_ROWS, STAGE_COLS), F32),
                           pltpu.SemaphoreType.DMA((STAGE_SLOTS,))]),
        compiler_params=pltpu.CompilerParams(
            dimension_semantics=("arbitrary",), vmem_limit_bytes=VMEM_LIMIT),
        name="mix_mlp",
    )(x2, oa, ob, *consts, *weights)


def _rope_tables(seq):
    rows = seq // GRID_W
    row = jnp.repeat(jnp.arange(rows, dtype=jnp.int32), GRID_W)
    col = jnp.tile(jnp.arange(GRID_W, dtype=jnp.int32), rows)
    n_freq = HEAD_DIM // 4
    inv_freq = ROPE_THETA ** (-jnp.arange(n_freq, dtype=F32) / n_freq)
    ang_row = row.astype(F32)[:, None] * inv_freq[None, :]
    ang_col = col.astype(F32)[:, None] * inv_freq[None, :]
    cr, sr, cc, sc = jnp.cos(ang_row), jnp.sin(ang_row), jnp.cos(ang_col), jnp.sin(ang_col)
    cos64 = jnp.concatenate([cr, cr, cc, cc], axis=-1)
    sin64 = jnp.concatenate([-sr, sr, -sc, sc], axis=-1)
    return jnp.tile(cos64, (1, 2)), jnp.tile(sin64, (1, 2))


def _t5_bucket(rel):
    nb = N_BUCKETS // 2
    max_exact = nb // 2
    n = jnp.abs(rel)
    large = max_exact + (jnp.log(jnp.maximum(n, 1).astype(F32) / max_exact)
                         / math.log(MAX_DISTANCE / max_exact) * (nb - max_exact)).astype(jnp.int32)
    large = jnp.minimum(large, nb - 1)
    return jnp.where(rel > 0, nb, 0) + jnp.where(n < max_exact, n, large)


def _window_bias(rel_bias):
    w = WINDOW
    period = 4 * w
    slot = jnp.arange(period, dtype=jnp.int32)
    rel = jnp.where(slot < 3 * w, slot - w, slot - 5 * w)
    table = rel_bias.astype(F32)[_t5_bucket(rel)].T * LOG2E
    table = jnp.where(jnp.abs(rel)[None] <= w, table, NEG_INF)
    band = jnp.tile(table, (1, w))[:, :w * (period - 1)].reshape(-1, w, period - 1)
    band = band[:, :, :3 * w]
    return jnp.transpose(band.reshape(N_KV, GROUP * w, 3 * w), (0, 2, 1))


def kernel(x, w_in, b_gate, qn_a, kn_a, qn_b, kn_b, w_o_a, w_o_b, w_out,
           sink_b, rel_bias, norm_mix, norm_mlp, w_mlp1, w_mlp2):
    batch, seq, d = x.shape
    depth = w_in.shape[0]
    n_qkv = 2 * (N_HEADS + 2 * N_KV) * HEAD_DIM
    cos_t, sin_t = _rope_tables(seq)
    bias = _window_bias(rel_bias)
    x2 = x.reshape(batch * seq, d)
    pair = lambda v: jnp.tile(v, 2)
    for l in range(depth):
        gains = jnp.zeros((SUBLANES, LANES), F32).at[:2].set(
            jnp.stack([pair(kn_a[l]), pair(kn_b[l])]))
        gq = jnp.concatenate([
            jnp.broadcast_to((pair(qn) * Q_SCALE)[:, None], (LANES, LANES))
            for qn in (qn_a[l], qn_b[l])], axis=0)
        qt_a, qt_b, kk, vt_a, vt_b = _qkv_call(
            x2, norm_mix[l][None, :], w_in, l, n_qkv,
            gains, gq, cos_t, sin_t, seq, tm=QKV_TILE)
        oa = _global_attn_call(qt_a, kk, vt_a, batch, seq)
        sink = jnp.repeat(sink_b[l].astype(F32) * LOG2E, WINDOW).reshape(
            N_KV, 1, GROUP * WINDOW)
        ob = _window_attn_call(qt_b, kk, vt_b, bias, sink, batch, seq)
        x2 = _mix_mlp_call(
            x2, oa, ob, norm_mix[l][None, :], b_gate[l][None, :], norm_mlp[l][None, :],
            w_in, w_o_a, w_o_b, w_out, w_mlp1, w_mlp2,
            layer=l, gate_col0=n_qkv, tm=MLP_TILE)
    return x2.reshape(batch, seq, d)
```

```python
import math
from functools import partial

import jax
import jax.numpy as jnp
from jax import lax
from jax.experimental import pallas as pl
from jax.experimental.pallas import tpu as pltpu

F32 = jnp.float32
BF16 = jnp.bfloat16

HEAD_DIM = 64
N_HEADS = 8
N_KV = 2
GROUP = N_HEADS // N_KV
GRID_W = 64
ROPE_THETA = 10000.0
WINDOW = 128
N_BUCKETS = 32
MAX_DISTANCE = 128
EPS = 1e-6
NEG_INF = -1e30

LANES = 128
SUBLANES = 8
VMEM_LIMIT = 56 * 1024 * 1024
ROT = HEAD_DIM // 4

QKV_TILE = 512
MLP_TILE = 512
MLP_GROUPS = 2

ROW_BLOCK = 128
VT_ROWS = 80
LOG2E = math.log2(math.e)
Q_SCALE = HEAD_DIM ** -0.5 * LOG2E


def _rms(x, gain):
    ms = jnp.mean(x * x, axis=-1, keepdims=True)
    return x * lax.rsqrt(ms + EPS) * gain


def _qkv_body(*refs, layer):
    *io_refs, p0, p1, w_ref, stage, sem = refs
    win_hbm = io_refs[2]
    i = pl.program_id(0)

    @pl.when(i == 0)
    def _():
        p1[...] = jnp.zeros_like(p1)
        _stage_weights([(win_hbm.at[layer], 0, 0, w_ref)], stage, sem)

    @pl.when(i % 2 == 0)
    def _():
        _qkv_step(*io_refs, w_ref, p0, p1)

    @pl.when(i % 2 == 1)
    def _():
        _qkv_step(*io_refs, w_ref, p1, p0)


def _qkv_step(x_ref, gn_ref, _, gain_ref, gq_ref, cos_ref, sin_ref, cost_ref, sint_ref,
              ones_ref, qta_ref, qtb_ref, kk_ref, vta_ref, vtb_ref, w_ref, p_new, p_ref):
    tm = x_ref.shape[0]
    lane = lax.broadcasted_iota(jnp.int32, (tm, LANES), 1)
    pair_first = (lane & ROT) == 0
    cos = cos_ref[...]
    sin = sin_ref[...]

    def chunk(c):
        return p_ref[:, c * LANES:(c + 1) * LANES]

    def head_norm_pair(ca, gain_a, cb, gain_b):
        c2 = jnp.concatenate([ca, cb], axis=1)
        sq = c2 * c2
        hi = sq.astype(BF16)
        lo = (sq - hi.astype(F32)).astype(BF16)
        ss = (jnp.dot(hi, ones_ref[...], preferred_element_type=F32)
              + jnp.dot(lo, ones_ref[...], preferred_element_type=F32))
        y = c2 * lax.rsqrt(ss * (1.0 / HEAD_DIM) + EPS)
        return y[:, :LANES] * gain_a, y[:, LANES:] * gain_b

    def rope(c):
        ahead = pltpu.roll(c, LANES - ROT, 1)
        behind = pltpu.roll(c, ROT, 1)
        return c * cos + jnp.where(pair_first, ahead, behind) * sin

    def finish_q(ref, c, pc, gain_t, rotary):
        for r in range(tm // ROW_BLOCK):
            cols = slice(r * ROW_BLOCK, (r + 1) * ROW_BLOCK)
            bt = p_ref[cols, pc * LANES:(pc + 1) * LANES].T
            halves = []
            for hh in range(LANES // HEAD_DIM):
                xh = bt[hh * HEAD_DIM:(hh + 1) * HEAD_DIM]
                ms = jnp.sum(xh * xh, axis=0, keepdims=True) * (1.0 / HEAD_DIM)
                halves.append(xh * lax.rsqrt(ms + EPS))
            y = jnp.concatenate(halves, axis=0) * gain_t
            if rotary:
                slabs = [y[t * ROT:(t + 1) * ROT] for t in range(LANES // ROT)]
                partner = jnp.concatenate(
                    [slabs[t ^ 1] for t in range(LANES // ROT)], axis=0)
                y = y * cost_ref[:, cols] + partner * sint_ref[:, cols]
            ref[r, c * LANES:(c + 1) * LANES, :] = y.astype(BF16)

    kn_a, kn_b = gain_ref[0:1, :], gain_ref[1:2, :]
    for c in range(4):
        finish_q(qta_ref, c, c, gq_ref[:LANES, :], True)
        finish_q(qtb_ref, c, 6 + c, gq_ref[LANES:, :], False)
    ka, kb = head_norm_pair(chunk(4), kn_a, chunk(10), kn_b)
    kk_ref[:, :LANES] = rope(ka).astype(BF16)
    kk_ref[:, LANES:] = kb.astype(BF16)
    pad_rows = VT_ROWS - HEAD_DIM
    ones_row = (lax.broadcasted_iota(jnp.int32, (pad_rows, tm), 0) == 0).astype(F32)
    for ref, c in ((vta_ref, 5), (vtb_ref, 11)):
        vt = chunk(c).T
        ref[...] = jnp.concatenate(
            [vt[:HEAD_DIM], ones_row, vt[HEAD_DIM:], ones_row], axis=0).astype(BF16)
    h = _rms(x_ref[...], gn_ref[...]).astype(BF16)
    p_new[...] = jnp.dot(h, w_ref[...], preferred_element_type=F32)


def _qkv_call(x2, gn, w_in, layer, n_qkv, gains, gq, cos_t, sin_t, seq, tm):
    n, d = x2.shape
    cos_tt, sin_tt = cos_t.T, sin_t.T
    nseq = seq // tm
    qd = N_HEADS * HEAD_DIM
    head_ones = jnp.kron(jnp.eye(2 * LANES // HEAD_DIM, dtype=F32),
                         jnp.ones((HEAD_DIM, HEAD_DIM), F32)).astype(BF16)
    n_tiles = n // tm
    tile_in = lambda i: jnp.minimum(i, n_tiles - 1)
    tile_out = lambda i: jnp.maximum(i - 1, 0)
    qt_spec = pl.BlockSpec((tm // ROW_BLOCK, qd, ROW_BLOCK), lambda i: (tile_out(i), 0, 0))
    qt_shape = jax.ShapeDtypeStruct((n // ROW_BLOCK, qd, ROW_BLOCK), BF16)
    return pl.pallas_call(
        partial(_qkv_body, layer=layer),
        grid=(n_tiles + 1,),
        in_specs=[
            pl.BlockSpec((tm, d), lambda i: (tile_in(i), 0)),
            pl.BlockSpec((1, d), lambda i: (0, 0)),
            pl.BlockSpec(memory_space=pl.ANY),
            pl.BlockSpec(gains.shape, lambda i: (0, 0)),
            pl.BlockSpec(gq.shape, lambda i: (0, 0)),
            pl.BlockSpec((tm, LANES), lambda i: (tile_out(i) % nseq, 0)),
            pl.BlockSpec((tm, LANES), lambda i: (tile_out(i) % nseq, 0)),
            pl.BlockSpec((LANES, tm), lambda i: (0, tile_out(i) % nseq)),
            pl.BlockSpec((LANES, tm), lambda i: (0, tile_out(i) % nseq)),
            pl.BlockSpec(head_ones.shape, lambda i: (0, 0)),
        ],
        out_specs=[qt_spec, qt_spec,
                   pl.BlockSpec((tm, 2 * LANES), lambda i: (tile_out(i), 0)),
                   pl.BlockSpec((N_KV * VT_ROWS, tm), lambda i: (0, tile_out(i))),
                   pl.BlockSpec((N_KV * VT_ROWS, tm), lambda i: (0, tile_out(i)))],
        out_shape=[qt_shape, qt_shape,
                   jax.ShapeDtypeStruct((n, 2 * LANES), BF16),
                   jax.ShapeDtypeStruct((N_KV * VT_ROWS, n), BF16),
                   jax.ShapeDtypeStruct((N_KV * VT_ROWS, n), BF16)],
        scratch_shapes=[pltpu.VMEM((tm, n_qkv), F32)] * 2 + [
            pltpu.VMEM((d, n_qkv), BF16),
            pltpu.VMEM((STAGE_SLOTS, STAGE_ROWS, STAGE_COLS), F32),
            pltpu.SemaphoreType.DMA((STAGE_SLOTS,))],
        compiler_params=pltpu.CompilerParams(
            dimension_semantics=("arbitrary",), vmem_limit_bytes=VMEM_LIMIT),
        name="qkv_proj",
    )(x2, gn, w_in, gains, gq, cos_t, sin_t, cos_tt, sin_tt, head_ones)


def _query_rhs(q_ref, j, g):
    qt = jnp.concatenate(
        [q_ref[j, h * HEAD_DIM:(h + 1) * HEAD_DIM, :] for h in range(GROUP)], axis=1)
    zero = jnp.zeros_like(qt)
    return jnp.concatenate([jnp.where(g == 0, qt, zero), jnp.where(g == 0, zero, qt)], axis=0)


def _store_heads_t(o_ref, row0, o):
    rb = ROW_BLOCK
    o = jnp.concatenate([o[:, k * rb:(k + 1) * rb] for k in range(GROUP)], axis=0)
    o_ref[pl.ds(row0, rb), :] = o.T.astype(o_ref.dtype)


KEY_CHUNK = 256
GLOBAL_UNROLL = 6
PV_DELAY = 1


def _global_attn_body(q_ref, k_ref, vt_ref, o_ref, s0, s1, m0, m1):
    s_buf, m_buf = (s0, s1), (m0, m1)
    g = pl.program_id(1)
    n_blocks = q_ref.shape[0]
    n_chunks = k_ref.shape[0] // KEY_CHUNK

    def keys(c):
        return slice(c * KEY_CHUNK, (c + 1) * KEY_CHUNK)

    def step(j_scores, j_finish, slot):
        m = None
        o = None
        pending = []
        if j_scores is not None:
            rhs = _query_rhs(q_ref, j_scores, g)

        def accumulate(o, c, p):
            oc = jnp.dot(vt_ref[:, keys(c)], p, preferred_element_type=F32)
            return oc if o is None else o + oc

        for c in range(n_chunks):
            if j_scores is not None:
                s = jnp.dot(k_ref[keys(c), :], rhs, preferred_element_type=F32)
                s_buf[slot][keys(c), :] = s
                mc = jnp.max(s, axis=0, keepdims=True)
                m = mc if m is None else jnp.maximum(m, mc)
            if j_finish is not None:
                p = jnp.exp2(s_buf[1 - slot][keys(c), :] - m_buf[1 - slot][...]).astype(BF16)
                pending.append((c, p))
                if len(pending) > PV_DELAY:
                    o = accumulate(o, *pending.pop(0))
        for item in pending:
            o = accumulate(o, *item)
        if j_scores is not None:
            m_buf[slot][...] = m
        if j_finish is not None:
            row0 = j_finish * ROW_BLOCK
            if not isinstance(row0, int):
                row0 = pl.multiple_of(row0, ROW_BLOCK)
            _store_heads_t(o_ref, row0, o[:HEAD_DIM] / o[HEAD_DIM:HEAD_DIM + 1])

    step(0, None, 0)
    unroll = GLOBAL_UNROLL
    n_loop = (n_blocks - 1) // unroll

    def steady(jj, carry):
        j = 1 + unroll * jj
        for u in range(unroll):
            step(j + u, j + u - 1, (1 + u) % 2)
        return carry

    lax.fori_loop(0, n_loop, steady, 0)
    for j in range(1 + unroll * n_loop, n_blocks):
        step(j, j - 1, j % 2)
    step(None, n_blocks - 1, n_blocks % 2)


def _global_attn_call(qt, kk, vt, batch, seq):
    n = kk.shape[0]
    nblk = seq // ROW_BLOCK
    cols = GROUP * ROW_BLOCK
    assert nblk % 2 == 0 and seq % KEY_CHUNK == 0
    return pl.pallas_call(
        _global_attn_body,
        grid=(batch, N_KV),
        in_specs=[
            pl.BlockSpec((nblk, GROUP * HEAD_DIM, ROW_BLOCK), lambda b, g: (b, g, 0)),
            pl.BlockSpec((seq, LANES), lambda b, g: (b, 0)),
            pl.BlockSpec((VT_ROWS, seq), lambda b, g: (g, b)),
        ],
        out_specs=pl.BlockSpec((seq, GROUP * HEAD_DIM), lambda b, g: (b, g)),
        out_shape=jax.ShapeDtypeStruct((n, N_HEADS * HEAD_DIM), BF16),
        scratch_shapes=(
            [pltpu.VMEM((seq, cols), F32)] * 2 + [pltpu.VMEM((1, cols), F32)] * 2),
        compiler_params=pltpu.CompilerParams(
            dimension_semantics=("arbitrary",) * 2, vmem_limit_bytes=VMEM_LIMIT),
        name="global_attn",
    )(qt, kk, vt)


def _window_attn_body(q_ref, k_ref, vt_ref, bias_ref, sink_ref, o_ref, s0, s1, m0, m1):
    w = WINDOW
    n_blocks = q_ref.shape[0]
    s_buf, m_buf = (s0, s1), (m0, m1)
    g = pl.program_id(1)
    sink = sink_ref[...]

    def band(j):
        lo, hi = max(j - 1, 0), min(j + 2, n_blocks)
        return slice(lo * w, hi * w), slice((lo - (j - 1)) * w, (hi - (j - 1)) * w)

    def step(j_scores, j_finish):
        if j_scores is not None:
            keys, rows = band(j_scores)
            n_keys = keys.stop - keys.start
            s = jnp.dot(k_ref[keys, :], _query_rhs(q_ref, j_scores, g),
                        preferred_element_type=F32) + bias_ref[rows, :]
            s_buf[j_scores % 2][:n_keys, :] = s
            m_buf[j_scores % 2][...] = jnp.maximum(jnp.max(s, axis=0, keepdims=True), sink)
        if j_finish is not None:
            keys, _ = band(j_finish)
            n_keys = keys.stop - keys.start
            m = m_buf[j_finish % 2][...]
            p = jnp.exp2(s_buf[j_finish % 2][:n_keys, :] - m).astype(BF16)
            o = jnp.dot(vt_ref[:, keys], p, preferred_element_type=F32)
            denom = o[HEAD_DIM:HEAD_DIM + 1] + jnp.exp2(sink - m)
            _store_heads_t(o_ref, j_finish * w, o[:HEAD_DIM] / denom)

    step(0, None)
    for j in range(1, n_blocks):
        step(j, j - 1)
    step(None, n_blocks - 1)


def _window_attn_call(qt, kk, vt, bias, sink, batch, seq):
    n = kk.shape[0]
    w = WINDOW
    nblk = seq // w
    cols = GROUP * w
    return pl.pallas_call(
        _window_attn_body,
        grid=(batch, N_KV),
        in_specs=[
            pl.BlockSpec((nblk, GROUP * HEAD_DIM, w), lambda b, g: (b, g, 0)),
            pl.BlockSpec((seq, LANES), lambda b, g: (b, 1)),
            pl.BlockSpec((VT_ROWS, seq), lambda b, g: (g, b)),
            pl.BlockSpec((None, 3 * w, cols), lambda b, g: (g, 0, 0)),
            pl.BlockSpec((None, 1, cols), lambda b, g: (g, 0, 0)),
        ],
        out_specs=pl.BlockSpec((seq, GROUP * HEAD_DIM), lambda b, g: (b, g)),
        out_shape=jax.ShapeDtypeStruct((n, N_HEADS * HEAD_DIM), BF16),
        scratch_shapes=(
            [pltpu.VMEM((3 * w, cols), F32)] * 2 + [pltpu.VMEM((1, cols), F32)] * 2),
        compiler_params=pltpu.CompilerParams(
            dimension_semantics=("arbitrary",) * 2, vmem_limit_bytes=VMEM_LIMIT),
        name="window_attn",
    )(qt, kk, vt, bias, sink)


STAGE_ROWS, STAGE_COLS = 256, 1024
STAGE_SLOTS = 4


def _stage_weights(jobs, stage, sem):
    chunks = []
    for src, row0, col0, dst in jobs:
        rows, cols = dst.shape
        width = STAGE_COLS if cols % STAGE_COLS == 0 else STAGE_COLS // 2
        assert rows % STAGE_ROWS == 0 and cols % width == 0
        for r in range(0, rows, STAGE_ROWS):
            for c in range(0, cols, width):
                chunks.append((src.at[pl.ds(row0 + r, STAGE_ROWS), pl.ds(col0 + c, width)],
                               dst.at[pl.ds(r, STAGE_ROWS), pl.ds(c, width)], width))

    n_slots = stage.shape[0]

    def copy(k):
        src, _, width = chunks[k]
        return pltpu.make_async_copy(src, stage.at[k % n_slots, :, pl.ds(0, width)],
                                     sem.at[k % n_slots])

    for k in range(min(n_slots - 1, len(chunks))):
        copy(k).start()
    for k in range(len(chunks)):
        if k + n_slots - 1 < len(chunks):
            copy(k + n_slots - 1).start()
        copy(k).wait()
        _, dst, width = chunks[k]
        dst[...] = stage[k % n_slots, :, :width].astype(BF16)


def _mix_mlp_body(x_ref, oa_ref, ob_ref, gn_ref, bg_ref, gm_ref,
                  win_hbm, woa_hbm, wob_hbm, wout_hbm, w1_hbm, w2_hbm, o_ref,
                  wg_ref, woa_ref, wob_ref, wout_ref, w1_ref, w2_ref, stage, sem, *, layer, col0):
    @pl.when(pl.program_id(0) == 0)
    def _():
        _stage_weights(
            [(win_hbm.at[layer], 0, col0, wg_ref), (woa_hbm.at[layer], 0, 0, woa_ref),
             (wob_hbm.at[layer], 0, 0, wob_ref), (wout_hbm.at[layer], 0, 0, wout_ref),
             (w1_hbm.at[layer], 0, 0, w1_ref), (w2_hbm.at[layer], 0, 0, w2_ref)],
            stage, sem)

    tm, d = x_ref.shape
    rows = [slice(k * tm // MLP_GROUPS, (k + 1) * tm // MLP_GROUPS) for k in range(MLP_GROUPS)]
    dot = partial(jnp.dot, preferred_element_type=F32)
    x = [x_ref[r, :] for r in rows]
    h = [_rms(v, gn_ref[...]).astype(BF16) for v in x]
    g = [jax.nn.sigmoid(dot(v, wg_ref[...]) + bg_ref[...]) for v in h]
    ya = [dot(oa_ref[r, :], woa_ref[...]) for r in rows]
    yb = [dot(ob_ref[r, :], wob_ref[...]) for r in rows]
    mixed = [(gk[:, :d] * a + gk[:, d:] * b).astype(BF16) for gk, a, b in zip(g, ya, yb)]
    x = [v + dot(mk, wout_ref[...]) for v, mk in zip(x, mixed)]
    h = [_rms(v, gm_ref[...]).astype(BF16) for v in x]
    u = [jnp.square(jnp.maximum(dot(v, w1_ref[...]), 0.0)).astype(BF16) for v in h]
    for r, v, uk in zip(rows, x, u):
        o_ref[r, :] = v + dot(uk, w2_ref[...])


def _resident(arr):
    return pl.BlockSpec(arr.shape, lambda i: (0,) * arr.ndim, pipeline_mode=pl.Buffered(1))


def _mix_mlp_call(x2, oa, ob, gn, bg, gm, w_in, w_o_a, w_o_b, w_out, w_mlp1, w_mlp2,
                  layer, gate_col0, tm):
    n, d = x2.shape
    row = lambda width: pl.BlockSpec((tm, width), lambda i: (i, 0))
    consts = (gn, bg, gm)
    weights = (w_in, w_o_a, w_o_b, w_out, w_mlp1, w_mlp2)
    n_gate = w_in.shape[2] - gate_col0
    bf16_shapes = [(d, n_gate)] + [w.shape[1:] for w in weights[1:]]
    return pl.pallas_call(
        partial(_mix_mlp_body, layer=layer, col0=gate_col0),
        grid=(n // tm,),
        in_specs=([row(d), row(oa.shape[1]), row(ob.shape[1])] + [_resident(a) for a in consts]
                  + [pl.BlockSpec(memory_space=pl.ANY)] * len(weights)),
        out_specs=row(d),
        out_shape=jax.ShapeDtypeStruct((n, d), F32),
        scratch_shapes=([pltpu.VMEM(shape, BF16) for shape in bf16_shapes]
                        + [pltpu.VMEM((STAGE_SLOTS, STAGE_ROWS, STAGE_COLS), F32),
                           pltpu.SemaphoreType.DMA((STAGE_SLOTS,))]),
        compiler_params=pltpu.CompilerParams(
            dimension_semantics=("arbitrary",), vmem_limit_bytes=VMEM_LIMIT),
        name="mix_mlp",
    )(x2, oa, ob, *consts, *weights)


def _rope_tables(seq):
    rows = seq // GRID_W
    row = jnp.repeat(jnp.arange(rows, dtype=jnp.int32), GRID_W)
    col = jnp.tile(jnp.arange(GRID_W, dtype=jnp.int32), rows)
    n_freq = HEAD_DIM // 4
    inv_freq = ROPE_THETA ** (-jnp.arange(n_freq, dtype=F32) / n_freq)
    ang_row = row.astype(F32)[:, None] * inv_freq[None, :]
    ang_col = col.astype(F32)[:, None] * inv_freq[None, :]
    cr, sr, cc, sc = jnp.cos(ang_row), jnp.sin(ang_row), jnp.cos(ang_col), jnp.sin(ang_col)
    cos64 = jnp.concatenate([cr, cr, cc, cc], axis=-1)
    sin64 = jnp.concatenate([-sr, sr, -sc, sc], axis=-1)
    return jnp.tile(cos64, (1, 2)), jnp.tile(sin64, (1, 2))


def _t5_bucket(rel):
    nb = N_BUCKETS // 2
    max_exact = nb // 2
    n = jnp.abs(rel)
    large = max_exact + (jnp.log(jnp.maximum(n, 1).astype(F32) / max_exact)
                         / math.log(MAX_DISTANCE / max_exact) * (nb - max_exact)).astype(jnp.int32)
    large = jnp.minimum(large, nb - 1)
    return jnp.where(rel > 0, nb, 0) + jnp.where(n < max_exact, n, large)


def _window_bias(rel_bias):
    w = WINDOW
    period = 4 * w
    slot = jnp.arange(period, dtype=jnp.int32)
    rel = jnp.where(slot < 3 * w, slot - w, slot - 5 * w)
    table = rel_bias.astype(F32)[_t5_bucket(rel)].T * LOG2E
    table = jnp.where(jnp.abs(rel)[None] <= w, table, NEG_INF)
    band = jnp.tile(table, (1, w))[:, :w * (period - 1)].reshape(-1, w, period - 1)
    band = band[:, :, :3 * w]
    return jnp.transpose(band.reshape(N_KV, GROUP * w, 3 * w), (0, 2, 1))


def kernel(x, w_in, b_gate, qn_a, kn_a, qn_b, kn_b, w_o_a, w_o_b, w_out,
           sink_b, rel_bias, norm_mix, norm_mlp, w_mlp1, w_mlp2):
    batch, seq, d = x.shape
    depth = w_in.shape[0]
    n_qkv = 2 * (N_HEADS + 2 * N_KV) * HEAD_DIM
    cos_t, sin_t = _rope_tables(seq)
    bias = _window_bias(rel_bias)
    x2 = x.reshape(batch * seq, d)
    pair = lambda v: jnp.tile(v, 2)
    for l in range(depth):
        gains = jnp.zeros((SUBLANES, LANES), F32).at[:2].set(
            jnp.stack([pair(kn_a[l]), pair(kn_b[l])]))
        gq = jnp.concatenate([
            jnp.broadcast_to((pair(qn) * Q_SCALE)[:, None], (LANES, LANES))
            for qn in (qn_a[l], qn_b[l])], axis=0)
        qt_a, qt_b, kk, vt_a, vt_b = _qkv_call(
            x2, norm_mix[l][None, :], w_in, l, n_qkv,
            gains, gq, cos_t, sin_t, seq, tm=QKV_TILE)
        oa = _global_attn_call(qt_a, kk, vt_a, batch, seq)
        sink = jnp.repeat(sink_b[l].astype(F32) * LOG2E, WINDOW).reshape(
            N_KV, 1, GROUP * WINDOW)
        ob = _window_attn_call(qt_b, kk, vt_b, bias, sink, batch, seq)
        x2 = _mix_mlp_call(
            x2, oa, ob, norm_mix[l][None, :], b_gate[l][None, :], norm_mlp[l][None, :],
            w_in, w_o_a, w_o_b, w_out, w_mlp1, w_mlp2,
            layer=l, gate_col0=n_qkv, tm=MLP_TILE)
    return x2.reshape(batch, seq, d)
```

```python
import math
from functools import partial

import jax
import jax.numpy as jnp
from jax import lax
from jax.experimental import pallas as pl
from jax.experimental.pallas import tpu as pltpu

F32 = jnp.float32
BF16 = jnp.bfloat16

HEAD_DIM = 64
N_HEADS = 8
N_KV = 2
GROUP = N_HEADS // N_KV
GRID_W = 64
ROPE_THETA = 10000.0
WINDOW = 128
N_BUCKETS = 32
MAX_DISTANCE = 128
EPS = 1e-6
NEG_INF = -1e30

LANES = 128
SUBLANES = 8
VMEM_LIMIT = 56 * 1024 * 1024
ROT = HEAD_DIM // 4

QKV_TILE = 512
MLP_TILE = 512
MLP_GROUPS = 2

ROW_BLOCK = 128
VT_ROWS = 80
LOG2E = math.log2(math.e)
Q_SCALE = HEAD_DIM ** -0.5 * LOG2E


def _rms(x, gain):
    ms = jnp.mean(x * x, axis=-1, keepdims=True)
    return x * lax.rsqrt(ms + EPS) * gain


def _qkv_body(*refs, layer):
    *io_refs, p0, p1, w_ref, stage, sem = refs
    win_hbm = io_refs[2]
    i = pl.program_id(0)

    @pl.when(i == 0)
    def _():
        p1[...] = jnp.zeros_like(p1)
        _stage_weights([(win_hbm.at[layer], 0, 0, w_ref)], stage, sem)

    @pl.when(i % 2 == 0)
    def _():
        _qkv_step(*io_refs, w_ref, p0, p1)

    @pl.when(i % 2 == 1)
    def _():
        _qkv_step(*io_refs, w_ref, p1, p0)


def _qkv_step(x_ref, gn_ref, _, gain_ref, gq_ref, cos_ref, sin_ref, cost_ref, sint_ref,
              ones_ref, qta_ref, qtb_ref, kk_ref, vta_ref, vtb_ref, w_ref, p_new, p_ref):
    tm = x_ref.shape[0]
    lane = lax.broadcasted_iota(jnp.int32, (tm, LANES), 1)
    pair_first = (lane & ROT) == 0
    cos = cos_ref[...]
    sin = sin_ref[...]

    def chunk(c):
        return p_ref[:, c * LANES:(c + 1) * LANES]

    def head_norm_pair(ca, gain_a, cb, gain_b):
        c2 = jnp.concatenate([ca, cb], axis=1)
        sq = c2 * c2
        hi = sq.astype(BF16)
        lo = (sq - hi.astype(F32)).astype(BF16)
        ss = (jnp.dot(hi, ones_ref[...], preferred_element_type=F32)
              + jnp.dot(lo, ones_ref[...], preferred_element_type=F32))
        y = c2 * lax.rsqrt(ss * (1.0 / HEAD_DIM) + EPS)
        return y[:, :LANES] * gain_a, y[:, LANES:] * gain_b

    def rope(c):
        ahead = pltpu.roll(c, LANES - ROT, 1)
        behind = pltpu.roll(c, ROT, 1)
        return c * cos + jnp.where(pair_first, ahead, behind) * sin

    def finish_q(ref, c, pc, gain_t, rotary):
        for r in range(tm // ROW_BLOCK):
            cols = slice(r * ROW_BLOCK, (r + 1) * ROW_BLOCK)
            bt = p_ref[cols, pc * LANES:(pc + 1) * LANES].T
            halves = []
            for hh in range(LANES // HEAD_DIM):
                xh = bt[hh * HEAD_DIM:(hh + 1) * HEAD_DIM]
                ms = jnp.sum(xh * xh, axis=0, keepdims=True) * (1.0 / HEAD_DIM)
                halves.append(xh * lax.rsqrt(ms + EPS))
            y = jnp.concatenate(halves, axis=0) * gain_t
            if rotary:
                slabs = [y[t * ROT:(t + 1) * ROT] for t in range(LANES // ROT)]
                partner = jnp.concatenate(
                    [slabs[t ^ 1] for t in range(LANES // ROT)], axis=0)
                y = y * cost_ref[:, cols] + partner * sint_ref[:, cols]
            ref[r, c * LANES:(c + 1) * LANES, :] = y.astype(BF16)

    kn_a, kn_b = gain_ref[0:1, :], gain_ref[1:2, :]
    for c in range(4):
        finish_q(qta_ref, c, c, gq_ref[:LANES, :], True)
        finish_q(qtb_ref, c, 6 + c, gq_ref[LANES:, :], False)
    ka, kb = head_norm_pair(chunk(4), kn_a, chunk(10), kn_b)
    kk_ref[:, :LANES] = rope(ka).astype(BF16)
    kk_ref[:, LANES:] = kb.astype(BF16)
    pad_rows = VT_ROWS - HEAD_DIM
    ones_row = (lax.broadcasted_iota(jnp.int32, (pad_rows, tm), 0) == 0).astype(F32)
    for ref, c in ((vta_ref, 5), (vtb_ref, 11)):
        vt = chunk(c).T
        vt = jnp.concatenate(
            [vt[:HEAD_DIM], ones_row, vt[HEAD_DIM:], ones_row], axis=0).astype(BF16)
        if len(ref.shape) == 2:
            ref[...] = vt
        else:
            for r in range(tm // ROW_BLOCK):
                ref[r] = vt[:, r * ROW_BLOCK:(r + 1) * ROW_BLOCK]
    h = _rms(x_ref[...], gn_ref[...]).astype(BF16)
    p_new[...] = jnp.dot(h, w_ref[...], preferred_element_type=F32)


def _qkv_call(x2, gn, w_in, layer, n_qkv, gains, gq, cos_t, sin_t, seq, tm):
    n, d = x2.shape
    cos_tt, sin_tt = cos_t.T, sin_t.T
    nseq = seq // tm
    qd = N_HEADS * HEAD_DIM
    head_ones = jnp.kron(jnp.eye(2 * LANES // HEAD_DIM, dtype=F32),
                         jnp.ones((HEAD_DIM, HEAD_DIM), F32)).astype(BF16)
    n_tiles = n // tm
    tile_in = lambda i: jnp.minimum(i, n_tiles - 1)
    tile_out = lambda i: jnp.maximum(i - 1, 0)
    qt_spec = pl.BlockSpec((tm // ROW_BLOCK, qd, ROW_BLOCK), lambda i: (tile_out(i), 0, 0))
    qt_shape = jax.ShapeDtypeStruct((n // ROW_BLOCK, qd, ROW_BLOCK), BF16)
    return pl.pallas_call(
        partial(_qkv_body, layer=layer),
        grid=(n_tiles + 1,),
        in_specs=[
            pl.BlockSpec((tm, d), lambda i: (tile_in(i), 0)),
            pl.BlockSpec((1, d), lambda i: (0, 0)),
            pl.BlockSpec(memory_space=pl.ANY),
            pl.BlockSpec(gains.shape, lambda i: (0, 0)),
            pl.BlockSpec(gq.shape, lambda i: (0, 0)),
            pl.BlockSpec((tm, LANES), lambda i: (tile_out(i) % nseq, 0)),
            pl.BlockSpec((tm, LANES), lambda i: (tile_out(i) % nseq, 0)),
            pl.BlockSpec((LANES, tm), lambda i: (0, tile_out(i) % nseq)),
            pl.BlockSpec((LANES, tm), lambda i: (0, tile_out(i) % nseq)),
            pl.BlockSpec(head_ones.shape, lambda i: (0, 0)),
        ],
        out_specs=[qt_spec, qt_spec,
                   pl.BlockSpec((tm, 2 * LANES), lambda i: (tile_out(i), 0)),
                   pl.BlockSpec((N_KV * VT_ROWS, tm), lambda i: (0, tile_out(i))),
                   pl.BlockSpec((tm // ROW_BLOCK, N_KV * VT_ROWS, ROW_BLOCK),
                                lambda i: (tile_out(i), 0, 0))],
        out_shape=[qt_shape, qt_shape,
                   jax.ShapeDtypeStruct((n, 2 * LANES), BF16),
                   jax.ShapeDtypeStruct((N_KV * VT_ROWS, n), BF16),
                   jax.ShapeDtypeStruct((n // ROW_BLOCK, N_KV * VT_ROWS, ROW_BLOCK), BF16)],
        scratch_shapes=[pltpu.VMEM((tm, n_qkv), F32)] * 2 + [
            pltpu.VMEM((d, n_qkv), BF16),
            pltpu.VMEM((STAGE_SLOTS, STAGE_ROWS, STAGE_COLS), F32),
            pltpu.SemaphoreType.DMA((STAGE_SLOTS,))],
        compiler_params=pltpu.CompilerParams(
            dimension_semantics=("arbitrary",), vmem_limit_bytes=VMEM_LIMIT),
        name="qkv_proj",
    )(x2, gn, w_in, gains, gq, cos_t, sin_t, cos_tt, sin_tt, head_ones)


def _query_rhs(q_ref, j, g):
    qt = jnp.concatenate(
        [q_ref[j, h * HEAD_DIM:(h + 1) * HEAD_DIM, :] for h in range(GROUP)], axis=1)
    zero = jnp.zeros_like(qt)
    return jnp.concatenate([jnp.where(g == 0, qt, zero), jnp.where(g == 0, zero, qt)], axis=0)


def _store_heads_t(o_ref, row0, o):
    rb = ROW_BLOCK
    o = jnp.concatenate([o[:, k * rb:(k + 1) * rb] for k in range(GROUP)], axis=0)
    o_ref[pl.ds(row0, rb), :] = o.T.astype(o_ref.dtype)


KEY_CHUNK = 256
GLOBAL_UNROLL = 6
PV_DELAY = 1
WINDOW_SLOTS = (1, 3, 5)


def _clamp_block(j, n_blocks):
    if isinstance(j, int):
        return min(max(j, 0), n_blocks - 1)
    return jnp.clip(j, 0, n_blocks - 1)


def _block_row(j):
    row0 = j * ROW_BLOCK
    return row0 if isinstance(row0, int) else pl.multiple_of(row0, ROW_BLOCK)


def _attn_body(qa_ref, ka_ref, vta_ref, qb_ref, kb_ref, vtb_ref, bias_ref, sink_ref,
               oa_ref, ob_ref, s0, s1, m0, m1, ws0, ws1, wm0, wm1):
    s_buf, m_buf = (s0, s1), (m0, m1)
    ws_buf, wm_buf = (ws0, ws1), (wm0, wm1)
    g = pl.program_id(1)
    n_blocks = qa_ref.shape[0]
    n_chunks = ka_ref.shape[0] // KEY_CHUNK
    w = WINDOW
    sink = sink_ref[...]

    def keys(c):
        return slice(c * KEY_CHUNK, (c + 1) * KEY_CHUNK)

    def window_scores(j, slot):
        kw = jnp.concatenate(
            [kb_ref[pl.ds(_block_row(_clamp_block(j + t, n_blocks)), w), :] for t in (-1, 0, 1)],
            axis=0)
        s = jnp.dot(kw, _query_rhs(qb_ref, j, g), preferred_element_type=F32) + bias_ref[...]
        rows = [s[t * w:(t + 1) * w] for t in range(3)]
        rows[0] = jnp.where(j > 0, rows[0], NEG_INF)
        rows[2] = jnp.where(j < n_blocks - 1, rows[2], NEG_INF)
        for t in range(3):
            ws_buf[slot][t * w:(t + 1) * w, :] = rows[t]
        top = jnp.maximum(jnp.maximum(rows[0], rows[1]), rows[2])
        wm_buf[slot][...] = jnp.maximum(jnp.max(top, axis=0, keepdims=True), sink)

    def window_probs(slot):
        return jnp.exp2(ws_buf[slot][...] - wm_buf[slot][...]).astype(BF16)

    def window_output(j, slot, p):
        vband = jnp.concatenate(
            [vtb_ref[_clamp_block(j + t, n_blocks)] for t in (-1, 0, 1)], axis=1)
        o = jnp.dot(vband, p, preferred_element_type=F32)
        denom = o[HEAD_DIM:HEAD_DIM + 1] + jnp.exp2(sink - wm_buf[slot][...])
        _store_heads_t(ob_ref, _block_row(j), o[:HEAD_DIM] / denom)

    def step(j_scores, j_finish, slot):
        m = None
        o = None
        pw = None
        pending = []
        if j_scores is not None:
            rhs = _query_rhs(qa_ref, j_scores, g)

        def accumulate(o, c, p):
            oc = jnp.dot(vta_ref[:, keys(c)], p, preferred_element_type=F32)
            return oc if o is None else o + oc

        for c in range(n_chunks):
            if j_scores is not None:
                s = jnp.dot(ka_ref[keys(c), :], rhs, preferred_element_type=F32)
                s_buf[slot][keys(c), :] = s
                mc = jnp.max(s, axis=0, keepdims=True)
                m = mc if m is None else jnp.maximum(m, mc)
            if j_finish is not None:
                p = jnp.exp2(s_buf[1 - slot][keys(c), :] - m_buf[1 - slot][...]).astype(BF16)
                pending.append((c, p))
                if len(pending) > PV_DELAY:
                    o = accumulate(o, *pending.pop(0))
            if c == WINDOW_SLOTS[0] and j_scores is not None:
                window_scores(j_scores, slot)
            if c == WINDOW_SLOTS[1] and j_finish is not None:
                pw = window_probs(1 - slot)
            if c == WINDOW_SLOTS[2] and j_finish is not None:
                window_output(j_finish, 1 - slot, pw)
        for item in pending:
            o = accumulate(o, *item)
        if j_scores is not None:
            m_buf[slot][...] = m
        if j_finish is not None:
            _store_heads_t(oa_ref, _block_row(j_finish), o[:HEAD_DIM] / o[HEAD_DIM:HEAD_DIM + 1])

    step(0, None, 0)
    unroll = GLOBAL_UNROLL
    n_loop = (n_blocks - 1) // unroll

    def steady(jj, carry):
        j = 1 + unroll * jj
        for u in range(unroll):
            step(j + u, j + u - 1, (1 + u) % 2)
        return carry

    lax.fori_loop(0, n_loop, steady, 0)
    for j in range(1 + unroll * n_loop, n_blocks):
        step(j, j - 1, j % 2)
    step(None, n_blocks - 1, n_blocks % 2)


def _attn_call(qt_a, qt_b, kk, vt_a, vt_b, bias, sink, batch, seq):
    n = kk.shape[0]
    w = WINDOW
    nblk = seq // ROW_BLOCK
    cols = GROUP * ROW_BLOCK
    assert seq % KEY_CHUNK == 0 and max(WINDOW_SLOTS) < seq // KEY_CHUNK and w == ROW_BLOCK
    q_spec = pl.BlockSpec((nblk, GROUP * HEAD_DIM, ROW_BLOCK), lambda b, g: (b, g, 0))
    o_spec = pl.BlockSpec((seq, GROUP * HEAD_DIM), lambda b, g: (b, g))
    o_shape = jax.ShapeDtypeStruct((n, N_HEADS * HEAD_DIM), BF16)
    return pl.pallas_call(
        _attn_body,
        grid=(batch, N_KV),
        in_specs=[
            q_spec,
            pl.BlockSpec((seq, LANES), lambda b, g: (b, 0)),
            pl.BlockSpec((VT_ROWS, seq), lambda b, g: (g, b)),
            q_spec,
            pl.BlockSpec((seq, LANES), lambda b, g: (b, 1)),
            pl.BlockSpec((nblk, VT_ROWS, ROW_BLOCK), lambda b, g: (b, g, 0)),
            pl.BlockSpec((None, 3 * w, cols), lambda b, g: (g, 0, 0)),
            pl.BlockSpec((None, 1, cols), lambda b, g: (g, 0, 0)),
        ],
        out_specs=[o_spec, o_spec],
        out_shape=[o_shape, o_shape],
        scratch_shapes=(
            [pltpu.VMEM((seq, cols), F32)] * 2 + [pltpu.VMEM((1, cols), F32)] * 2
            + [pltpu.VMEM((3 * w, cols), F32)] * 2 + [pltpu.VMEM((1, cols), F32)] * 2),
        compiler_params=pltpu.CompilerParams(
            dimension_semantics=("arbitrary",) * 2, vmem_limit_bytes=VMEM_LIMIT),
        name="attention",
    )(qt_a, kk, vt_a, qt_b, kk, vt_b, bias, sink)


STAGE_ROWS, STAGE_COLS = 256, 1024
STAGE_SLOTS = 4


def _stage_weights(jobs, stage, sem):
    chunks = []
    for src, row0, col0, dst in jobs:
        rows, cols = dst.shape
        width = STAGE_COLS if cols % STAGE_COLS == 0 else STAGE_COLS // 2
        assert rows % STAGE_ROWS == 0 and cols % width == 0
        for r in range(0, rows, STAGE_ROWS):
            for c in range(0, cols, width):
                chunks.append((src.at[pl.ds(row0 + r, STAGE_ROWS), pl.ds(col0 + c, width)],
                               dst.at[pl.ds(r, STAGE_ROWS), pl.ds(c, width)], width))

    n_slots = stage.shape[0]

    def copy(k):
        src, _, width = chunks[k]
        return pltpu.make_async_copy(src, stage.at[k % n_slots, :, pl.ds(0, width)],
                                     sem.at[k % n_slots])

    for k in range(min(n_slots - 1, len(chunks))):
        copy(k).start()
    for k in range(len(chunks)):
        if k + n_slots - 1 < len(chunks):
            copy(k + n_slots - 1).start()
        copy(k).wait()
        _, dst, width = chunks[k]
        dst[...] = stage[k % n_slots, :, :width].astype(BF16)


def _mix_mlp_body(x_ref, oa_ref, ob_ref, gn_ref, bg_ref, gm_ref,
                  win_hbm, woa_hbm, wob_hbm, wout_hbm, w1_hbm, w2_hbm, o_ref,
                  wg_ref, woa_ref, wob_ref, wout_ref, w1_ref, w2_ref, stage, sem, *, layer, col0):
    @pl.when(pl.program_id(0) == 0)
    def _():
        _stage_weights(
            [(win_hbm.at[layer], 0, col0, wg_ref), (woa_hbm.at[layer], 0, 0, woa_ref),
             (wob_hbm.at[layer], 0, 0, wob_ref), (wout_hbm.at[layer], 0, 0, wout_ref),
             (w1_hbm.at[layer], 0, 0, w1_ref), (w2_hbm.at[layer], 0, 0, w2_ref)],
            stage, sem)

    tm, d = x_ref.shape
    rows = [slice(k * tm // MLP_GROUPS, (k + 1) * tm // MLP_GROUPS) for k in range(MLP_GROUPS)]
    dot = partial(jnp.dot, preferred_element_type=F32)
    x = [x_ref[r, :] for r in rows]
    h = [_rms(v, gn_ref[...]).astype(BF16) for v in x]
    g = [jax.nn.sigmoid(dot(v, wg_ref[...]) + bg_ref[...]) for v in h]
    ya = [dot(oa_ref[r, :], woa_ref[...]) for r in rows]
    yb = [dot(ob_ref[r, :], wob_ref[...]) for r in rows]
    mixed = [(gk[:, :d] * a + gk[:, d:] * b).astype(BF16) for gk, a, b in zip(g, ya, yb)]
    x = [v + dot(mk, wout_ref[...]) for v, mk in zip(x, mixed)]
    h = [_rms(v, gm_ref[...]).astype(BF16) for v in x]
    u = [jnp.square(jnp.maximum(dot(v, w1_ref[...]), 0.0)).astype(BF16) for v in h]
    for r, v, uk in zip(rows, x, u):
        o_ref[r, :] = v + dot(uk, w2_ref[...])


def _resident(arr):
    return pl.BlockSpec(arr.shape, lambda i: (0,) * arr.ndim, pipeline_mode=pl.Buffered(1))


def _mix_mlp_call(x2, oa, ob, gn, bg, gm, w_in, w_o_a, w_o_b, w_out, w_mlp1, w_mlp2,
                  layer, gate_col0, tm):
    n, d = x2.shape
    row = lambda width: pl.BlockSpec((tm, width), lambda i: (i, 0))
    consts = (gn, bg, gm)
    weights = (w_in, w_o_a, w_o_b, w_out, w_mlp1, w_mlp2)
    n_gate = w_in.shape[2] - gate_col0
    bf16_shapes = [(d, n_gate)] + [w.shape[1:] for w in weights[1:]]
    return pl.pallas_call(
        partial(_mix_mlp_body, layer=layer, col0=gate_col0),
        grid=(n // tm,),
        in_specs=([row(d), row(oa.shape[1]), row(ob.shape[1])] + [_resident(a) for a in consts]
                  + [pl.BlockSpec(memory_space=pl.ANY)] * len(weights)),
        out_specs=row(d),
        out_shape=jax.ShapeDtypeStruct((n, d), F32),
        scratch_shapes=([pltpu.VMEM(shape, BF16) for shape in bf16_shapes]
                        + [pltpu.VMEM((STAGE_SLOTS, STAGE_ROWS, STAGE_COLS), F32),
                           pltpu.SemaphoreType.DMA((STAGE_SLOTS,))]),
        compiler_params=pltpu.CompilerParams(
            dimension_semantics=("arbitrary",), vmem_limit_bytes=VMEM_LIMIT),
        name="mix_mlp",
    )(x2, oa, ob, *consts, *weights)


def _rope_tables(seq):
    rows = seq // GRID_W
    row = jnp.repeat(jnp.arange(rows, dtype=jnp.int32), GRID_W)
    col = jnp.tile(jnp.arange(GRID_W, dtype=jnp.int32), rows)
    n_freq = HEAD_DIM // 4
    inv_freq = ROPE_THETA ** (-jnp.arange(n_freq, dtype=F32) / n_freq)
    ang_row = row.astype(F32)[:, None] * inv_freq[None, :]
    ang_col = col.astype(F32)[:, None] * inv_freq[None, :]
    cr, sr, cc, sc = jnp.cos(ang_row), jnp.sin(ang_row), jnp.cos(ang_col), jnp.sin(ang_col)
    cos64 = jnp.concatenate([cr, cr, cc, cc], axis=-1)
    sin64 = jnp.concatenate([-sr, sr, -sc, sc], axis=-1)
    return jnp.tile(cos64, (1, 2)), jnp.tile(sin64, (1, 2))


def _t5_bucket(rel):
    nb = N_BUCKETS // 2
    max_exact = nb // 2
    n = jnp.abs(rel)
    large = max_exact + (jnp.log(jnp.maximum(n, 1).astype(F32) / max_exact)
                         / math.log(MAX_DISTANCE / max_exact) * (nb - max_exact)).astype(jnp.int32)
    large = jnp.minimum(large, nb - 1)
    return jnp.where(rel > 0, nb, 0) + jnp.where(n < max_exact, n, large)


def _window_bias(rel_bias):
    w = WINDOW
    period = 4 * w
    slot = jnp.arange(period, dtype=jnp.int32)
    rel = jnp.where(slot < 3 * w, slot - w, slot - 5 * w)
    table = rel_bias.astype(F32)[_t5_bucket(rel)].T * LOG2E
    table = jnp.where(jnp.abs(rel)[None] <= w, table, NEG_INF)
    band = jnp.tile(table, (1, w))[:, :w * (period - 1)].reshape(-1, w, period - 1)
    band = band[:, :, :3 * w]
    return jnp.transpose(band.reshape(N_KV, GROUP * w, 3 * w), (0, 2, 1))


def kernel(x, w_in, b_gate, qn_a, kn_a, qn_b, kn_b, w_o_a, w_o_b, w_out,
           sink_b, rel_bias, norm_mix, norm_mlp, w_mlp1, w_mlp2):
    batch, seq, d = x.shape
    depth = w_in.shape[0]
    n_qkv = 2 * (N_HEADS + 2 * N_KV) * HEAD_DIM
    cos_t, sin_t = _rope_tables(seq)
    bias = _window_bias(rel_bias)
    x2 = x.reshape(batch * seq, d)
    pair = lambda v: jnp.tile(v, 2)
    for l in range(depth):
        gains = jnp.zeros((SUBLANES, LANES), F32).at[:2].set(
            jnp.stack([pair(kn_a[l]), pair(kn_b[l])]))
        gq = jnp.concatenate([
            jnp.broadcast_to((pair(qn) * Q_SCALE)[:, None], (LANES, LANES))
            for qn in (qn_a[l], qn_b[l])], axis=0)
        qt_a, qt_b, kk, vt_a, vt_b = _qkv_call(
            x2, norm_mix[l][None, :], w_in, l, n_qkv,
            gains, gq, cos_t, sin_t, seq, tm=QKV_TILE)
        sink = jnp.repeat(sink_b[l].astype(F32) * LOG2E, WINDOW).reshape(
            N_KV, 1, GROUP * WINDOW)
        oa, ob = _attn_call(qt_a, qt_b, kk, vt_a, vt_b, bias, sink, batch, seq)
        x2 = _mix_mlp_call(
            x2, oa, ob, norm_mix[l][None, :], b_gate[l][None, :], norm_mlp[l][None, :],
            w_in, w_o_a, w_o_b, w_out, w_mlp1, w_mlp2,
            layer=l, gate_col0=n_qkv, tm=MLP_TILE)
    return x2.reshape(batch, seq, d)
```

```python
import math
from functools import partial

import jax
import jax.numpy as jnp
from jax import lax
from jax.experimental import pallas as pl
from jax.experimental.pallas import tpu as pltpu

F32 = jnp.float32
BF16 = jnp.bfloat16

HEAD_DIM = 64
N_HEADS = 8
N_KV = 2
GROUP = N_HEADS // N_KV
GRID_W = 64
ROPE_THETA = 10000.0
WINDOW = 128
N_BUCKETS = 32
MAX_DISTANCE = 128
EPS = 1e-6
NEG_INF = -1e30

LANES = 128
SUBLANES = 8
VMEM_LIMIT = 56 * 1024 * 1024
ROT = HEAD_DIM // 4

QKV_TILE = 1024
MLP_TILE = 512
MLP_GROUPS = 2

ROW_BLOCK = 128
VT_ROWS = 80
LOG2E = math.log2(math.e)
Q_SCALE = HEAD_DIM ** -0.5 * LOG2E


def _rms(x, gain):
    ms = jnp.mean(x * x, axis=-1, keepdims=True)
    return x * lax.rsqrt(ms + EPS) * gain


def _qkv_body(*refs, layer):
    *io_refs, p0, p1, w_ref, stage, sem = refs
    win_hbm = io_refs[2]
    i = pl.program_id(0)

    @pl.when(i == 0)
    def _():
        p1[...] = jnp.zeros_like(p1)
        _stage_weights([(win_hbm.at[layer], 0, 0, w_ref)], stage, sem)

    @pl.when(i % 2 == 0)
    def _():
        _qkv_step(*io_refs, w_ref, p0, p1)

    @pl.when(i % 2 == 1)
    def _():
        _qkv_step(*io_refs, w_ref, p1, p0)


def _qkv_step(x_ref, gn_ref, _, gain_ref, gq_ref, cos_ref, sin_ref, cost_ref, sint_ref,
              ones_ref, qta_ref, qtb_ref, kk_ref, vta_ref, vtb_ref, w_ref, p_new, p_ref):
    tm = x_ref.shape[0]
    lane = lax.broadcasted_iota(jnp.int32, (tm, LANES), 1)
    pair_first = (lane & ROT) == 0
    cos = cos_ref[...]
    sin = sin_ref[...]

    def chunk(c):
        return p_ref[:, c * LANES:(c + 1) * LANES]

    def head_norm_pair(ca, gain_a, cb, gain_b):
        c2 = jnp.concatenate([ca, cb], axis=1)
        sq = c2 * c2
        hi = sq.astype(BF16)
        lo = (sq - hi.astype(F32)).astype(BF16)
        ss = (jnp.dot(hi, ones_ref[...], preferred_element_type=F32)
              + jnp.dot(lo, ones_ref[...], preferred_element_type=F32))
        y = c2 * lax.rsqrt(ss * (1.0 / HEAD_DIM) + EPS)
        return y[:, :LANES] * gain_a, y[:, LANES:] * gain_b

    def rope(c):
        ahead = pltpu.roll(c, LANES - ROT, 1)
        behind = pltpu.roll(c, ROT, 1)
        return c * cos + jnp.where(pair_first, ahead, behind) * sin

    def finish_q(ref, c, pc, gain_t, rotary):
        for r in range(tm // ROW_BLOCK):
            cols = slice(r * ROW_BLOCK, (r + 1) * ROW_BLOCK)
            bt = p_ref[cols, pc * LANES:(pc + 1) * LANES].T
            halves = []
            for hh in range(LANES // HEAD_DIM):
                xh = bt[hh * HEAD_DIM:(hh + 1) * HEAD_DIM]
                ms = jnp.sum(xh * xh, axis=0, keepdims=True) * (1.0 / HEAD_DIM)
                halves.append(xh * lax.rsqrt(ms + EPS))
            y = jnp.concatenate(halves, axis=0) * gain_t
            if rotary:
                slabs = [y[t * ROT:(t + 1) * ROT] for t in range(LANES // ROT)]
                partner = jnp.concatenate(
                    [slabs[t ^ 1] for t in range(LANES // ROT)], axis=0)
                y = y * cost_ref[:, cols] + partner * sint_ref[:, cols]
            ref[r, c * LANES:(c + 1) * LANES, :] = y.astype(BF16)

    kn_a, kn_b = gain_ref[0:1, :], gain_ref[1:2, :]
    for c in range(4):
        finish_q(qta_ref, c, c, gq_ref[:LANES, :], True)
        finish_q(qtb_ref, c, 6 + c, gq_ref[LANES:, :], False)
    ka, kb = head_norm_pair(chunk(4), kn_a, chunk(10), kn_b)
    kk_ref[:, :LANES] = rope(ka).astype(BF16)
    kk_ref[:, LANES:] = kb.astype(BF16)
    pad_rows = VT_ROWS - HEAD_DIM
    ones_row = (lax.broadcasted_iota(jnp.int32, (pad_rows, tm), 0) == 0).astype(F32)
    for ref, c in ((vta_ref, 5), (vtb_ref, 11)):
        vt = chunk(c).T
        vt = jnp.concatenate(
            [vt[:HEAD_DIM], ones_row, vt[HEAD_DIM:], ones_row], axis=0).astype(BF16)
        if len(ref.shape) == 2:
            ref[...] = vt
        else:
            for r in range(tm // ROW_BLOCK):
                ref[r] = vt[:, r * ROW_BLOCK:(r + 1) * ROW_BLOCK]
    h = _rms(x_ref[...], gn_ref[...]).astype(BF16)
    p_new[...] = jnp.dot(h, w_ref[...], preferred_element_type=F32)


def _qkv_call(x2, gn, w_in, layer, n_qkv, gains, gq, cos_t, sin_t, seq, tm):
    n, d = x2.shape
    cos_tt, sin_tt = cos_t.T, sin_t.T
    nseq = seq // tm
    qd = N_HEADS * HEAD_DIM
    head_ones = jnp.kron(jnp.eye(2 * LANES // HEAD_DIM, dtype=F32),
                         jnp.ones((HEAD_DIM, HEAD_DIM), F32)).astype(BF16)
    n_tiles = n // tm
    tile_in = lambda i: jnp.minimum(i, n_tiles - 1)
    tile_out = lambda i: jnp.maximum(i - 1, 0)
    qt_spec = pl.BlockSpec((tm // ROW_BLOCK, qd, ROW_BLOCK), lambda i: (tile_out(i), 0, 0))
    qt_shape = jax.ShapeDtypeStruct((n // ROW_BLOCK, qd, ROW_BLOCK), BF16)
    return pl.pallas_call(
        partial(_qkv_body, layer=layer),
        grid=(n_tiles + 1,),
        in_specs=[
            pl.BlockSpec((tm, d), lambda i: (tile_in(i), 0)),
            pl.BlockSpec((1, d), lambda i: (0, 0)),
            pl.BlockSpec(memory_space=pl.ANY),
            pl.BlockSpec(gains.shape, lambda i: (0, 0)),
            pl.BlockSpec(gq.shape, lambda i: (0, 0)),
            pl.BlockSpec((tm, LANES), lambda i: (tile_out(i) % nseq, 0)),
            pl.BlockSpec((tm, LANES), lambda i: (tile_out(i) % nseq, 0)),
            pl.BlockSpec((LANES, tm), lambda i: (0, tile_out(i) % nseq)),
            pl.BlockSpec((LANES, tm), lambda i: (0, tile_out(i) % nseq)),
            pl.BlockSpec(head_ones.shape, lambda i: (0, 0)),
        ],
        out_specs=[qt_spec, qt_spec,
                   pl.BlockSpec((tm, 2 * LANES), lambda i: (tile_out(i), 0)),
                   pl.BlockSpec((N_KV * VT_ROWS, tm), lambda i: (0, tile_out(i))),
                   pl.BlockSpec((tm // ROW_BLOCK, N_KV * VT_ROWS, ROW_BLOCK),
                                lambda i: (tile_out(i), 0, 0))],
        out_shape=[qt_shape, qt_shape,
                   jax.ShapeDtypeStruct((n, 2 * LANES), BF16),
                   jax.ShapeDtypeStruct((N_KV * VT_ROWS, n), BF16),
                   jax.ShapeDtypeStruct((n // ROW_BLOCK, N_KV * VT_ROWS, ROW_BLOCK), BF16)],
        scratch_shapes=[pltpu.VMEM((tm, n_qkv), F32)] * 2 + [
            pltpu.VMEM((d, n_qkv), BF16),
            pltpu.VMEM((STAGE_SLOTS, STAGE_ROWS, STAGE_COLS), F32),
            pltpu.SemaphoreType.DMA((STAGE_SLOTS,))],
        compiler_params=pltpu.CompilerParams(
            dimension_semantics=("arbitrary",), vmem_limit_bytes=VMEM_LIMIT),
        name="qkv_proj",
    )(x2, gn, w_in, gains, gq, cos_t, sin_t, cos_tt, sin_tt, head_ones)


def _query_rhs(q_ref, j, g):
    qt = jnp.concatenate(
        [q_ref[j, h * HEAD_DIM:(h + 1) * HEAD_DIM, :] for h in range(GROUP)], axis=1)
    zero = jnp.zeros_like(qt)
    return jnp.concatenate([jnp.where(g == 0, qt, zero), jnp.where(g == 0, zero, qt)], axis=0)


def _store_heads_t(o_ref, row0, o):
    rb = ROW_BLOCK
    o = jnp.concatenate([o[:, k * rb:(k + 1) * rb] for k in range(GROUP)], axis=0)
    o_ref[pl.ds(row0, rb), :] = o.T.astype(o_ref.dtype)


KEY_CHUNK = 256
GLOBAL_UNROLL = 6
PV_DELAY = 1
WINDOW_SLOTS = (0, 2, 4)


def _clamp_block(j, n_blocks):
    if isinstance(j, int):
        return min(max(j, 0), n_blocks - 1)
    return jnp.clip(j, 0, n_blocks - 1)


def _block_row(j):
    row0 = j * ROW_BLOCK
    return row0 if isinstance(row0, int) else pl.multiple_of(row0, ROW_BLOCK)


def _attn_body(qa_ref, ka_ref, vta_ref, qb_ref, kb_ref, vtb_ref, bias_ref, sink_ref,
               oa_ref, ob_ref, s0, s1, m0, m1, ws0, ws1, wm0, wm1):
    s_buf, m_buf = (s0, s1), (m0, m1)
    ws_buf, wm_buf = (ws0, ws1), (wm0, wm1)
    g = pl.program_id(1)
    n_blocks = qa_ref.shape[0]
    n_chunks = ka_ref.shape[0] // KEY_CHUNK
    w = WINDOW
    sink = sink_ref[...]

    def keys(c):
        return slice(c * KEY_CHUNK, (c + 1) * KEY_CHUNK)

    def window_scores(j, slot):
        kw = jnp.concatenate(
            [kb_ref[pl.ds(_block_row(_clamp_block(j + t, n_blocks)), w), :] for t in (-1, 0, 1)],
            axis=0)
        s = jnp.dot(kw, _query_rhs(qb_ref, j, g), preferred_element_type=F32) + bias_ref[...]
        rows = [s[t * w:(t + 1) * w] for t in range(3)]
        rows[0] = jnp.where(j > 0, rows[0], NEG_INF)
        rows[2] = jnp.where(j < n_blocks - 1, rows[2], NEG_INF)
        for t in range(3):
            ws_buf[slot][t * w:(t + 1) * w, :] = rows[t]
        top = jnp.maximum(jnp.maximum(rows[0], rows[1]), rows[2])
        wm_buf[slot][...] = jnp.maximum(jnp.max(top, axis=0, keepdims=True), sink)

    def window_probs(slot):
        return jnp.exp2(ws_buf[slot][...] - wm_buf[slot][...]).astype(BF16)

    def window_output(j, slot, p):
        vband = jnp.concatenate(
            [vtb_ref[_clamp_block(j + t, n_blocks)] for t in (-1, 0, 1)], axis=1)
        o = jnp.dot(vband, p, preferred_element_type=F32)
        denom = o[HEAD_DIM:HEAD_DIM + 1] + jnp.exp2(sink - wm_buf[slot][...])
        _store_heads_t(ob_ref, _block_row(j), o[:HEAD_DIM] / denom)

    def step(j_scores, j_finish, slot):
        m = None
        o = None
        pw = None
        pending = []
        if j_scores is not None:
            rhs = _query_rhs(qa_ref, j_scores, g)

        def accumulate(o, c, p):
            oc = jnp.dot(vta_ref[:, keys(c)], p, preferred_element_type=F32)
            return oc if o is None else o + oc

        for c in range(n_chunks):
            if j_scores is not None:
                s = jnp.dot(ka_ref[keys(c), :], rhs, preferred_element_type=F32)
                s_buf[slot][keys(c), :] = s
                mc = jnp.max(s, axis=0, keepdims=True)
                m = mc if m is None else jnp.maximum(m, mc)
            if j_finish is not None:
                p = jnp.exp2(s_buf[1 - slot][keys(c), :] - m_buf[1 - slot][...]).astype(BF16)
                pending.append((c, p))
                if len(pending) > PV_DELAY:
                    o = accumulate(o, *pending.pop(0))
            if c == WINDOW_SLOTS[0] and j_scores is not None:
                window_scores(j_scores, slot)
            if c == WINDOW_SLOTS[1] and j_finish is not None:
                pw = window_probs(1 - slot)
            if c == WINDOW_SLOTS[2] and j_finish is not None:
                window_output(j_finish, 1 - slot, pw)
        for item in pending:
            o = accumulate(o, *item)
        if j_scores is not None:
            m_buf[slot][...] = m
        if j_finish is not None:
            _store_heads_t(oa_ref, _block_row(j_finish), o[:HEAD_DIM] / o[HEAD_DIM:HEAD_DIM + 1])

    step(0, None, 0)
    unroll = GLOBAL_UNROLL
    n_loop = (n_blocks - 1) // unroll

    def steady(jj, carry):
        j = 1 + unroll * jj
        for u in range(unroll):
            step(j + u, j + u - 1, (1 + u) % 2)
        return carry

    lax.fori_loop(0, n_loop, steady, 0)
    for j in range(1 + unroll * n_loop, n_blocks):
        step(j, j - 1, j % 2)
    step(None, n_blocks - 1, n_blocks % 2)


def _attn_call(qt_a, qt_b, kk, vt_a, vt_b, bias, sink, batch, seq):
    n = kk.shape[0]
    w = WINDOW
    nblk = seq // ROW_BLOCK
    cols = GROUP * ROW_BLOCK
    assert seq % KEY_CHUNK == 0 and max(WINDOW_SLOTS) < seq // KEY_CHUNK and w == ROW_BLOCK
    q_spec = pl.BlockSpec((nblk, GROUP * HEAD_DIM, ROW_BLOCK), lambda b, g: (b, g, 0))
    o_spec = pl.BlockSpec((seq, GROUP * HEAD_DIM), lambda b, g: (b, g))
    o_shape = jax.ShapeDtypeStruct((n, N_HEADS * HEAD_DIM), BF16)
    return pl.pallas_call(
        _attn_body,
        grid=(batch, N_KV),
        in_specs=[
            q_spec,
            pl.BlockSpec((seq, LANES), lambda b, g: (b, 0)),
            pl.BlockSpec((VT_ROWS, seq), lambda b, g: (g, b)),
            q_spec,
            pl.BlockSpec((seq, LANES), lambda b, g: (b, 1)),
            pl.BlockSpec((nblk, VT_ROWS, ROW_BLOCK), lambda b, g: (b, g, 0)),
            pl.BlockSpec((None, 3 * w, cols), lambda b, g: (g, 0, 0)),
            pl.BlockSpec((None, 1, cols), lambda b, g: (g, 0, 0)),
        ],
        out_specs=[o_spec, o_spec],
        out_shape=[o_shape, o_shape],
        scratch_shapes=(
            [pltpu.VMEM((seq, cols), F32)] * 2 + [pltpu.VMEM((1, cols), F32)] * 2
            + [pltpu.VMEM((3 * w, cols), F32)] * 2 + [pltpu.VMEM((1, cols), F32)] * 2),
        compiler_params=pltpu.CompilerParams(
            dimension_semantics=("arbitrary",) * 2, vmem_limit_bytes=VMEM_LIMIT),
        name="attention",
    )(qt_a, kk, vt_a, qt_b, kk, vt_b, bias, sink)


STAGE_ROWS, STAGE_COLS = 256, 1024
STAGE_SLOTS = 4


def _stage_weights(jobs, stage, sem):
    chunks = []
    for src, row0, col0, dst in jobs:
        rows, cols = dst.shape
        width = STAGE_COLS if cols % STAGE_COLS == 0 else STAGE_COLS // 2
        assert rows % STAGE_ROWS == 0 and cols % width == 0
        for r in range(0, rows, STAGE_ROWS):
            for c in range(0, cols, width):
                chunks.append((src.at[pl.ds(row0 + r, STAGE_ROWS), pl.ds(col0 + c, width)],
                               dst.at[pl.ds(r, STAGE_ROWS), pl.ds(c, width)], width))

    n_slots = stage.shape[0]

    def copy(k):
        src, _, width = chunks[k]
        return pltpu.make_async_copy(src, stage.at[k % n_slots, :, pl.ds(0, width)],
                                     sem.at[k % n_slots])

    for k in range(min(n_slots - 1, len(chunks))):
        copy(k).start()
    for k in range(len(chunks)):
        if k + n_slots - 1 < len(chunks):
            copy(k + n_slots - 1).start()
        copy(k).wait()
        _, dst, width = chunks[k]
        dst[...] = stage[k % n_slots, :, :width].astype(BF16)


def _mix_mlp_body(x_ref, oa_ref, ob_ref, gn_ref, bg_ref, gm_ref,
                  win_hbm, woa_hbm, wob_hbm, wout_hbm, w1_hbm, w2_hbm, o_ref,
                  wg_ref, woa_ref, wob_ref, wout_ref, w1_ref, w2_ref, stage, sem, *, layer, col0):
    @pl.when(pl.program_id(0) == 0)
    def _():
        _stage_weights(
            [(win_hbm.at[layer], 0, col0, wg_ref), (woa_hbm.at[layer], 0, 0, woa_ref),
             (wob_hbm.at[layer], 0, 0, wob_ref), (wout_hbm.at[layer], 0, 0, wout_ref),
             (w1_hbm.at[layer], 0, 0, w1_ref), (w2_hbm.at[layer], 0, 0, w2_ref)],
            stage, sem)

    tm, d = x_ref.shape
    rows = [slice(k * tm // MLP_GROUPS, (k + 1) * tm // MLP_GROUPS) for k in range(MLP_GROUPS)]
    dot = partial(jnp.dot, preferred_element_type=F32)
    x = [x_ref[r, :] for r in rows]
    h = [_rms(v, gn_ref[...]).astype(BF16) for v in x]
    g = [jax.nn.sigmoid(dot(v, wg_ref[...]) + bg_ref[...]) for v in h]
    ya = [dot(oa_ref[r, :], woa_ref[...]) for r in rows]
    yb = [dot(ob_ref[r, :], wob_ref[...]) for r in rows]
    mixed = [(gk[:, :d] * a + gk[:, d:] * b).astype(BF16) for gk, a, b in zip(g, ya, yb)]
    x = [v + dot(mk, wout_ref[...]) for v, mk in zip(x, mixed)]
    h = [_rms(v, gm_ref[...]).astype(BF16) for v in x]
    u = [jnp.square(jnp.maximum(dot(v, w1_ref[...]), 0.0)).astype(BF16) for v in h]
    for r, v, uk in zip(rows, x, u):
        o_ref[r, :] = v + dot(uk, w2_ref[...])


def _resident(arr):
    return pl.BlockSpec(arr.shape, lambda i: (0,) * arr.ndim, pipeline_mode=pl.Buffered(1))


def _mix_mlp_call(x2, oa, ob, gn, bg, gm, w_in, w_o_a, w_o_b, w_out, w_mlp1, w_mlp2,
                  layer, gate_col0, tm):
    n, d = x2.shape
    row = lambda width: pl.BlockSpec((tm, width), lambda i: (i, 0))
    consts = (gn, bg, gm)
    weights = (w_in, w_o_a, w_o_b, w_out, w_mlp1, w_mlp2)
    n_gate = w_in.shape[2] - gate_col0
    bf16_shapes = [(d, n_gate)] + [w.shape[1:] for w in weights[1:]]
    return pl.pallas_call(
        partial(_mix_mlp_body, layer=layer, col0=gate_col0),
        grid=(n // tm,),
        in_specs=([row(d), row(oa.shape[1]), row(ob.shape[1])] + [_resident(a) for a in consts]
                  + [pl.BlockSpec(memory_space=pl.ANY)] * len(weights)),
        out_specs=row(d),
        out_shape=jax.ShapeDtypeStruct((n, d), F32),
        scratch_shapes=([pltpu.VMEM(shape, BF16) for shape in bf16_shapes]
                        + [pltpu.VMEM((STAGE_SLOTS, STAGE_ROWS, STAGE_COLS), F32),
                           pltpu.SemaphoreType.DMA((STAGE_SLOTS,))]),
        compiler_params=pltpu.CompilerParams(
            dimension_semantics=("arbitrary",), vmem_limit_bytes=VMEM_LIMIT),
        name="mix_mlp",
    )(x2, oa, ob, *consts, *weights)


def _rope_tables(seq):
    rows = seq // GRID_W
    row = jnp.repeat(jnp.arange(rows, dtype=jnp.int32), GRID_W)
    col = jnp.tile(jnp.arange(GRID_W, dtype=jnp.int32), rows)
    n_freq = HEAD_DIM // 4
    inv_freq = ROPE_THETA ** (-jnp.arange(n_freq, dtype=F32) / n_freq)
    ang_row = row.astype(F32)[:, None] * inv_freq[None, :]
    ang_col = col.astype(F32)[:, None] * inv_freq[None, :]
    cr, sr, cc, sc = jnp.cos(ang_row), jnp.sin(ang_row), jnp.cos(ang_col), jnp.sin(ang_col)
    cos64 = jnp.concatenate([cr, cr, cc, cc], axis=-1)
    sin64 = jnp.concatenate([-sr, sr, -sc, sc], axis=-1)
    return jnp.tile(cos64, (1, 2)), jnp.tile(sin64, (1, 2))


def _t5_bucket(rel):
    nb = N_BUCKETS // 2
    max_exact = nb // 2
    n = jnp.abs(rel)
    large = max_exact + (jnp.log(jnp.maximum(n, 1).astype(F32) / max_exact)
                         / math.log(MAX_DISTANCE / max_exact) * (nb - max_exact)).astype(jnp.int32)
    large = jnp.minimum(large, nb - 1)
    return jnp.where(rel > 0, nb, 0) + jnp.where(n < max_exact, n, large)


def _window_bias(rel_bias):
    w = WINDOW
    period = 4 * w
    slot = jnp.arange(period, dtype=jnp.int32)
    rel = jnp.where(slot < 3 * w, slot - w, slot - 5 * w)
    table = rel_bias.astype(F32)[_t5_bucket(rel)].T * LOG2E
    table = jnp.where(jnp.abs(rel)[None] <= w, table, NEG_INF)
    band = jnp.tile(table, (1, w))[:, :w * (period - 1)].reshape(-1, w, period - 1)
    band = band[:, :, :3 * w]
    return jnp.transpose(band.reshape(N_KV, GROUP * w, 3 * w), (0, 2, 1))


def kernel(x, w_in, b_gate, qn_a, kn_a, qn_b, kn_b, w_o_a, w_o_b, w_out,
           sink_b, rel_bias, norm_mix, norm_mlp, w_mlp1, w_mlp2):
    batch, seq, d = x.shape
    depth = w_in.shape[0]
    n_qkv = 2 * (N_HEADS + 2 * N_KV) * HEAD_DIM
    cos_t, sin_t = _rope_tables(seq)
    bias = _window_bias(rel_bias)
    x2 = x.reshape(batch * seq, d)
    pair = lambda v: jnp.tile(v, 2)
    for l in range(depth):
        gains = jnp.zeros((SUBLANES, LANES), F32).at[:2].set(
            jnp.stack([pair(kn_a[l]), pair(kn_b[l])]))
        gq = jnp.concatenate([
            jnp.broadcast_to((pair(qn) * Q_SCALE)[:, None], (LANES, LANES))
            for qn in (qn_a[l], qn_b[l])], axis=0)
        qt_a, qt_b, kk, vt_a, vt_b = _qkv_call(
            x2, norm_mix[l][None, :], w_in, l, n_qkv,
            gains, gq, cos_t, sin_t, seq, tm=QKV_TILE)
        sink = jnp.repeat(sink_b[l].astype(F32) * LOG2E, WINDOW).reshape(
            N_KV, 1, GROUP * WINDOW)
        oa, ob = _attn_call(qt_a, qt_b, kk, vt_a, vt_b, bias, sink, batch, seq)
        x2 = _mix_mlp_call(
            x2, oa, ob, norm_mix[l][None, :], b_gate[l][None, :], norm_mlp[l][None, :],
            w_in, w_o_a, w_o_b, w_out, w_mlp1, w_mlp2,
            layer=l, gate_col0=n_qkv, tm=MLP_TILE)
    return x2.reshape(batch, seq, d)
```

```python
import math
from functools import partial

import jax
import jax.numpy as jnp
from jax import lax
from jax.experimental import pallas as pl
from jax.experimental.pallas import tpu as pltpu

F32 = jnp.float32
BF16 = jnp.bfloat16

HEAD_DIM = 64
N_HEADS = 8
N_KV = 2
GROUP = N_HEADS // N_KV
GRID_W = 64
ROPE_THETA = 10000.0
WINDOW = 128
N_BUCKETS = 32
MAX_DISTANCE = 128
EPS = 1e-6
NEG_INF = -1e30

LANES = 128
SUBLANES = 8
VMEM_LIMIT = 56 * 1024 * 1024
ROT = HEAD_DIM // 4

QKV_TILE = 1024
MLP_TILE = 512
MLP_GROUPS = 2

ROW_BLOCK = 128
VT_ROWS = 80
LOG2E = math.log2(math.e)
Q_SCALE = HEAD_DIM ** -0.5 * LOG2E


def _rms(x, gain):
    ms = jnp.mean(x * x, axis=-1, keepdims=True)
    return x * lax.rsqrt(ms + EPS) * gain


def _qkv_body(*refs, layer, n_tiles):
    *io_refs, p0, p1, w_ref, stage, sem = refs
    win_hbm = io_refs[2]
    i = pl.program_id(0)
    slots = (p0, p1)

    @pl.when(i == 0)
    def _():
        p1[...] = jnp.zeros_like(p1)
        _stage_weights([(win_hbm.at[layer], 0, 0, w_ref)], stage, sem)

    for parity in range(2):
        @pl.when(jnp.logical_and(i % 2 == parity, i < n_tiles))
        def _():
            _qkv_step(*io_refs, w_ref, slots[parity], slots[1 - parity], project=True)

    @pl.when(i == n_tiles)
    def _():
        _qkv_step(*io_refs, w_ref, None, slots[1 - n_tiles % 2], project=False)


def _qkv_step(x_ref, gn_ref, _, gain_ref, gq_ref, cos_ref, sin_ref, cost_ref, sint_ref,
              ones_ref, qta_ref, qtb_ref, kk_ref, vta_ref, vtb_ref, w_ref, p_new, p_ref, *,
              project):
    tm = x_ref.shape[0]
    lane = lax.broadcasted_iota(jnp.int32, (tm, LANES), 1)
    pair_first = (lane & ROT) == 0
    cos = cos_ref[...]
    sin = sin_ref[...]

    def chunk(c):
        return p_ref[:, c * LANES:(c + 1) * LANES]

    def head_norm_pair(ca, gain_a, cb, gain_b):
        c2 = jnp.concatenate([ca, cb], axis=1)
        sq = c2 * c2
        hi = sq.astype(BF16)
        lo = (sq - hi.astype(F32)).astype(BF16)
        ss = (jnp.dot(hi, ones_ref[...], preferred_element_type=F32)
              + jnp.dot(lo, ones_ref[...], preferred_element_type=F32))
        y = c2 * lax.rsqrt(ss * (1.0 / HEAD_DIM) + EPS)
        return y[:, :LANES] * gain_a, y[:, LANES:] * gain_b

    def rope(c):
        ahead = pltpu.roll(c, LANES - ROT, 1)
        behind = pltpu.roll(c, ROT, 1)
        return c * cos + jnp.where(pair_first, ahead, behind) * sin

    def finish_q(ref, c, pc, gain_t, rotary):
        for r in range(tm // ROW_BLOCK):
            cols = slice(r * ROW_BLOCK, (r + 1) * ROW_BLOCK)
            bt = p_ref[cols, pc * LANES:(pc + 1) * LANES].T
            halves = []
            for hh in range(LANES // HEAD_DIM):
                xh = bt[hh * HEAD_DIM:(hh + 1) * HEAD_DIM]
                ms = jnp.sum(xh * xh, axis=0, keepdims=True) * (1.0 / HEAD_DIM)
                halves.append(xh * lax.rsqrt(ms + EPS))
            y = jnp.concatenate(halves, axis=0) * gain_t
            if rotary:
                slabs = [y[t * ROT:(t + 1) * ROT] for t in range(LANES // ROT)]
                partner = jnp.concatenate(
                    [slabs[t ^ 1] for t in range(LANES // ROT)], axis=0)
                y = y * cost_ref[:, cols] + partner * sint_ref[:, cols]
            ref[r, c * LANES:(c + 1) * LANES, :] = y.astype(BF16)

    kn_a, kn_b = gain_ref[0:1, :], gain_ref[1:2, :]
    for c in range(4):
        finish_q(qta_ref, c, c, gq_ref[:LANES, :], True)
        finish_q(qtb_ref, c, 6 + c, gq_ref[LANES:, :], False)
    ka, kb = head_norm_pair(chunk(4), kn_a, chunk(10), kn_b)
    kk_ref[:, :LANES] = rope(ka).astype(BF16)
    kk_ref[:, LANES:] = kb.astype(BF16)
    pad_rows = VT_ROWS - HEAD_DIM
    ones_row = (lax.broadcasted_iota(jnp.int32, (pad_rows, tm), 0) == 0).astype(F32)
    for ref, c in ((vta_ref, 5), (vtb_ref, 11)):
        vt = chunk(c).T
        vt = jnp.concatenate(
            [vt[:HEAD_DIM], ones_row, vt[HEAD_DIM:], ones_row], axis=0).astype(BF16)
        if len(ref.shape) == 2:
            ref[...] = vt
        else:
            for r in range(tm // ROW_BLOCK):
                ref[r] = vt[:, r * ROW_BLOCK:(r + 1) * ROW_BLOCK]
    if project:
        h = _rms(x_ref[...], gn_ref[...]).astype(BF16)
        p_new[...] = jnp.dot(h, w_ref[...], preferred_element_type=F32)


def _qkv_call(x2, gn, w_in, layer, n_qkv, gains, gq, cos_t, sin_t, seq, tm):
    n, d = x2.shape
    cos_tt, sin_tt = cos_t.T, sin_t.T
    nseq = seq // tm
    qd = N_HEADS * HEAD_DIM
    head_ones = jnp.kron(jnp.eye(2 * LANES // HEAD_DIM, dtype=F32),
                         jnp.ones((HEAD_DIM, HEAD_DIM), F32)).astype(BF16)
    n_tiles = n // tm
    tile_in = lambda i: jnp.minimum(i, n_tiles - 1)
    tile_out = lambda i: jnp.maximum(i - 1, 0)
    qt_spec = pl.BlockSpec((tm // ROW_BLOCK, qd, ROW_BLOCK), lambda i: (tile_out(i), 0, 0))
    qt_shape = jax.ShapeDtypeStruct((n // ROW_BLOCK, qd, ROW_BLOCK), BF16)
    return pl.pallas_call(
        partial(_qkv_body, layer=layer, n_tiles=n_tiles),
        grid=(n_tiles + 1,),
        in_specs=[
            pl.BlockSpec((tm, d), lambda i: (tile_in(i), 0)),
            pl.BlockSpec((1, d), lambda i: (0, 0)),
            pl.BlockSpec(memory_space=pl.ANY),
            pl.BlockSpec(gains.shape, lambda i: (0, 0)),
            pl.BlockSpec(gq.shape, lambda i: (0, 0)),
            pl.BlockSpec((tm, LANES), lambda i: (tile_out(i) % nseq, 0)),
            pl.BlockSpec((tm, LANES), lambda i: (tile_out(i) % nseq, 0)),
            pl.BlockSpec((LANES, tm), lambda i: (0, tile_out(i) % nseq)),
            pl.BlockSpec((LANES, tm), lambda i: (0, tile_out(i) % nseq)),
            pl.BlockSpec(head_ones.shape, lambda i: (0, 0)),
        ],
        out_specs=[qt_spec, qt_spec,
                   pl.BlockSpec((tm, 2 * LANES), lambda i: (tile_out(i), 0)),
                   pl.BlockSpec((N_KV * VT_ROWS, tm), lambda i: (0, tile_out(i))),
                   pl.BlockSpec((tm // ROW_BLOCK, N_KV * VT_ROWS, ROW_BLOCK),
                                lambda i: (tile_out(i), 0, 0))],
        out_shape=[qt_shape, qt_shape,
                   jax.ShapeDtypeStruct((n, 2 * LANES), BF16),
                   jax.ShapeDtypeStruct((N_KV * VT_ROWS, n), BF16),
                   jax.ShapeDtypeStruct((n // ROW_BLOCK, N_KV * VT_ROWS, ROW_BLOCK), BF16)],
        scratch_shapes=[pltpu.VMEM((tm, n_qkv), F32)] * 2 + [
            pltpu.VMEM((d, n_qkv), BF16),
            pltpu.VMEM((STAGE_SLOTS, STAGE_ROWS, STAGE_COLS), F32),
            pltpu.SemaphoreType.DMA((STAGE_SLOTS,))],
        compiler_params=pltpu.CompilerParams(
            dimension_semantics=("arbitrary",), vmem_limit_bytes=VMEM_LIMIT),
        name="qkv_proj",
    )(x2, gn, w_in, gains, gq, cos_t, sin_t, cos_tt, sin_tt, head_ones)


def _query_rhs(q_ref, j, g):
    qt = jnp.concatenate(
        [q_ref[j, h * HEAD_DIM:(h + 1) * HEAD_DIM, :] for h in range(GROUP)], axis=1)
    zero = jnp.zeros_like(qt)
    return jnp.concatenate([jnp.where(g == 0, qt, zero), jnp.where(g == 0, zero, qt)], axis=0)


def _store_heads_t(o_ref, row0, o):
    rb = ROW_BLOCK
    o = jnp.concatenate([o[:, k * rb:(k + 1) * rb] for k in range(GROUP)], axis=0)
    o_ref[pl.ds(row0, rb), :] = o.T.astype(o_ref.dtype)


KEY_CHUNK = 256
GLOBAL_UNROLL = 6
PV_DELAY = 1
WINDOW_SLOTS = (0, 2, 4)


def _clamp_block(j, n_blocks):
    if isinstance(j, int):
        return min(max(j, 0), n_blocks - 1)
    return jnp.clip(j, 0, n_blocks - 1)


def _block_row(j):
    row0 = j * ROW_BLOCK
    return row0 if isinstance(row0, int) else pl.multiple_of(row0, ROW_BLOCK)


def _attn_body(qa_ref, ka_ref, vta_ref, qb_ref, kb_ref, vtb_ref, bias_ref, sink_ref,
               oa_ref, ob_ref, s0, s1, m0, m1, ws0, ws1, wm0, wm1):
    s_buf, m_buf = (s0, s1), (m0, m1)
    ws_buf, wm_buf = (ws0, ws1), (wm0, wm1)
    g = pl.program_id(1)
    n_blocks = qa_ref.shape[0]
    n_chunks = ka_ref.shape[0] // KEY_CHUNK
    w = WINDOW
    sink = sink_ref[...]

    def keys(c):
        return slice(c * KEY_CHUNK, (c + 1) * KEY_CHUNK)

    def window_scores(j, slot):
        kw = jnp.concatenate(
            [kb_ref[pl.ds(_block_row(_clamp_block(j + t, n_blocks)), w), :] for t in (-1, 0, 1)],
            axis=0)
        s = jnp.dot(kw, _query_rhs(qb_ref, j, g), preferred_element_type=F32) + bias_ref[...]
        rows = [s[t * w:(t + 1) * w] for t in range(3)]
        rows[0] = jnp.where(j > 0, rows[0], NEG_INF)
        rows[2] = jnp.where(j < n_blocks - 1, rows[2], NEG_INF)
        for t in range(3):
            ws_buf[slot][t * w:(t + 1) * w, :] = rows[t]
        top = jnp.maximum(jnp.maximum(rows[0], rows[1]), rows[2])
        wm_buf[slot][...] = jnp.maximum(jnp.max(top, axis=0, keepdims=True), sink)

    def window_probs(slot):
        return jnp.exp2(ws_buf[slot][...] - wm_buf[slot][...]).astype(BF16)

    def window_output(j, slot, p):
        vband = jnp.concatenate(
            [vtb_ref[_clamp_block(j + t, n_blocks)] for t in (-1, 0, 1)], axis=1)
        o = jnp.dot(vband, p, preferred_element_type=F32)
        denom = o[HEAD_DIM:HEAD_DIM + 1] + jnp.exp2(sink - wm_buf[slot][...])
        _store_heads_t(ob_ref, _block_row(j), o[:HEAD_DIM] / denom)

    def step(j_scores, j_finish, slot):
        m = None
        o = None
        pw = None
        pending = []
        if j_scores is not None:
            rhs = _query_rhs(qa_ref, j_scores, g)

        def accumulate(o, c, p):
            oc = jnp.dot(vta_ref[:, keys(c)], p, preferred_element_type=F32)
            return oc if o is None else o + oc

        for c in range(n_chunks):
            if j_scores is not None:
                s = jnp.dot(ka_ref[keys(c), :], rhs, preferred_element_type=F32)
                s_buf[slot][keys(c), :] = s
                mc = jnp.max(s, axis=0, keepdims=True)
                m = mc if m is None else jnp.maximum(m, mc)
            if j_finish is not None:
                p = jnp.exp2(s_buf[1 - slot][keys(c), :] - m_buf[1 - slot][...]).astype(BF16)
                pending.append((c, p))
                if len(pending) > PV_DELAY:
                    o = accumulate(o, *pending.pop(0))
            if c == WINDOW_SLOTS[0] and j_scores is not None:
                window_scores(j_scores, slot)
            if c == WINDOW_SLOTS[1] and j_finish is not None:
                pw = window_probs(1 - slot)
            if c == WINDOW_SLOTS[2] and j_finish is not None:
                window_output(j_finish, 1 - slot, pw)
        for item in pending:
            o = accumulate(o, *item)
        if j_scores is not None:
            m_buf[slot][...] = m
        if j_finish is not None:
            _store_heads_t(oa_ref, _block_row(j_finish), o[:HEAD_DIM] / o[HEAD_DIM:HEAD_DIM + 1])

    step(0, None, 0)
    unroll = GLOBAL_UNROLL
    n_loop = (n_blocks - 1) // unroll

    def steady(jj, carry):
        j = 1 + unroll * jj
        for u in range(unroll):
            step(j + u, j + u - 1, (1 + u) % 2)
        return carry

    lax.fori_loop(0, n_loop, steady, 0)
    for j in range(1 + unroll * n_loop, n_blocks):
        step(j, j - 1, j % 2)
    step(None, n_blocks - 1, n_blocks % 2)


def _attn_call(qt_a, qt_b, kk, vt_a, vt_b, bias, sink, batch, seq):
    n = kk.shape[0]
    w = WINDOW
    nblk = seq // ROW_BLOCK
    cols = GROUP * ROW_BLOCK
    assert seq % KEY_CHUNK == 0 and max(WINDOW_SLOTS) < seq // KEY_CHUNK and w == ROW_BLOCK
    q_spec = pl.BlockSpec((nblk, GROUP * HEAD_DIM, ROW_BLOCK), lambda b, g: (b, g, 0))
    o_spec = pl.BlockSpec((seq, GROUP * HEAD_DIM), lambda b, g: (b, g))
    o_shape = jax.ShapeDtypeStruct((n, N_HEADS * HEAD_DIM), BF16)
    return pl.pallas_call(
        _attn_body,
        grid=(batch, N_KV),
        in_specs=[
            q_spec,
            pl.BlockSpec((seq, LANES), lambda b, g: (b, 0)),
            pl.BlockSpec((VT_ROWS, seq), lambda b, g: (g, b)),
            q_spec,
            pl.BlockSpec((seq, LANES), lambda b, g: (b, 1)),
            pl.BlockSpec((nblk, VT_ROWS, ROW_BLOCK), lambda b, g: (b, g, 0)),
            pl.BlockSpec((None, 3 * w, cols), lambda b, g: (g, 0, 0)),
            pl.BlockSpec((None, 1, cols), lambda b, g: (g, 0, 0)),
        ],
        out_specs=[o_spec, o_spec],
        out_shape=[o_shape, o_shape],
        scratch_shapes=(
            [pltpu.VMEM((seq, cols), F32)] * 2 + [pltpu.VMEM((1, cols), F32)] * 2
            + [pltpu.VMEM((3 * w, cols), F32)] * 2 + [pltpu.VMEM((1, cols), F32)] * 2),
        compiler_params=pltpu.CompilerParams(
            dimension_semantics=("arbitrary",) * 2, vmem_limit_bytes=VMEM_LIMIT),
        name="attention",
    )(qt_a, kk, vt_a, qt_b, kk, vt_b, bias, sink)


STAGE_ROWS, STAGE_COLS = 256, 1024
STAGE_SLOTS = 4


def _stage_weights(jobs, stage, sem):
    chunks = []
    for src, row0, col0, dst in jobs:
        rows, cols = dst.shape
        width = STAGE_COLS if cols % STAGE_COLS == 0 else STAGE_COLS // 2
        assert rows % STAGE_ROWS == 0 and cols % width == 0
        for r in range(0, rows, STAGE_ROWS):
            for c in range(0, cols, width):
                chunks.append((src.at[pl.ds(row0 + r, STAGE_ROWS), pl.ds(col0 + c, width)],
                               dst.at[pl.ds(r, STAGE_ROWS), pl.ds(c, width)], width))

    n_slots = stage.shape[0]

    def copy(k):
        src, _, width = chunks[k]
        return pltpu.make_async_copy(src, stage.at[k % n_slots, :, pl.ds(0, width)],
                                     sem.at[k % n_slots])

    for k in range(min(n_slots - 1, len(chunks))):
        copy(k).start()
    for k in range(len(chunks)):
        if k + n_slots - 1 < len(chunks):
            copy(k + n_slots - 1).start()
        copy(k).wait()
        _, dst, width = chunks[k]
        dst[...] = stage[k % n_slots, :, :width].astype(BF16)


def _mix_mlp_body(x_ref, oa_ref, ob_ref, gn_ref, bg_ref, gm_ref,
                  win_hbm, woa_hbm, wob_hbm, wout_hbm, w1_hbm, w2_hbm, o_ref,
                  wg_ref, woa_ref, wob_ref, wout_ref, w1_ref, w2_ref, stage, sem, *, layer, col0):
    @pl.when(pl.program_id(0) == 0)
    def _():
        _stage_weights(
            [(win_hbm.at[layer], 0, col0, wg_ref), (woa_hbm.at[layer], 0, 0, woa_ref),
             (wob_hbm.at[layer], 0, 0, wob_ref), (wout_hbm.at[layer], 0, 0, wout_ref),
             (w1_hbm.at[layer], 0, 0, w1_ref), (w2_hbm.at[layer], 0, 0, w2_ref)],
            stage, sem)

    tm, d = x_ref.shape
    rows = [slice(k * tm // MLP_GROUPS, (k + 1) * tm // MLP_GROUPS) for k in range(MLP_GROUPS)]
    dot = partial(jnp.dot, preferred_element_type=F32)
    x = [x_ref[r, :] for r in rows]
    h = [_rms(v, gn_ref[...]).astype(BF16) for v in x]
    g = [jax.nn.sigmoid(dot(v, wg_ref[...]) + bg_ref[...]) for v in h]
    ya = [dot(oa_ref[r, :], woa_ref[...]) for r in rows]
    yb = [dot(ob_ref[r, :], wob_ref[...]) for r in rows]
    mixed = [(gk[:, :d] * a + gk[:, d:] * b).astype(BF16) for gk, a, b in zip(g, ya, yb)]
    x = [v + dot(mk, wout_ref[...]) for v, mk in zip(x, mixed)]
    h = [_rms(v, gm_ref[...]).astype(BF16) for v in x]
    u = [jnp.square(jnp.maximum(dot(v, w1_ref[...]), 0.0)).astype(BF16) for v in h]
    for r, v, uk in zip(rows, x, u):
        o_ref[r, :] = v + dot(uk, w2_ref[...])


def _resident(arr):
    return pl.BlockSpec(arr.shape, lambda i: (0,) * arr.ndim, pipeline_mode=pl.Buffered(1))


def _mix_mlp_call(x2, oa, ob, gn, bg, gm, w_in, w_o_a, w_o_b, w_out, w_mlp1, w_mlp2,
                  layer, gate_col0, tm):
    n, d = x2.shape
    row = lambda width: pl.BlockSpec((tm, width), lambda i: (i, 0))
    consts = (gn, bg, gm)
    weights = (w_in, w_o_a, w_o_b, w_out, w_mlp1, w_mlp2)
    n_gate = w_in.shape[2] - gate_col0
    bf16_shapes = [(d, n_gate)] + [w.shape[1:] for w in weights[1:]]
    return pl.pallas_call(
        partial(_mix_mlp_body, layer=layer, col0=gate_col0),
        grid=(n // tm,),
        in_specs=([row(d), row(oa.shape[1]), row(ob.shape[1])] + [_resident(a) for a in consts]
                  + [pl.BlockSpec(memory_space=pl.ANY)] * len(weights)),
        out_specs=row(d),
        out_shape=jax.ShapeDtypeStruct((n, d), F32),
        scratch_shapes=([pltpu.VMEM(shape, BF16) for shape in bf16_shapes]
                        + [pltpu.VMEM((STAGE_SLOTS, STAGE_ROWS, STAGE_COLS), F32),
                           pltpu.SemaphoreType.DMA((STAGE_SLOTS,))]),
        compiler_params=pltpu.CompilerParams(
            dimension_semantics=("arbitrary",), vmem_limit_bytes=VMEM_LIMIT),
        name="mix_mlp",
    )(x2, oa, ob, *consts, *weights)


def _rope_tables(seq):
    rows = seq // GRID_W
    row = jnp.repeat(jnp.arange(rows, dtype=jnp.int32), GRID_W)
    col = jnp.tile(jnp.arange(GRID_W, dtype=jnp.int32), rows)
    n_freq = HEAD_DIM // 4
    inv_freq = ROPE_THETA ** (-jnp.arange(n_freq, dtype=F32) / n_freq)
    ang_row = row.astype(F32)[:, None] * inv_freq[None, :]
    ang_col = col.astype(F32)[:, None] * inv_freq[None, :]
    cr, sr, cc, sc = jnp.cos(ang_row), jnp.sin(ang_row), jnp.cos(ang_col), jnp.sin(ang_col)
    cos64 = jnp.concatenate([cr, cr, cc, cc], axis=-1)
    sin64 = jnp.concatenate([-sr, sr, -sc, sc], axis=-1)
    return jnp.tile(cos64, (1, 2)), jnp.tile(sin64, (1, 2))


def _t5_bucket(rel):
    nb = N_BUCKETS // 2
    max_exact = nb // 2
    n = jnp.abs(rel)
    large = max_exact + (jnp.log(jnp.maximum(n, 1).astype(F32) / max_exact)
                         / math.log(MAX_DISTANCE / max_exact) * (nb - max_exact)).astype(jnp.int32)
    large = jnp.minimum(large, nb - 1)
    return jnp.where(rel > 0, nb, 0) + jnp.where(n < max_exact, n, large)


def _window_bias(rel_bias):
    w = WINDOW
    period = 4 * w
    slot = jnp.arange(period, dtype=jnp.int32)
    rel = jnp.where(slot < 3 * w, slot - w, slot - 5 * w)
    table = rel_bias.astype(F32)[_t5_bucket(rel)].T * LOG2E
    table = jnp.where(jnp.abs(rel)[None] <= w, table, NEG_INF)
    band = jnp.tile(table, (1, w))[:, :w * (period - 1)].reshape(-1, w, period - 1)
    band = band[:, :, :3 * w]
    return jnp.transpose(band.reshape(N_KV, GROUP * w, 3 * w), (0, 2, 1))


def kernel(x, w_in, b_gate, qn_a, kn_a, qn_b, kn_b, w_o_a, w_o_b, w_out,
           sink_b, rel_bias, norm_mix, norm_mlp, w_mlp1, w_mlp2):
    batch, seq, d = x.shape
    depth = w_in.shape[0]
    n_qkv = 2 * (N_HEADS + 2 * N_KV) * HEAD_DIM
    cos_t, sin_t = _rope_tables(seq)
    bias = _window_bias(rel_bias)
    x2 = x.reshape(batch * seq, d)
    pair = lambda v: jnp.tile(v, 2)
    for l in range(depth):
        gains = jnp.zeros((SUBLANES, LANES), F32).at[:2].set(
            jnp.stack([pair(kn_a[l]), pair(kn_b[l])]))
        gq = jnp.concatenate([
            jnp.broadcast_to((pair(qn) * Q_SCALE)[:, None], (LANES, LANES))
            for qn in (qn_a[l], qn_b[l])], axis=0)
        qt_a, qt_b, kk, vt_a, vt_b = _qkv_call(
            x2, norm_mix[l][None, :], w_in, l, n_qkv,
            gains, gq, cos_t, sin_t, seq, tm=QKV_TILE)
        sink = jnp.repeat(sink_b[l].astype(F32) * LOG2E, WINDOW).reshape(
            N_KV, 1, GROUP * WINDOW)
        oa, ob = _attn_call(qt_a, qt_b, kk, vt_a, vt_b, bias, sink, batch, seq)
        x2 = _mix_mlp_call(
            x2, oa, ob, norm_mix[l][None, :], b_gate[l][None, :], norm_mlp[l][None, :],
            w_in, w_o_a, w_o_b, w_out, w_mlp1, w_mlp2,
            layer=l, gate_col0=n_qkv, tm=MLP_TILE)
    return x2.reshape(batch, seq, d)
```

```python
import math
from functools import partial

import jax
import jax.numpy as jnp
from jax import lax
from jax.experimental import pallas as pl
from jax.experimental.pallas import tpu as pltpu

F32 = jnp.float32
BF16 = jnp.bfloat16

HEAD_DIM = 64
N_HEADS = 8
N_KV = 2
GROUP = N_HEADS // N_KV
GRID_W = 64
ROPE_THETA = 10000.0
WINDOW = 128
N_BUCKETS = 32
MAX_DISTANCE = 128
EPS = 1e-6
NEG_INF = -1e30

LANES = 128
VMEM_LIMIT = 56 * 1024 * 1024
ROT = HEAD_DIM // 4

QKV_TILE = 1024
MLP_TILE = 512
MLP_GROUPS = 2

ROW_BLOCK = 128
VT_ROWS = 80
LOG2E = math.log2(math.e)
Q_SCALE = HEAD_DIM ** -0.5 * LOG2E


def _rms(x, gain):
    ms = jnp.mean(x * x, axis=-1, keepdims=True)
    return x * lax.rsqrt(ms + EPS) * gain


def _qkv_body(*refs, layer, n_tiles):
    *io_refs, p0, p1, w_ref, stage, sem = refs
    win_hbm = io_refs[2]
    i = pl.program_id(0)
    slots = (p0, p1)

    @pl.when(i == 0)
    def _():
        p1[...] = jnp.zeros_like(p1)
        _stage_weights([(win_hbm.at[layer], 0, 0, w_ref)], stage, sem)

    for parity in range(2):
        @pl.when(jnp.logical_and(i % 2 == parity, i < n_tiles))
        def _():
            _qkv_step(*io_refs, w_ref, slots[parity], slots[1 - parity], project=True)

    @pl.when(i == n_tiles)
    def _():
        _qkv_step(*io_refs, w_ref, None, slots[1 - n_tiles % 2], project=False)


def _qkv_step(x_ref, gn_ref, _, gains_ref, cost_ref, sint_ref,
              qta_ref, qtb_ref, kk_ref, vta_ref, vtb_ref, w_ref, p_new, p_ref, *, project):
    tm = x_ref.shape[0]

    def chunk(c):
        return p_ref[:, c * LANES:(c + 1) * LANES]

    def normed_block(pc, r, table, rotary):
        cols = slice(r * ROW_BLOCK, (r + 1) * ROW_BLOCK)
        bt = p_ref[cols, pc * LANES:(pc + 1) * LANES].T
        halves = []
        for hh in range(LANES // HEAD_DIM):
            xh = bt[hh * HEAD_DIM:(hh + 1) * HEAD_DIM]
            ms = jnp.sum(xh * xh, axis=0, keepdims=True) * (1.0 / HEAD_DIM)
            halves.append(xh * lax.rsqrt(ms + EPS))
        y = jnp.concatenate(halves, axis=0) * gains_ref[table * LANES:(table + 1) * LANES, :]
        if rotary:
            slabs = [y[t * ROT:(t + 1) * ROT] for t in range(LANES // ROT)]
            partner = jnp.concatenate([slabs[t ^ 1] for t in range(LANES // ROT)], axis=0)
            y = y * cost_ref[:, cols] + partner * sint_ref[:, cols]
        return y

    for r in range(tm // ROW_BLOCK):
        rows = slice(r * ROW_BLOCK, (r + 1) * ROW_BLOCK)
        for c in range(4):
            qta_ref[r, c * LANES:(c + 1) * LANES, :] = normed_block(c, r, 0, True).astype(BF16)
            qtb_ref[r, c * LANES:(c + 1) * LANES, :] = normed_block(6 + c, r, 1, False).astype(BF16)
        kk_ref[rows, :LANES] = normed_block(4, r, 2, True).T.astype(BF16)
        kk_ref[rows, LANES:] = normed_block(10, r, 3, False).T.astype(BF16)
    pad_rows = VT_ROWS - HEAD_DIM
    ones_row = (lax.broadcasted_iota(jnp.int32, (pad_rows, tm), 0) == 0).astype(F32)
    for ref, c in ((vta_ref, 5), (vtb_ref, 11)):
        vt = chunk(c).T
        vt = jnp.concatenate(
            [vt[:HEAD_DIM], ones_row, vt[HEAD_DIM:], ones_row], axis=0).astype(BF16)
        if len(ref.shape) == 2:
            ref[...] = vt
        else:
            for r in range(tm // ROW_BLOCK):
                ref[r] = vt[:, r * ROW_BLOCK:(r + 1) * ROW_BLOCK]
    if project:
        h = _rms(x_ref[...], gn_ref[...]).astype(BF16)
        p_new[...] = jnp.dot(h, w_ref[...], preferred_element_type=F32)


def _qkv_call(x2, gn, w_in, layer, n_qkv, gains, cos_tt, sin_tt, seq, tm):
    n, d = x2.shape
    nseq = seq // tm
    qd = N_HEADS * HEAD_DIM
    n_tiles = n // tm
    tile_in = lambda i: jnp.minimum(i, n_tiles - 1)
    tile_out = lambda i: jnp.maximum(i - 1, 0)
    qt_spec = pl.BlockSpec((tm // ROW_BLOCK, qd, ROW_BLOCK), lambda i: (tile_out(i), 0, 0))
    qt_shape = jax.ShapeDtypeStruct((n // ROW_BLOCK, qd, ROW_BLOCK), BF16)
    return pl.pallas_call(
        partial(_qkv_body, layer=layer, n_tiles=n_tiles),
        grid=(n_tiles + 1,),
        in_specs=[
            pl.BlockSpec((tm, d), lambda i: (tile_in(i), 0)),
            pl.BlockSpec((1, d), lambda i: (0, 0)),
            pl.BlockSpec(memory_space=pl.ANY),
            pl.BlockSpec(gains.shape, lambda i: (0, 0)),
            pl.BlockSpec((LANES, tm), lambda i: (0, tile_out(i) % nseq)),
            pl.BlockSpec((LANES, tm), lambda i: (0, tile_out(i) % nseq)),
        ],
        out_specs=[qt_spec, qt_spec,
                   pl.BlockSpec((tm, 2 * LANES), lambda i: (tile_out(i), 0)),
                   pl.BlockSpec((N_KV * VT_ROWS, tm), lambda i: (0, tile_out(i))),
                   pl.BlockSpec((tm // ROW_BLOCK, N_KV * VT_ROWS, ROW_BLOCK),
                                lambda i: (tile_out(i), 0, 0))],
        out_shape=[qt_shape, qt_shape,
                   jax.ShapeDtypeStruct((n, 2 * LANES), BF16),
                   jax.ShapeDtypeStruct((N_KV * VT_ROWS, n), BF16),
                   jax.ShapeDtypeStruct((n // ROW_BLOCK, N_KV * VT_ROWS, ROW_BLOCK), BF16)],
        scratch_shapes=[pltpu.VMEM((tm, n_qkv), F32)] * 2 + [
            pltpu.VMEM((d, n_qkv), BF16),
            pltpu.VMEM((STAGE_SLOTS, STAGE_ROWS, STAGE_COLS), F32),
            pltpu.SemaphoreType.DMA((STAGE_SLOTS,))],
        compiler_params=pltpu.CompilerParams(
            dimension_semantics=("arbitrary",), vmem_limit_bytes=VMEM_LIMIT),
        name="qkv_proj",
    )(x2, gn, w_in, gains, cos_tt, sin_tt)


def _query_rhs(q_ref, j, g):
    qt = jnp.concatenate(
        [q_ref[j, h * HEAD_DIM:(h + 1) * HEAD_DIM, :] for h in range(GROUP)], axis=1)
    zero = jnp.zeros_like(qt)
    return jnp.concatenate([jnp.where(g == 0, qt, zero), jnp.where(g == 0, zero, qt)], axis=0)


def _store_heads_t(o_ref, row0, o):
    rb = ROW_BLOCK
    o = jnp.concatenate([o[:, k * rb:(k + 1) * rb] for k in range(GROUP)], axis=0)
    o_ref[pl.ds(row0, rb), :] = o.T.astype(o_ref.dtype)


KEY_CHUNK = 256
GLOBAL_UNROLL = 6
PV_DELAY = 1
WINDOW_SLOTS = (0, 2, 4)


def _clamp_block(j, n_blocks):
    if isinstance(j, int):
        return min(max(j, 0), n_blocks - 1)
    return jnp.clip(j, 0, n_blocks - 1)


def _block_row(j):
    row0 = j * ROW_BLOCK
    return row0 if isinstance(row0, int) else pl.multiple_of(row0, ROW_BLOCK)


def _attn_body(qa_ref, ka_ref, vta_ref, qb_ref, kb_ref, vtb_ref, bias_ref, sink_ref,
               oa_ref, ob_ref, s0, s1, m0, m1, ws0, ws1, wm0, wm1):
    s_buf, m_buf = (s0, s1), (m0, m1)
    ws_buf, wm_buf = (ws0, ws1), (wm0, wm1)
    g = pl.program_id(1)
    n_blocks = qa_ref.shape[0]
    n_chunks = ka_ref.shape[0] // KEY_CHUNK
    w = WINDOW
    sink = sink_ref[...]

    def keys(c):
        return slice(c * KEY_CHUNK, (c + 1) * KEY_CHUNK)

    def window_scores(j, slot):
        kw = jnp.concatenate(
            [kb_ref[pl.ds(_block_row(_clamp_block(j + t, n_blocks)), w), :] for t in (-1, 0, 1)],
            axis=0)
        s = jnp.dot(kw, _query_rhs(qb_ref, j, g), preferred_element_type=F32) + bias_ref[...]
        rows = [s[t * w:(t + 1) * w] for t in range(3)]
        rows[0] = jnp.where(j > 0, rows[0], NEG_INF)
        rows[2] = jnp.where(j < n_blocks - 1, rows[2], NEG_INF)
        for t in range(3):
            ws_buf[slot][t * w:(t + 1) * w, :] = rows[t]
        top = jnp.maximum(jnp.maximum(rows[0], rows[1]), rows[2])
        wm_buf[slot][...] = jnp.maximum(jnp.max(top, axis=0, keepdims=True), sink)

    def window_probs(slot):
        return jnp.exp2(ws_buf[slot][...] - wm_buf[slot][...]).astype(BF16)

    def window_output(j, slot, p):
        vband = jnp.concatenate(
            [vtb_ref[_clamp_block(j + t, n_blocks)] for t in (-1, 0, 1)], axis=1)
        o = jnp.dot(vband, p, preferred_element_type=F32)
        denom = o[HEAD_DIM:HEAD_DIM + 1] + jnp.exp2(sink - wm_buf[slot][...])
        _store_heads_t(ob_ref, _block_row(j), o[:HEAD_DIM] / denom)

    def step(j_scores, j_finish, slot):
        m = None
        o = None
        pw = None
        pending = []
        if j_scores is not None:
            rhs = _query_rhs(qa_ref, j_scores, g)

        def accumulate(o, c, p):
            oc = jnp.dot(vta_ref[:, keys(c)], p, preferred_element_type=F32)
            return oc if o is None else o + oc

        for c in range(n_chunks):
            if j_scores is not None:
                s = jnp.dot(ka_ref[keys(c), :], rhs, preferred_element_type=F32)
                s_buf[slot][keys(c), :] = s
                mc = jnp.max(s, axis=0, keepdims=True)
                m = mc if m is None else jnp.maximum(m, mc)
            if j_finish is not None:
                p = jnp.exp2(s_buf[1 - slot][keys(c), :] - m_buf[1 - slot][...]).astype(BF16)
                pending.append((c, p))
                if len(pending) > PV_DELAY:
                    o = accumulate(o, *pending.pop(0))
            if c == WINDOW_SLOTS[0] and j_scores is not None:
                window_scores(j_scores, slot)
            if c == WINDOW_SLOTS[1] and j_finish is not None:
                pw = window_probs(1 - slot)
            if c == WINDOW_SLOTS[2] and j_finish is not None:
                window_output(j_finish, 1 - slot, pw)
        for item in pending:
            o = accumulate(o, *item)
        if j_scores is not None:
            m_buf[slot][...] = m
        if j_finish is not None:
            _store_heads_t(oa_ref, _block_row(j_finish), o[:HEAD_DIM] / o[HEAD_DIM:HEAD_DIM + 1])

    step(0, None, 0)
    unroll = GLOBAL_UNROLL
    n_loop = (n_blocks - 1) // unroll

    def steady(jj, carry):
        j = 1 + unroll * jj
        for u in range(unroll):
            step(j + u, j + u - 1, (1 + u) % 2)
        return carry

    lax.fori_loop(0, n_loop, steady, 0)
    for j in range(1 + unroll * n_loop, n_blocks):
        step(j, j - 1, j % 2)
    step(None, n_blocks - 1, n_blocks % 2)


def _attn_call(qt_a, qt_b, kk, vt_a, vt_b, bias, sink, batch, seq):
    n = kk.shape[0]
    w = WINDOW
    nblk = seq // ROW_BLOCK
    cols = GROUP * ROW_BLOCK
    assert seq % KEY_CHUNK == 0 and max(WINDOW_SLOTS) < seq // KEY_CHUNK and w == ROW_BLOCK
    q_spec = pl.BlockSpec((nblk, GROUP * HEAD_DIM, ROW_BLOCK), lambda b, g: (b, g, 0))
    o_spec = pl.BlockSpec((seq, GROUP * HEAD_DIM), lambda b, g: (b, g))
    o_shape = jax.ShapeDtypeStruct((n, N_HEADS * HEAD_DIM), BF16)
    return pl.pallas_call(
        _attn_body,
        grid=(batch, N_KV),
        in_specs=[
            q_spec,
            pl.BlockSpec((seq, LANES), lambda b, g: (b, 0)),
            pl.BlockSpec((VT_ROWS, seq), lambda b, g: (g, b)),
            q_spec,
            pl.BlockSpec((seq, LANES), lambda b, g: (b, 1)),
            pl.BlockSpec((nblk, VT_ROWS, ROW_BLOCK), lambda b, g: (b, g, 0)),
            pl.BlockSpec((None, 3 * w, cols), lambda b, g: (g, 0, 0)),
            pl.BlockSpec((None, 1, cols), lambda b, g: (g, 0, 0)),
        ],
        out_specs=[o_spec, o_spec],
        out_shape=[o_shape, o_shape],
        scratch_shapes=(
            [pltpu.VMEM((seq, cols), F32)] * 2 + [pltpu.VMEM((1, cols), F32)] * 2
            + [pltpu.VMEM((3 * w, cols), F32)] * 2 + [pltpu.VMEM((1, cols), F32)] * 2),
        compiler_params=pltpu.CompilerParams(
            dimension_semantics=("arbitrary",) * 2, vmem_limit_bytes=VMEM_LIMIT),
        name="attention",
    )(qt_a, kk, vt_a, qt_b, kk, vt_b, bias, sink)


STAGE_ROWS, STAGE_COLS = 256, 1024
STAGE_SLOTS = 4


def _stage_weights(jobs, stage, sem):
    chunks = []
    for src, row0, col0, dst in jobs:
        rows, cols = dst.shape
        width = STAGE_COLS if cols % STAGE_COLS == 0 else STAGE_COLS // 2
        assert rows % STAGE_ROWS == 0 and cols % width == 0
        for r in range(0, rows, STAGE_ROWS):
            for c in range(0, cols, width):
                chunks.append((src.at[pl.ds(row0 + r, STAGE_ROWS), pl.ds(col0 + c, width)],
                               dst.at[pl.ds(r, STAGE_ROWS), pl.ds(c, width)], width))

    n_slots = stage.shape[0]

    def copy(k):
        src, _, width = chunks[k]
        return pltpu.make_async_copy(src, stage.at[k % n_slots, :, pl.ds(0, width)],
                                     sem.at[k % n_slots])

    for k in range(min(n_slots - 1, len(chunks))):
        copy(k).start()
    for k in range(len(chunks)):
        if k + n_slots - 1 < len(chunks):
            copy(k + n_slots - 1).start()
        copy(k).wait()
        _, dst, width = chunks[k]
        dst[...] = stage[k % n_slots, :, :width].astype(BF16)


def _mix_mlp_body(x_ref, oa_ref, ob_ref, gn_ref, bg_ref, gm_ref,
                  win_hbm, woa_hbm, wob_hbm, wout_hbm, w1_hbm, w2_hbm, o_ref,
                  wg_ref, woa_ref, wob_ref, wout_ref, w1_ref, w2_ref, stage, sem, *, layer, col0):
    @pl.when(pl.program_id(0) == 0)
    def _():
        _stage_weights(
            [(win_hbm.at[layer], 0, col0, wg_ref), (woa_hbm.at[layer], 0, 0, woa_ref),
             (wob_hbm.at[layer], 0, 0, wob_ref), (wout_hbm.at[layer], 0, 0, wout_ref),
             (w1_hbm.at[layer], 0, 0, w1_ref), (w2_hbm.at[layer], 0, 0, w2_ref)],
            stage, sem)

    tm, d = x_ref.shape
    rows = [slice(k * tm // MLP_GROUPS, (k + 1) * tm // MLP_GROUPS) for k in range(MLP_GROUPS)]
    dot = partial(jnp.dot, preferred_element_type=F32)
    x = [x_ref[r, :] for r in rows]
    h = [_rms(v, gn_ref[...]).astype(BF16) for v in x]
    g = [jax.nn.sigmoid(dot(v, wg_ref[...]) + bg_ref[...]) for v in h]
    ya = [dot(oa_ref[r, :], woa_ref[...]) for r in rows]
    yb = [dot(ob_ref[r, :], wob_ref[...]) for r in rows]
    mixed = [(gk[:, :d] * a + gk[:, d:] * b).astype(BF16) for gk, a, b in zip(g, ya, yb)]
    x = [v + dot(mk, wout_ref[...]) for v, mk in zip(x, mixed)]
    h = [_rms(v, gm_ref[...]).astype(BF16) for v in x]
    u = [jnp.square(jnp.maximum(dot(v, w1_ref[...]), 0.0)).astype(BF16) for v in h]
    for r, v, uk in zip(rows, x, u):
        o_ref[r, :] = v + dot(uk, w2_ref[...])


def _resident(arr):
    return pl.BlockSpec(arr.shape, lambda i: (0,) * arr.ndim, pipeline_mode=pl.Buffered(1))


def _mix_mlp_call(x2, oa, ob, gn, bg, gm, w_in, w_o_a, w_o_b, w_out, w_mlp1, w_mlp2,
                  layer, gate_col0, tm):
    n, d = x2.shape
    row = lambda width: pl.BlockSpec((tm, width), lambda i: (i, 0))
    consts = (gn, bg, gm)
    weights = (w_in, w_o_a, w_o_b, w_out, w_mlp1, w_mlp2)
    n_gate = w_in.shape[2] - gate_col0
    bf16_shapes = [(d, n_gate)] + [w.shape[1:] for w in weights[1:]]
    return pl.pallas_call(
        partial(_mix_mlp_body, layer=layer, col0=gate_col0),
        grid=(n // tm,),
        in_specs=([row(d), row(oa.shape[1]), row(ob.shape[1])] + [_resident(a) for a in consts]
                  + [pl.BlockSpec(memory_space=pl.ANY)] * len(weights)),
        out_specs=row(d),
        out_shape=jax.ShapeDtypeStruct((n, d), F32),
        scratch_shapes=([pltpu.VMEM(shape, BF16) for shape in bf16_shapes]
                        + [pltpu.VMEM((STAGE_SLOTS, STAGE_ROWS, STAGE_COLS), F32),
                           pltpu.SemaphoreType.DMA((STAGE_SLOTS,))]),
        compiler_params=pltpu.CompilerParams(
            dimension_semantics=("arbitrary",), vmem_limit_bytes=VMEM_LIMIT),
        name="mix_mlp",
    )(x2, oa, ob, *consts, *weights)


def _rope_tables(seq):
    rows = seq // GRID_W
    row = jnp.repeat(jnp.arange(rows, dtype=jnp.int32), GRID_W)
    col = jnp.tile(jnp.arange(GRID_W, dtype=jnp.int32), rows)
    n_freq = HEAD_DIM // 4
    inv_freq = ROPE_THETA ** (-jnp.arange(n_freq, dtype=F32) / n_freq)
    ang_row = row.astype(F32)[:, None] * inv_freq[None, :]
    ang_col = col.astype(F32)[:, None] * inv_freq[None, :]
    cr, sr, cc, sc = jnp.cos(ang_row), jnp.sin(ang_row), jnp.cos(ang_col), jnp.sin(ang_col)
    cos64 = jnp.concatenate([cr, cr, cc, cc], axis=-1)
    sin64 = jnp.concatenate([-sr, sr, -sc, sc], axis=-1)
    return jnp.tile(cos64, (1, 2)).T, jnp.tile(sin64, (1, 2)).T


def _t5_bucket(rel):
    nb = N_BUCKETS // 2
    max_exact = nb // 2
    n = jnp.abs(rel)
    large = max_exact + (jnp.log(jnp.maximum(n, 1).astype(F32) / max_exact)
                         / math.log(MAX_DISTANCE / max_exact) * (nb - max_exact)).astype(jnp.int32)
    large = jnp.minimum(large, nb - 1)
    return jnp.where(rel > 0, nb, 0) + jnp.where(n < max_exact, n, large)


def _window_bias(rel_bias):
    w = WINDOW
    period = 4 * w
    slot = jnp.arange(period, dtype=jnp.int32)
    rel = jnp.where(slot < 3 * w, slot - w, slot - 5 * w)
    table = rel_bias.astype(F32)[_t5_bucket(rel)].T * LOG2E
    table = jnp.where(jnp.abs(rel)[None] <= w, table, NEG_INF)
    band = jnp.tile(table, (1, w))[:, :w * (period - 1)].reshape(-1, w, period - 1)
    band = band[:, :, :3 * w]
    return jnp.transpose(band.reshape(N_KV, GROUP * w, 3 * w), (0, 2, 1))


def kernel(x, w_in, b_gate, qn_a, kn_a, qn_b, kn_b, w_o_a, w_o_b, w_out,
           sink_b, rel_bias, norm_mix, norm_mlp, w_mlp1, w_mlp2):
    batch, seq, d = x.shape
    depth = w_in.shape[0]
    n_qkv = 2 * (N_HEADS + 2 * N_KV) * HEAD_DIM
    cos_tt, sin_tt = _rope_tables(seq)
    bias = _window_bias(rel_bias)
    x2 = x.reshape(batch * seq, d)
    for l in range(depth):
        gains = jnp.concatenate([
            jnp.broadcast_to((jnp.tile(gain, 2) * scale)[:, None], (LANES, LANES))
            for gain, scale in ((qn_a[l], Q_SCALE), (qn_b[l], Q_SCALE),
                                (kn_a[l], 1.0), (kn_b[l], 1.0))], axis=0)
        qt_a, qt_b, kk, vt_a, vt_b = _qkv_call(
            x2, norm_mix[l][None, :], w_in, l, n_qkv, gains, cos_tt, sin_tt, seq, tm=QKV_TILE)
        sink = jnp.repeat(sink_b[l].astype(F32) * LOG2E, WINDOW).reshape(
            N_KV, 1, GROUP * WINDOW)
        oa, ob = _attn_call(qt_a, qt_b, kk, vt_a, vt_b, bias, sink, batch, seq)
        x2 = _mix_mlp_call(
            x2, oa, ob, norm_mix[l][None, :], b_gate[l][None, :], norm_mlp[l][None, :],
            w_in, w_o_a, w_o_b, w_out, w_mlp1, w_mlp2,
            layer=l, gate_col0=n_qkv, tm=MLP_TILE)
    return x2.reshape(batch, seq, d)
```
